```python
import jax, jax.numpy as jnp
from jax import lax
import numpy as np

D_MODEL = 1024
BATCH = 1
SEQ = 16384
DEPTH = 1
DEC_BATCH = 32
DEC_SEQ = 16
PAST_LEN = 2048

CHUNK = 64
Q_BLOCK = 128
MIX_WIDTH = D_MODEL
CONV_DIM = MIX_WIDTH // 2
ATTN_DIM = MIX_WIDTH - CONV_DIM
QK_DIM = 64
N_HEADS = ATTN_DIM // (2 * QK_DIM)
V_DIM = 2 * QK_DIM
CONV_W = 3
D_FF = 4 * D_MODEL
LN_EPS = 1e-5
ALPHA = (2 * DEPTH) ** 0.25
BETA = (8 * DEPTH) ** -0.25
QK_COLS = N_HEADS * 2 * QK_DIM
V_COLS = N_HEADS * V_DIM
IN_WIDTH = 3 * CONV_DIM + 2 * QK_COLS + V_COLS

kernel_name = "hybrid_conv_diffattn_stream_step"


def layer_norm(x, g, b):
    xf = x.astype(jnp.float32)
    mu = xf.mean(-1, keepdims=True)
    var = jnp.square(xf - mu).mean(-1, keepdims=True)
    return ((xf - mu) * lax.rsqrt(var + LN_EPS) * g.astype(jnp.float32) + b.astype(jnp.float32)).astype(x.dtype)


def rms_norm(x, g):
    xf = x.astype(jnp.float32)
    ms = jnp.square(xf).mean(-1, keepdims=True)
    return (xf * lax.rsqrt(ms + LN_EPS) * g.astype(jnp.float32)).astype(x.dtype)


def project(x, w_in):
    B, S, _ = x.shape
    z = jnp.einsum('bsd,de->bse', x, w_in)
    o1, o2, o3 = CONV_DIM, 2 * CONV_DIM, 3 * CONV_DIM
    o4, o5 = o3 + QK_COLS, o3 + 2 * QK_COLS
    gate_b = z[..., :o1]
    u = z[..., o1:o2] * z[..., o2:o3]
    q = z[..., o3:o4].reshape(B, S, N_HEADS, 2, QK_DIM)
    k = z[..., o4:o5].reshape(B, S, N_HEADS, 2, QK_DIM)
    v = z[..., o5:].reshape(B, S, N_HEADS, V_DIM)
    return gate_b, u, q, k, v


def causal_conv(u_ext, w):
    L = u_ext.shape[1] - (CONV_W - 1)
    return sum(w[j] * u_ext[:, j:j + L] for j in range(CONV_W))


def diff_attn_block(q, q_pos, k, v, k_pos, lam):
    s = jnp.einsum('bqhcd,bkhcd->bhcqk', q * (QK_DIM ** -0.5), k).astype(jnp.float32)
    mask = (k_pos[None, :] // CHUNK) <= (q_pos[:, None] // CHUNK)
    s = jnp.where(mask, s, -jnp.inf)
    p = jax.nn.softmax(s, axis=-1)
    a = p[:, :, 0] - lam * p[:, :, 1]
    return jnp.einsum('bhqk,bkhd->bqhd', a.astype(v.dtype), v)


def prompt_attention(q, k, v, lam):
    B, S = q.shape[:2]
    nb = S // Q_BLOCK
    pos = jnp.arange(S, dtype=jnp.int32)
    qb = q.reshape(B, nb, Q_BLOCK, N_HEADS, 2, QK_DIM).transpose(1, 0, 2, 3, 4, 5)
    pb = pos.reshape(nb, Q_BLOCK)
    out = lax.map(lambda a: diff_attn_block(a[0], a[1], k, v, pos, lam), (qb, pb))
    return out.transpose(1, 0, 2, 3, 4).reshape(B, S, N_HEADS, V_DIM)


def finish_layer(x, gate_b, conv_out, attn_o, lambda_init, subln_g, w_out,
                 ln1_g, ln1_b, w_ff1, w_ff2, ln2_g, ln2_b):
    B, S, _ = x.shape
    y_conv = gate_b * conv_out
    attn_o = rms_norm(attn_o, subln_g) * (1.0 - lambda_init)
    mixed = jnp.concatenate([y_conv, attn_o.reshape(B, S, ATTN_DIM)], axis=-1)
    x1 = layer_norm(ALPHA * x + jnp.einsum('bse,ed->bsd', mixed, w_out), ln1_g, ln1_b)
    hdn = jnp.square(jax.nn.relu(jnp.einsum('bsd,df->bsf', x1, w_ff1)))
    return layer_norm(ALPHA * x1 + jnp.einsum('bsf,fd->bsd', hdn, w_ff2), ln2_g, ln2_b)


def setup_inputs(seed: int = 0) -> dict:
    key = jax.random.key(seed)
    ks = jax.random.split(key, 20)
    nrm = lambda k, shp: jax.random.normal(k, shp, jnp.float32)
    col_scale = jnp.concatenate([jnp.ones((IN_WIDTH - V_COLS,), jnp.float32),
                                 jnp.full((V_COLS,), BETA, jnp.float32)])
    return {
        "x_prompt": nrm(ks[0], (BATCH, SEQ, D_MODEL)),
        "x_sample": nrm(ks[1], (DEC_BATCH, DEC_SEQ, D_MODEL)),
        "cache_k": nrm(ks[2], (DEPTH, DEC_BATCH, PAST_LEN, N_HEADS, 2 * QK_DIM)),
        "cache_v": BETA * nrm(ks[3], (DEPTH, DEC_BATCH, PAST_LEN, N_HEADS, V_DIM)),
        "state_conv": nrm(ks[4], (DEPTH, DEC_BATCH, CONV_W - 1, CONV_DIM)),
        "w_in": nrm(ks[5], (DEPTH, D_MODEL, IN_WIDTH)) * (D_MODEL ** -0.5) * col_scale,
        "conv_w": nrm(ks[6], (DEPTH, CONV_W, CONV_DIM)) * (CONV_W ** -0.5),
        "lambda_q1": 0.1 * nrm(ks[7], (DEPTH, QK_DIM)),
        "lambda_k1": 0.1 * nrm(ks[8], (DEPTH, QK_DIM)),
        "lambda_q2": 0.1 * nrm(ks[9], (DEPTH, QK_DIM)),
        "lambda_k2": 0.1 * nrm(ks[10], (DEPTH, QK_DIM)),
        "subln_g": 1.0 + 0.01 * nrm(ks[11], (DEPTH, V_DIM)),
        "w_out": nrm(ks[12], (DEPTH, MIX_WIDTH, D_MODEL)) * (MIX_WIDTH ** -0.5) * BETA,
        "ln1_g": 1.0 + 0.01 * nrm(ks[13], (DEPTH, D_MODEL)),
        "ln1_b": 0.01 * nrm(ks[14], (DEPTH, D_MODEL)),
        "w_ff1": nrm(ks[15], (DEPTH, D_MODEL, D_FF)) * (D_MODEL ** -0.5) * BETA,
        "w_ff2": nrm(ks[16], (DEPTH, D_FF, D_MODEL)) * (D_FF ** -0.5) * BETA,
        "ln2_g": 1.0 + 0.01 * nrm(ks[17], (DEPTH, D_MODEL)),
        "ln2_b": 0.01 * nrm(ks[18], (DEPTH, D_MODEL)),
    }


def reference(x_prompt, x_sample, cache_k, cache_v, state_conv, w_in, conv_w,
              lambda_q1, lambda_k1, lambda_q2, lambda_k2, subln_g, w_out,
              ln1_g, ln1_b, w_ff1, w_ff2, ln2_g, ln2_b):
    xp, xs = x_prompt, x_sample
    Bp, Sp = xp.shape[:2]
    Bs, Ss = xs.shape[:2]
    kp_l, vp_l, cp_l, ks_l, vs_l, cs_l = [], [], [], [], [], []
    for l in range(DEPTH):
        lambda_init = 0.8 - 0.6 * float(np.exp(-0.3 * l))
        lam = (jnp.exp(jnp.sum(lambda_q1[l].astype(jnp.float32) * lambda_k1[l].astype(jnp.float32)))
               - jnp.exp(jnp.sum(lambda_q2[l].astype(jnp.float32) * lambda_k2[l].astype(jnp.float32)))
               + lambda_init)
        lp = (lambda_init, subln_g[l], w_out[l], ln1_g[l], ln1_b[l], w_ff1[l], w_ff2[l], ln2_g[l], ln2_b[l])

        gb, u, q, k, v = project(xp, w_in[l])
        u_ext = jnp.concatenate([jnp.zeros((Bp, CONV_W - 1, CONV_DIM), u.dtype), u], axis=1)
        conv_out = causal_conv(u_ext, conv_w[l])
        attn_o = prompt_attention(q, k, v, lam)
        kp_l.append(k.reshape(Bp, Sp, N_HEADS, 2 * QK_DIM))
        vp_l.append(v)
        cp_l.append(u_ext[:, -(CONV_W - 1):])
        xp = finish_layer(xp, gb, conv_out, attn_o, *lp)

        gb, u, q, k, v = project(xs, w_in[l])
        u_ext = jnp.concatenate([state_conv[l].astype(u.dtype), u], axis=1)
        conv_out = causal_conv(u_ext, conv_w[l])
        k_all = jnp.concatenate(
            [cache_k[l].reshape(Bs, PAST_LEN, N_HEADS, 2, QK_DIM).astype(k.dtype), k], axis=1)
        v_all = jnp.concatenate([cache_v[l].astype(v.dtype), v], axis=1)
        q_pos = PAST_LEN + jnp.arange(Ss, dtype=jnp.int32)
        k_pos = jnp.arange(PAST_LEN + Ss, dtype=jnp.int32)
        attn_o = diff_attn_block(q, q_pos, k_all, v_all, k_pos, lam)
        ks_l.append(k.reshape(Bs, Ss, N_HEADS, 2 * QK_DIM))
        vs_l.append(v)
        cs_l.append(u_ext[:, -(CONV_W - 1):])
        xs = finish_layer(xs, gb, conv_out, attn_o, *lp)

    k_prompt = jnp.stack(kp_l, 0)
    v_prompt = jnp.stack(vp_l, 0)
    conv_prompt = jnp.stack(cp_l, 0)
    k_sample = jnp.stack(ks_l, 0)
    v_sample = jnp.stack(vs_l, 0)
    conv_sample = jnp.stack(cs_l, 0)
    return (xp, xs, k_prompt, v_prompt, conv_prompt, k_sample, v_sample, conv_sample)
```

```python
from functools import partial

import jax
import jax.numpy as jnp
import numpy as np
from jax import lax
from jax.experimental import pallas as pl
from jax.experimental.pallas import tpu as pltpu

D_MODEL = 1024
CHUNK = 64
CONV_DIM = 512
ATTN_DIM = 512
QK_DIM = 64
N_HEADS = 4
HEAD_W = 2 * QK_DIM
V_DIM = 128
CONV_W = 3
D_FF = 4096
LN_EPS = 1e-5
DEPTH = 1
ALPHA = (2 * DEPTH) ** 0.25
IN_WIDTH = 3 * CONV_DIM + 3 * ATTN_DIM

V7X_MXU_DIM = 256
V7X_VMEM_BYTES = 64 * 1024 * 1024

ROW_TILE = 512
Q_TILE = V7X_MXU_DIM
K_TILE = V7X_MXU_DIM
FF_CHUNK = 1024

_F32 = jnp.float32
_BF16 = jnp.bfloat16


def _vmem_limit(nbytes):
    assert nbytes <= V7X_VMEM_BYTES - (4 << 20)
    return int(nbytes)


def _dot(a, b):
    return jnp.dot(a, b, preferred_element_type=_F32)


def _dot_nt(a, b):
    return lax.dot_general(a, b, (((1,), (1,)), ((), ())), preferred_element_type=_F32)


def _dot_tn(a, b):
    return lax.dot_general(a, b, (((0,), (0,)), ((), ())), preferred_element_type=_F32)


def _div_pow2(x, n):
    assert n & (n - 1) == 0
    return lax.shift_right_logical(x, n.bit_length() - 1)


def _mod_pow2(x, n):
    assert n & (n - 1) == 0
    return lax.bitwise_and(x, n - 1)


def _layer_norm(x, g, b):
    mu = jnp.mean(x, axis=-1, keepdims=True)
    xc = x - mu
    var = jnp.mean(xc * xc, axis=-1, keepdims=True)
    return xc * lax.rsqrt(var + LN_EPS) * g + b


def _lambda_value(lq1, lk1, lq2, lk2, lambda_init):
    s1 = jnp.sum(lq1 * lk1, axis=-1, keepdims=True)
    s2 = jnp.sum(lq2 * lk2, axis=-1, keepdims=True)
    return jnp.exp(s1) - jnp.exp(s2) + lambda_init


def _project_columns(x_ref, w_ref):
    xb = x_ref[...].astype(_BF16)
    return [_dot(xb, w_ref[:, g * CONV_DIM:(g + 1) * CONV_DIM]) for g in range(IN_WIDTH // CONV_DIM)]


def _conv_from_taps(u, u1, u2, cw_ref):
    return cw_ref[0:1, :] * u2 + cw_ref[1:2, :] * u1 + cw_ref[2:3, :] * u


def _proj_prompt_kernel(x_ref, w_ref, cw_ref,
                        yconv_ref, kf_ref, vf_ref, qa_ref, qb_ref, kb_ref, vt_ref, cstate_ref,
                        carry_ref):
    i = pl.program_id(0)

    @pl.when(i == 0)
    def _():
        carry_ref[...] = jnp.zeros_like(carry_ref)

    gate, cc, hh, q, k, v = _project_columns(x_ref, w_ref)
    u = cc * hh
    rows = lax.broadcasted_iota(jnp.int32, u.shape, 0)
    prev2 = carry_ref[0:1, :]
    prev1 = carry_ref[1:2, :]
    u1 = jnp.where(rows == 0, prev1, pltpu.roll(u, 1, 0))
    u2 = jnp.where(rows == 0, prev2, jnp.where(rows == 1, prev1, pltpu.roll(u, 2, 0)))
    yconv_ref[...] = (gate * _conv_from_taps(u, u1, u2, cw_ref)).astype(_BF16)
    tail = u[ROW_TILE - (CONV_W - 1):, :]
    carry_ref[0:CONV_W - 1, :] = tail
    cstate_ref[...] = tail

    kf_ref[...] = k
    vf_ref[...] = v
    qs = q * (QK_DIM ** -0.5)
    lane = lax.broadcasted_iota(jnp.int32, (ROW_TILE, HEAD_W), 1)
    first = lane < QK_DIM
    for h in range(N_HEADS):
        sl = slice(h * HEAD_W, (h + 1) * HEAD_W)
        qh = qs[:, sl]
        qa_ref[h] = jnp.where(first, qh, 0.0).astype(_BF16)
        qb_ref[h] = jnp.where(first, 0.0, qh).astype(_BF16)
        kb_ref[h] = k[:, sl].astype(_BF16)
        vh = v[:, sl]
        for s in range(ROW_TILE // K_TILE):
            vt_ref[h, s] = vh[s * K_TILE:(s + 1) * K_TILE, :].T.astype(_BF16)


def _proj_sample_kernel(x_ref, w_ref, cw_ref, h1_ref, h2_ref,
                        yconv_ref, q_ref, k_ref, v_ref, u_ref, *, period):
    gate, cc, hh, q, k, v = _project_columns(x_ref, w_ref)
    u = cc * hh
    t = _mod_pow2(lax.broadcasted_iota(jnp.int32, u.shape, 0), period)
    u1 = jnp.where(t < 1, h1_ref[...], pltpu.roll(u, 1, 0))
    u2 = jnp.where(t < 2, h2_ref[...], pltpu.roll(u, 2, 0))
    yconv_ref[...] = (gate * _conv_from_taps(u, u1, u2, cw_ref)).astype(_BF16)
    q_ref[...] = q * (QK_DIM ** -0.5)
    k_ref[...] = k
    v_ref[...] = v
    u_ref[...] = u


def _project_prompt(x, w_in, conv_w):
    s = x.shape[0]
    assert s % ROW_TILE == 0 and ROW_TILE % K_TILE == 0
    n = s // ROW_TILE
    sub = ROW_TILE // K_TILE
    row = lambda i: (i, 0)
    head_row = lambda i: (0, i, 0)
    out_shape = (
        jax.ShapeDtypeStruct((s, CONV_DIM), _BF16),
        jax.ShapeDtypeStruct((s, ATTN_DIM), _F32),
        jax.ShapeDtypeStruct((s, ATTN_DIM), _F32),
        jax.ShapeDtypeStruct((N_HEADS, s, HEAD_W), _BF16),
        jax.ShapeDtypeStruct((N_HEADS, s, HEAD_W), _BF16),
        jax.ShapeDtypeStruct((N_HEADS, s, HEAD_W), _BF16),
        jax.ShapeDtypeStruct((N_HEADS, s // K_TILE, V_DIM, K_TILE), _BF16),
        jax.ShapeDtypeStruct((CONV_W - 1, CONV_DIM), _F32),
    )
    out_specs = (
        pl.BlockSpec((ROW_TILE, CONV_DIM), row),
        pl.BlockSpec((ROW_TILE, ATTN_DIM), row),
        pl.BlockSpec((ROW_TILE, ATTN_DIM), row),
        pl.BlockSpec((N_HEADS, ROW_TILE, HEAD_W), head_row),
        pl.BlockSpec((N_HEADS, ROW_TILE, HEAD_W), head_row),
        pl.BlockSpec((N_HEADS, ROW_TILE, HEAD_W), head_row),
        pl.BlockSpec((N_HEADS, sub, V_DIM, K_TILE), lambda i: (0, i, 0, 0)),
        pl.BlockSpec((CONV_W - 1, CONV_DIM), lambda i: (0, 0)),
    )
    return pl.pallas_call(
        _proj_prompt_kernel,
        grid=(n,),
        in_specs=[
            pl.BlockSpec((ROW_TILE, D_MODEL), row),
            pl.BlockSpec((D_MODEL, IN_WIDTH), lambda i: (0, 0)),
            pl.BlockSpec((CONV_W, CONV_DIM), lambda i: (0, 0)),
        ],
        out_specs=out_specs,
        out_shape=out_shape,
        scratch_shapes=[pltpu.VMEM((8, CONV_DIM), _F32)],
        compiler_params=pltpu.CompilerParams(
            dimension_semantics=("arbitrary",), vmem_limit_bytes=_vmem_limit(48 << 20)),
        name="project_prompt",
    )(x, w_in, conv_w)


def _project_sample(x, w_in, conv_w, hist1, hist2, period):
    r = x.shape[0]
    full = lambda shape: pl.BlockSpec(shape, lambda i: (0,) * len(shape))
    wide = (r, CONV_DIM)
    return pl.pallas_call(
        partial(_proj_sample_kernel, period=period),
        grid=(1,),
        in_specs=[full((r, D_MODEL)), full((D_MODEL, IN_WIDTH)), full((CONV_W, CONV_DIM)),
                  full(wide), full(wide)],
        out_specs=(full(wide),) * 5,
        out_shape=(jax.ShapeDtypeStruct(wide, _BF16),) + (jax.ShapeDtypeStruct(wide, _F32),) * 4,
        compiler_params=pltpu.CompilerParams(
            dimension_semantics=("arbitrary",), vmem_limit_bytes=_vmem_limit(48 << 20)),
        name="project_sample",
    )(x, w_in, conv_w, hist1, hist2)


def _attn_prompt_kernel(qa_ref, qb_ref, k_ref, vt_ref, lq1_ref, lk1_ref, lq2_ref, lk2_ref, g_ref,
                        o_ref, *, lambda_init):
    i = pl.program_id(1)
    qs = (qa_ref[0], qb_ref[0])

    def step(j, carry, masked):
        kt = k_ref[0, pl.ds(pl.multiple_of(j * K_TILE, K_TILE), K_TILE), :]
        vt = vt_ref[0, j]
        out = []
        for c in range(2):
            m, l, acc = carry[c]
            st = _dot_nt(kt, qs[c])
            if masked:
                kc = _div_pow2(lax.broadcasted_iota(jnp.int32, st.shape, 0), CHUNK)
                qc = _div_pow2(lax.broadcasted_iota(jnp.int32, st.shape, 1), CHUNK)
                st = jnp.where(kc <= qc, st, -jnp.inf)
            m_new = jnp.maximum(m, jnp.max(st, axis=0, keepdims=True))
            alpha = jnp.exp(m - m_new)
            p = jnp.exp(st - m_new)
            l = alpha * l + jnp.sum(p, axis=0, keepdims=True)
            acc = alpha * acc + _dot(vt, p.astype(_BF16))
            out.append((m_new, l, acc))
        return tuple(out)

    init = tuple((jnp.full((1, Q_TILE), -jnp.inf, _F32), jnp.zeros((1, Q_TILE), _F32),
                  jnp.zeros((V_DIM, Q_TILE), _F32)) for _ in range(2))
    carry = lax.fori_loop(0, i, lambda j, c: step(j, c, False), init)
    (_, l0, acc0), (_, l1, acc1) = step(i, carry, True)

    lam = _lambda_value(lq1_ref[...], lk1_ref[...], lq2_ref[...], lk2_ref[...], lambda_init)
    o = acc0 / l0 - lam * (acc1 / l1)
    ms = jnp.mean(o * o, axis=0, keepdims=True)
    on = o * lax.rsqrt(ms + LN_EPS)
    on = on.T * g_ref[...] * (1.0 - lambda_init)
    o_ref[...] = on.astype(_BF16)


def _attention_prompt(qa, qb, kb, vt, lam_vecs, subln_g, lambda_init):
    s = qa.shape[1]
    assert Q_TILE == K_TILE and s % Q_TILE == 0 and Q_TILE % CHUNK == 0
    nq = s // Q_TILE
    vec = pl.BlockSpec((1, QK_DIM), lambda h, i: (0, 0))
    return pl.pallas_call(
        partial(_attn_prompt_kernel, lambda_init=lambda_init),
        grid=(N_HEADS, nq),
        in_specs=[
            pl.BlockSpec((1, Q_TILE, HEAD_W), lambda h, i: (h, i, 0)),
            pl.BlockSpec((1, Q_TILE, HEAD_W), lambda h, i: (h, i, 0)),
            pl.BlockSpec((1, s, HEAD_W), lambda h, i: (h, 0, 0)),
            pl.BlockSpec((1, s // K_TILE, V_DIM, K_TILE), lambda h, i: (h, 0, 0, 0)),
            vec, vec, vec, vec,
            pl.BlockSpec((1, V_DIM), lambda h, i: (0, 0)),
        ],
        out_specs=pl.BlockSpec((Q_TILE, V_DIM), lambda h, i: (i, h)),
        out_shape=jax.ShapeDtypeStruct((s, ATTN_DIM), _BF16),
        compiler_params=pltpu.CompilerParams(
            dimension_semantics=("arbitrary", "arbitrary"), vmem_limit_bytes=_vmem_limit(40 << 20)),
        name="attention_prompt",
    )(qa, qb, kb, vt, *lam_vecs, subln_g)


def _attn_sample_kernel(q_ref, kn_ref, vn_ref, ck_ref, cv_ref, lq1_ref, lk1_ref, lq2_ref, lk2_ref,
                        g_ref, o_ref, *, lambda_init, past_len, n_new):
    group = 2 * N_HEADS
    width = group * n_new
    assert width == HEAD_W
    qb = q_ref[...].astype(_BF16)
    sel_r = lax.broadcasted_iota(jnp.int32, (n_new, width), 0)
    sel_c = lax.broadcasted_iota(jnp.int32, (n_new, width), 1)
    spread = jnp.where(_mod_pow2(sel_c, n_new) == sel_r, 1.0, 0.0).astype(_BF16)
    q_all = _dot_tn(qb, spread)
    blk_r = _div_pow2(lax.broadcasted_iota(jnp.int32, (ATTN_DIM, width), 0), QK_DIM)
    blk_c = _div_pow2(lax.broadcasted_iota(jnp.int32, (ATTN_DIM, width), 1), n_new)
    q_bd = jnp.where(blk_r == blk_c, q_all, 0.0).astype(_BF16)

    s_old = _dot(ck_ref[0].astype(_BF16), q_bd)
    s_new = _dot(kn_ref[...].astype(_BF16), q_bd)

    def chunk_mask(shape, k_off):
        k_pos = k_off + lax.broadcasted_iota(jnp.int32, shape, 0)
        q_pos = past_len + _mod_pow2(lax.broadcasted_iota(jnp.int32, shape, 1), n_new)
        return _div_pow2(k_pos, CHUNK) <= _div_pow2(q_pos, CHUNK)

    s_old = jnp.where(chunk_mask(s_old.shape, 0), s_old, -jnp.inf)
    s_new = jnp.where(chunk_mask(s_new.shape, past_len), s_new, -jnp.inf)
    m = jnp.maximum(jnp.max(s_old, axis=0, keepdims=True), jnp.max(s_new, axis=0, keepdims=True))
    p_old = jnp.exp(s_old - m)
    p_new = jnp.exp(s_new - m)
    inv_l = 1.0 / (jnp.sum(p_old, axis=0, keepdims=True) + jnp.sum(p_new, axis=0, keepdims=True))
    a_old = (p_old * inv_l).astype(_BF16)
    a_new = (p_new * inv_l).astype(_BF16)
    pv = _dot_tn(a_old, cv_ref[0].astype(_BF16)) + _dot_tn(a_new, vn_ref[...].astype(_BF16))

    lam = _lambda_value(lq1_ref[...], lk1_ref[...], lq2_ref[...], lk2_ref[...], lambda_init)
    g = g_ref[...]
    for h in range(N_HEADS):
        cols = slice(h * V_DIM, (h + 1) * V_DIM)
        r0 = h * 2 * n_new
        o = pv[r0:r0 + n_new, cols] - lam * pv[r0 + n_new:r0 + 2 * n_new, cols]
        ms = jnp.mean(o * o, axis=-1, keepdims=True)
        o_ref[:, cols] = (o * lax.rsqrt(ms + LN_EPS) * g * (1.0 - lambda_init)).astype(_BF16)


def _attention_sample(q, k_new, v_new, cache_k, cache_v, lam_vecs, subln_g, lambda_init, n_new):
    nb, past_len = cache_k.shape[:2]
    rows = lambda b: (b, 0)
    vec = pl.BlockSpec((1, QK_DIM), lambda b: (0, 0))
    return pl.pallas_call(
        partial(_attn_sample_kernel, lambda_init=lambda_init, past_len=past_len, n_new=n_new),
        grid=(nb,),
        in_specs=[
            pl.BlockSpec((n_new, ATTN_DIM), rows),
            pl.BlockSpec((n_new, ATTN_DIM), rows),
            pl.BlockSpec((n_new, ATTN_DIM), rows),
            pl.BlockSpec((1, past_len, ATTN_DIM), lambda b: (b, 0, 0)),
            pl.BlockSpec((1, past_len, ATTN_DIM), lambda b: (b, 0, 0)),
            vec, vec, vec, vec,
            pl.BlockSpec((1, V_DIM), lambda b: (0, 0)),
        ],
        out_specs=pl.BlockSpec((n_new, ATTN_DIM), rows),
        out_shape=jax.ShapeDtypeStruct((nb * n_new, ATTN_DIM), _BF16),
        compiler_params=pltpu.CompilerParams(
            dimension_semantics=("arbitrary",), vmem_limit_bytes=_vmem_limit(40 << 20)),
        name="attention_sample",
    )(q, k_new, v_new, cache_k, cache_v, *lam_vecs, subln_g)


def _finish_kernel(x_ref, yconv_ref, attn_ref, wo_ref, g1_ref, b1_ref, w1_ref, w2_ref, g2_ref, b2_ref,
                   y_ref):
    mixed = jnp.concatenate([yconv_ref[...], attn_ref[...]], axis=-1)
    x1 = _layer_norm(ALPHA * x_ref[...] + _dot(mixed, wo_ref[...]), g1_ref[...], b1_ref[...])
    x1b = x1.astype(_BF16)
    ff = jnp.zeros_like(x1)
    for c in range(D_FF // FF_CHUNK):
        cols = slice(c * FF_CHUNK, (c + 1) * FF_CHUNK)
        hdn = jnp.square(jnp.maximum(_dot(x1b, w1_ref[:, cols]), 0.0))
        ff = ff + _dot(hdn.astype(_BF16), w2_ref[cols, :])
    y_ref[...] = _layer_norm(ALPHA * x1 + ff, g2_ref[...], b2_ref[...])


def _finish(x, yconv, attn, w_out, g1, b1, w_ff1, w_ff2, g2, b2):
    r = x.shape[0]
    assert r % ROW_TILE == 0
    row = lambda i: (i, 0)
    const = lambda shape: pl.BlockSpec(shape, lambda i: (0, 0))
    return pl.pallas_call(
        _finish_kernel,
        grid=(r // ROW_TILE,),
        in_specs=[
            pl.BlockSpec((ROW_TILE, D_MODEL), row),
            pl.BlockSpec((ROW_TILE, CONV_DIM), row),
            pl.BlockSpec((ROW_TILE, ATTN_DIM), row),
            const((D_MODEL, D_MODEL)), const((1, D_MODEL)), const((1, D_MODEL)),
            const((D_MODEL, D_FF)), const((D_FF, D_MODEL)), const((1, D_MODEL)), const((1, D_MODEL)),
        ],
        out_specs=pl.BlockSpec((ROW_TILE, D_MODEL), row),
        out_shape=jax.ShapeDtypeStruct((r, D_MODEL), _F32),
        compiler_params=pltpu.CompilerParams(
            dimension_semantics=("arbitrary",), vmem_limit_bytes=_vmem_limit(56 << 20)),
        name="finish_layer",
    )(x, yconv, attn, w_out, g1, b1, w_ff1, w_ff2, g2, b2)


def kernel(x_prompt, x_sample, cache_k, cache_v, state_conv, w_in, conv_w, lambda_q1, lambda_k1,
           lambda_q2, lambda_k2, subln_g, w_out, ln1_g, ln1_b, w_ff1, w_ff2, ln2_g, ln2_b):
    bp, sp, _ = x_prompt.shape
    bs, ss, _ = x_sample.shape
    depth = w_in.shape[0]
    assert bp == 1 and depth == 1
    past_len = cache_k.shape[2]
    l = 0
    lambda_init = 0.8 - 0.6 * float(np.exp(-0.3 * l))

    w_in_b = w_in[l].astype(_BF16)
    w_out_b = w_out[l].astype(_BF16)
    w_ff1_b = w_ff1[l].astype(_BF16)
    w_ff2_b = w_ff2[l].astype(_BF16)
    lam_vecs = tuple(v[l].reshape(1, QK_DIM) for v in (lambda_q1, lambda_k1, lambda_q2, lambda_k2))
    g_sub = subln_g[l].reshape(1, V_DIM)
    ln = tuple(v[l].reshape(1, D_MODEL) for v in (ln1_g, ln1_b, ln2_g, ln2_b))

    def finish(x2d, yconv, attn):
        return _finish(x2d, yconv, attn, w_out_b, ln[0], ln[1], w_ff1_b, w_ff2_b, ln[2], ln[3])

    xp2 = x_prompt.reshape(sp, D_MODEL)
    yconv_p, k_p, v_p, qa, qb, kb, vt, conv_p = _project_prompt(xp2, w_in_b, conv_w[l])
    attn_p = _attention_prompt(qa, qb, kb, vt, lam_vecs, g_sub, lambda_init)
    y_prompt = finish(xp2, yconv_p, attn_p).reshape(bp, sp, D_MODEL)

    xs2 = x_sample.reshape(bs * ss, D_MODEL)
    st = state_conv[l].astype(_F32)
    pad = lambda a: jnp.pad(a, ((0, 0), (0, ss - a.shape[1]), (0, 0))).reshape(bs * ss, CONV_DIM)
    hist2 = pad(st)
    hist1 = pad(st[:, 1:])
    yconv_s, q_s, k_s, v_s, u_s = _project_sample(xs2, w_in_b, conv_w[l], hist1, hist2, ss)
    ck = cache_k[l].reshape(bs, past_len, ATTN_DIM)
    cv = cache_v[l].reshape(bs, past_len, ATTN_DIM)
    attn_s = _attention_sample(q_s, k_s, v_s, ck, cv, lam_vecs, g_sub, lambda_init, ss)
    y_sample = finish(xs2, yconv_s, attn_s).reshape(bs, ss, D_MODEL)

    k_prompt = k_p.reshape(depth, bp, sp, N_HEADS, HEAD_W)
    v_prompt = v_p.reshape(depth, bp, sp, N_HEADS, V_DIM)
    conv_prompt = conv_p.reshape(depth, bp, CONV_W - 1, CONV_DIM)
    k_sample = k_s.reshape(depth, bs, ss, N_HEADS, HEAD_W)
    v_sample = v_s.reshape(depth, bs, ss, N_HEADS, V_DIM)
    conv_sample = u_s.reshape(bs, ss, CONV_DIM)[:, ss - (CONV_W - 1):].reshape(
        depth, bs, CONV_W - 1, CONV_DIM)
    return (y_prompt, y_sample, k_prompt, v_prompt, conv_prompt, k_sample, v_sample, conv_sample)
```

```python
from functools import partial

import jax
import jax.numpy as jnp
import numpy as np
from jax import lax
from jax.experimental import pallas as pl
from jax.experimental.pallas import tpu as pltpu

D_MODEL = 1024
CHUNK = 64
CONV_DIM = 512
ATTN_DIM = 512
QK_DIM = 64
N_HEADS = 4
HEAD_W = 2 * QK_DIM
V_DIM = 128
CONV_W = 3
D_FF = 4096
LN_EPS = 1e-5
DEPTH = 1
ALPHA = (2 * DEPTH) ** 0.25
IN_WIDTH = 3 * CONV_DIM + 3 * ATTN_DIM

V7X_MXU_DIM = 256
V7X_VMEM_BYTES = 64 * 1024 * 1024

ROW_TILE = 512
Q_TILE = V7X_MXU_DIM
K_TILE = 4 * V7X_MXU_DIM
PROJ_TILE = K_TILE
FF_CHUNK = 1024

_F32 = jnp.float32
_BF16 = jnp.bfloat16


def _vmem_limit(nbytes):
    assert nbytes <= V7X_VMEM_BYTES - (4 << 20)
    return int(nbytes)


def _dot(a, b):
    return jnp.dot(a, b, preferred_element_type=_F32)


def _dot_nt(a, b):
    return lax.dot_general(a, b, (((1,), (1,)), ((), ())), preferred_element_type=_F32)


def _dot_tn(a, b):
    return lax.dot_general(a, b, (((0,), (0,)), ((), ())), preferred_element_type=_F32)


def _div_pow2(x, n):
    assert n & (n - 1) == 0
    return lax.shift_right_logical(x, n.bit_length() - 1)


def _mod_pow2(x, n):
    assert n & (n - 1) == 0
    return lax.bitwise_and(x, n - 1)


def _layer_norm(x, g, b):
    mu = jnp.mean(x, axis=-1, keepdims=True)
    xc = x - mu
    var = jnp.mean(xc * xc, axis=-1, keepdims=True)
    return xc * lax.rsqrt(var + LN_EPS) * g + b


def _lambda_value(lq1, lk1, lq2, lk2, lambda_init):
    s1 = jnp.sum(lq1 * lk1, axis=-1, keepdims=True)
    s2 = jnp.sum(lq2 * lk2, axis=-1, keepdims=True)
    return jnp.exp(s1) - jnp.exp(s2) + lambda_init


def _project_columns(x_ref, w_ref):
    xb = x_ref[...].astype(_BF16)
    return [_dot(xb, w_ref[:, g * CONV_DIM:(g + 1) * CONV_DIM]) for g in range(IN_WIDTH // CONV_DIM)]


def _conv_from_taps(u, u1, u2, cw_ref):
    return cw_ref[0:1, :] * u2 + cw_ref[1:2, :] * u1 + cw_ref[2:3, :] * u


def _proj_prompt_kernel(x_ref, w_ref, cw_ref,
                        yconv_ref, kf_ref, vf_ref, qa_ref, qb_ref, kb_ref, vt_ref, cstate_ref,
                        carry_ref):
    i = pl.program_id(0)

    @pl.when(i == 0)
    def _():
        carry_ref[...] = jnp.zeros_like(carry_ref)

    gate, cc, hh, q, k, v = _project_columns(x_ref, w_ref)
    u = cc * hh
    rows = lax.broadcasted_iota(jnp.int32, u.shape, 0)
    prev2 = carry_ref[0:1, :]
    prev1 = carry_ref[1:2, :]
    u1 = jnp.where(rows == 0, prev1, pltpu.roll(u, 1, 0))
    u2 = jnp.where(rows == 0, prev2, jnp.where(rows == 1, prev1, pltpu.roll(u, 2, 0)))
    yconv_ref[...] = (gate * _conv_from_taps(u, u1, u2, cw_ref)).astype(_BF16)
    tail = u[PROJ_TILE - (CONV_W - 1):, :]
    carry_ref[0:CONV_W - 1, :] = tail
    cstate_ref[...] = tail

    kf_ref[...] = k
    vf_ref[...] = v
    qs = q * (QK_DIM ** -0.5)
    lane = lax.broadcasted_iota(jnp.int32, (PROJ_TILE, HEAD_W), 1)
    first = lane < QK_DIM
    for h in range(N_HEADS):
        sl = slice(h * HEAD_W, (h + 1) * HEAD_W)
        qh = qs[:, sl]
        qa_ref[h] = jnp.where(first, qh, 0.0).astype(_BF16)
        qb_ref[h] = jnp.where(first, 0.0, qh).astype(_BF16)
        kb_ref[h] = k[:, sl].astype(_BF16)
        vt_ref[h, 0] = v[:, sl].T.astype(_BF16)


def _proj_sample_kernel(x_ref, w_ref, cw_ref, h1_ref, h2_ref,
                        yconv_ref, q_ref, k_ref, v_ref, u_ref, *, period):
    gate, cc, hh, q, k, v = _project_columns(x_ref, w_ref)
    u = cc * hh
    t = _mod_pow2(lax.broadcasted_iota(jnp.int32, u.shape, 0), period)
    u1 = jnp.where(t < 1, h1_ref[...], pltpu.roll(u, 1, 0))
    u2 = jnp.where(t < 2, h2_ref[...], pltpu.roll(u, 2, 0))
    yconv_ref[...] = (gate * _conv_from_taps(u, u1, u2, cw_ref)).astype(_BF16)
    q_ref[...] = q * (QK_DIM ** -0.5)
    k_ref[...] = k
    v_ref[...] = v
    u_ref[...] = u


def _project_prompt(x, w_in, conv_w):
    s = x.shape[0]
    assert s % PROJ_TILE == 0 and PROJ_TILE == K_TILE
    n = s // PROJ_TILE
    row = lambda i: (i, 0)
    head_row = lambda i: (0, i, 0)
    out_shape = (
        jax.ShapeDtypeStruct((s, CONV_DIM), _BF16),
        jax.ShapeDtypeStruct((s, ATTN_DIM), _F32),
        jax.ShapeDtypeStruct((s, ATTN_DIM), _F32),
        jax.ShapeDtypeStruct((N_HEADS, s, HEAD_W), _BF16),
        jax.ShapeDtypeStruct((N_HEADS, s, HEAD_W), _BF16),
        jax.ShapeDtypeStruct((N_HEADS, s, HEAD_W), _BF16),
        jax.ShapeDtypeStruct((N_HEADS, s // K_TILE, V_DIM, K_TILE), _BF16),
        jax.ShapeDtypeStruct((CONV_W - 1, CONV_DIM), _F32),
    )
    out_specs = (
        pl.BlockSpec((PROJ_TILE, CONV_DIM), row),
        pl.BlockSpec((PROJ_TILE, ATTN_DIM), row),
        pl.BlockSpec((PROJ_TILE, ATTN_DIM), row),
        pl.BlockSpec((N_HEADS, PROJ_TILE, HEAD_W), head_row),
        pl.BlockSpec((N_HEADS, PROJ_TILE, HEAD_W), head_row),
        pl.BlockSpec((N_HEADS, PROJ_TILE, HEAD_W), head_row),
        pl.BlockSpec((N_HEADS, 1, V_DIM, K_TILE), lambda i: (0, i, 0, 0)),
        pl.BlockSpec((CONV_W - 1, CONV_DIM), lambda i: (0, 0)),
    )
    return pl.pallas_call(
        _proj_prompt_kernel,
        grid=(n,),
        in_specs=[
            pl.BlockSpec((PROJ_TILE, D_MODEL), row),
            pl.BlockSpec((D_MODEL, IN_WIDTH), lambda i: (0, 0)),
            pl.BlockSpec((CONV_W, CONV_DIM), lambda i: (0, 0)),
        ],
        out_specs=out_specs,
        out_shape=out_shape,
        scratch_shapes=[pltpu.VMEM((8, CONV_DIM), _F32)],
        compiler_params=pltpu.CompilerParams(
            dimension_semantics=("arbitrary",), vmem_limit_bytes=_vmem_limit(48 << 20)),
        name="project_prompt",
    )(x, w_in, conv_w)


def _project_sample(x, w_in, conv_w, hist1, hist2, period):
    r = x.shape[0]
    full = lambda shape: pl.BlockSpec(shape, lambda i: (0,) * len(shape))
    wide = (r, CONV_DIM)
    return pl.pallas_call(
        partial(_proj_sample_kernel, period=period),
        grid=(1,),
        in_specs=[full((r, D_MODEL)), full((D_MODEL, IN_WIDTH)), full((CONV_W, CONV_DIM)),
                  full(wide), full(wide)],
        out_specs=(full(wide),) * 5,
        out_shape=(jax.ShapeDtypeStruct(wide, _BF16),) + (jax.ShapeDtypeStruct(wide, _F32),) * 4,
        compiler_params=pltpu.CompilerParams(
            dimension_semantics=("arbitrary",), vmem_limit_bytes=_vmem_limit(48 << 20)),
        name="project_sample",
    )(x, w_in, conv_w, hist1, hist2)


def _attn_prompt_kernel(qa_ref, qb_ref, k_ref, vt_ref, lq1_ref, lk1_ref, lq2_ref, lk2_ref, g_ref,
                        o_ref, *, lambda_init):
    i = pl.program_id(1)
    ratio = K_TILE // Q_TILE
    qs = (qa_ref[0], qb_ref[0])

    n_full = _div_pow2(i, ratio)
    q_chunk0 = (i - n_full * ratio) * (Q_TILE // CHUNK)

    def step(j, carry, masked):
        kt = k_ref[0, pl.ds(pl.multiple_of(j * K_TILE, K_TILE), K_TILE), :]
        vt = vt_ref[0, j]
        out = []
        for c in range(2):
            m, l, acc = carry[c]
            st = _dot_nt(kt, qs[c])
            if masked:
                kc = _div_pow2(lax.broadcasted_iota(jnp.int32, st.shape, 0), CHUNK)
                qc = _div_pow2(lax.broadcasted_iota(jnp.int32, st.shape, 1), CHUNK) + q_chunk0
                st = jnp.where(kc <= qc, st, -jnp.inf)
            m_new = jnp.maximum(m, jnp.max(st, axis=0, keepdims=True))
            alpha = jnp.exp(m - m_new)
            p = jnp.exp(st - m_new)
            l = alpha * l + jnp.sum(p, axis=0, keepdims=True)
            acc = alpha * acc + _dot(vt, p.astype(_BF16))
            out.append((m_new, l, acc))
        return tuple(out)

    init = tuple((jnp.full((1, Q_TILE), -jnp.inf, _F32), jnp.zeros((1, Q_TILE), _F32),
                  jnp.zeros((V_DIM, Q_TILE), _F32)) for _ in range(2))
    carry = lax.fori_loop(0, n_full, lambda j, c: step(j, c, False), init)
    (_, l0, acc0), (_, l1, acc1) = step(n_full, carry, True)

    lam = _lambda_value(lq1_ref[...], lk1_ref[...], lq2_ref[...], lk2_ref[...], lambda_init)
    o = acc0 / l0 - lam * (acc1 / l1)
    ms = jnp.mean(o * o, axis=0, keepdims=True)
    on = o * lax.rsqrt(ms + LN_EPS)
    on = on.T * g_ref[...] * (1.0 - lambda_init)
    o_ref[...] = on.astype(_BF16)


def _attention_prompt(qa, qb, kb, vt, lam_vecs, subln_g, lambda_init):
    s = qa.shape[1]
    assert K_TILE % Q_TILE == 0 and s % K_TILE == 0 and Q_TILE % CHUNK == 0
    nq = s // Q_TILE
    vec = pl.BlockSpec((1, QK_DIM), lambda h, i: (0, 0))
    return pl.pallas_call(
        partial(_attn_prompt_kernel, lambda_init=lambda_init),
        grid=(N_HEADS, nq),
        in_specs=[
            pl.BlockSpec((1, Q_TILE, HEAD_W), lambda h, i: (h, i, 0)),
            pl.BlockSpec((1, Q_TILE, HEAD_W), lambda h, i: (h, i, 0)),
            pl.BlockSpec((1, s, HEAD_W), lambda h, i: (h, 0, 0)),
            pl.BlockSpec((1, s // K_TILE, V_DIM, K_TILE), lambda h, i: (h, 0, 0, 0)),
            vec, vec, vec, vec,
            pl.BlockSpec((1, V_DIM), lambda h, i: (0, 0)),
        ],
        out_specs=pl.BlockSpec((Q_TILE, V_DIM), lambda h, i: (i, h)),
        out_shape=jax.ShapeDtypeStruct((s, ATTN_DIM), _BF16),
        compiler_params=pltpu.CompilerParams(
            dimension_semantics=("arbitrary", "arbitrary"), vmem_limit_bytes=_vmem_limit(40 << 20)),
        name="attention_prompt",
    )(qa, qb, kb, vt, *lam_vecs, subln_g)


def _attn_sample_kernel(q_ref, kn_ref, vn_ref, ck_ref, cv_ref, lq1_ref, lk1_ref, lq2_ref, lk2_ref,
                        g_ref, o_ref, *, lambda_init, past_len, n_new):
    group = 2 * N_HEADS
    width = group * n_new
    assert width == HEAD_W
    qb = q_ref[...].astype(_BF16)
    sel_r = lax.broadcasted_iota(jnp.int32, (n_new, width), 0)
    sel_c = lax.broadcasted_iota(jnp.int32, (n_new, width), 1)
    spread = jnp.where(_mod_pow2(sel_c, n_new) == sel_r, 1.0, 0.0).astype(_BF16)
    q_all = _dot_tn(qb, spread)
    blk_r = _div_pow2(lax.broadcasted_iota(jnp.int32, (ATTN_DIM, width), 0), QK_DIM)
    blk_c = _div_pow2(lax.broadcasted_iota(jnp.int32, (ATTN_DIM, width), 1), n_new)
    q_bd = jnp.where(blk_r == blk_c, q_all, 0.0).astype(_BF16)

    s_old = _dot(ck_ref[0].astype(_BF16), q_bd)
    s_new = _dot(kn_ref[...].astype(_BF16), q_bd)

    def chunk_mask(shape, k_off):
        k_pos = k_off + lax.broadcasted_iota(jnp.int32, shape, 0)
        q_pos = past_len + _mod_pow2(lax.broadcasted_iota(jnp.int32, shape, 1), n_new)
        return _div_pow2(k_pos, CHUNK) <= _div_pow2(q_pos, CHUNK)

    s_old = jnp.where(chunk_mask(s_old.shape, 0), s_old, -jnp.inf)
    s_new = jnp.where(chunk_mask(s_new.shape, past_len), s_new, -jnp.inf)
    m = jnp.maximum(jnp.max(s_old, axis=0, keepdims=True), jnp.max(s_new, axis=0, keepdims=True))
    p_old = jnp.exp(s_old - m)
    p_new = jnp.exp(s_new - m)
    inv_l = 1.0 / (jnp.sum(p_old, axis=0, keepdims=True) + jnp.sum(p_new, axis=0, keepdims=True))
    a_old = (p_old * inv_l).astype(_BF16)
    a_new = (p_new * inv_l).astype(_BF16)
    pv = _dot_tn(a_old, cv_ref[0].astype(_BF16)) + _dot_tn(a_new, vn_ref[...].astype(_BF16))

    lam = _lambda_value(lq1_ref[...], lk1_ref[...], lq2_ref[...], lk2_ref[...], lambda_init)
    g = g_ref[...]
    for h in range(N_HEADS):
        cols = slice(h * V_DIM, (h + 1) * V_DIM)
        r0 = h * 2 * n_new
        o = pv[r0:r0 + n_new, cols] - lam * pv[r0 + n_new:r0 + 2 * n_new, cols]
        ms = jnp.mean(o * o, axis=-1, keepdims=True)
        o_ref[:, cols] = (o * lax.rsqrt(ms + LN_EPS) * g * (1.0 - lambda_init)).astype(_BF16)


def _attention_sample(q, k_new, v_new, cache_k, cache_v, lam_vecs, subln_g, lambda_init, n_new):
    nb, past_len = cache_k.shape[:2]
    rows = lambda b: (b, 0)
    vec = pl.BlockSpec((1, QK_DIM), lambda b: (0, 0))
    return pl.pallas_call(
        partial(_attn_sample_kernel, lambda_init=lambda_init, past_len=past_len, n_new=n_new),
        grid=(nb,),
        in_specs=[
            pl.BlockSpec((n_new, ATTN_DIM), rows),
            pl.BlockSpec((n_new, ATTN_DIM), rows),
            pl.BlockSpec((n_new, ATTN_DIM), rows),
            pl.BlockSpec((1, past_len, ATTN_DIM), lambda b: (b, 0, 0)),
            pl.BlockSpec((1, past_len, ATTN_DIM), lambda b: (b, 0, 0)),
            vec, vec, vec, vec,
            pl.BlockSpec((1, V_DIM), lambda b: (0, 0)),
        ],
        out_specs=pl.BlockSpec((n_new, ATTN_DIM), rows),
        out_shape=jax.ShapeDtypeStruct((nb * n_new, ATTN_DIM), _BF16),
        compiler_params=pltpu.CompilerParams(
            dimension_semantics=("arbitrary",), vmem_limit_bytes=_vmem_limit(40 << 20)),
        name="attention_sample",
    )(q, k_new, v_new, cache_k, cache_v, *lam_vecs, subln_g)


def _finish_kernel(x_ref, yconv_ref, attn_ref, wo_ref, g1_ref, b1_ref, w1_ref, w2_ref, g2_ref, b2_ref,
                   y_ref):
    mixed = jnp.concatenate([yconv_ref[...], attn_ref[...]], axis=-1)
    x1 = _layer_norm(ALPHA * x_ref[...] + _dot(mixed, wo_ref[...]), g1_ref[...], b1_ref[...])
    x1b = x1.astype(_BF16)
    ff = jnp.zeros_like(x1)
    for c in range(D_FF // FF_CHUNK):
        cols = slice(c * FF_CHUNK, (c + 1) * FF_CHUNK)
        hdn = jnp.square(jnp.maximum(_dot(x1b, w1_ref[:, cols]), 0.0))
        ff = ff + _dot(hdn.astype(_BF16), w2_ref[cols, :])
    y_ref[...] = _layer_norm(ALPHA * x1 + ff, g2_ref[...], b2_ref[...])


def _finish(x, yconv, attn, w_out, g1, b1, w_ff1, w_ff2, g2, b2):
    r = x.shape[0]
    assert r % ROW_TILE == 0
    row = lambda i: (i, 0)
    const = lambda shape: pl.BlockSpec(shape, lambda i: (0, 0))
    return pl.pallas_call(
        _finish_kernel,
        grid=(r // ROW_TILE,),
        in_specs=[
            pl.BlockSpec((ROW_TILE, D_MODEL), row),
            pl.BlockSpec((ROW_TILE, CONV_DIM), row),
            pl.BlockSpec((ROW_TILE, ATTN_DIM), row),
            const((D_MODEL, D_MODEL)), const((1, D_MODEL)), const((1, D_MODEL)),
            const((D_MODEL, D_FF)), const((D_FF, D_MODEL)), const((1, D_MODEL)), const((1, D_MODEL)),
        ],
        out_specs=pl.BlockSpec((ROW_TILE, D_MODEL), row),
        out_shape=jax.ShapeDtypeStruct((r, D_MODEL), _F32),
        compiler_params=pltpu.CompilerParams(
            dimension_semantics=("arbitrary",), vmem_limit_bytes=_vmem_limit(56 << 20)),
        name="finish_layer",
    )(x, yconv, attn, w_out, g1, b1, w_ff1, w_ff2, g2, b2)


def kernel(x_prompt, x_sample, cache_k, cache_v, state_conv, w_in, conv_w, lambda_q1, lambda_k1,
           lambda_q2, lambda_k2, subln_g, w_out, ln1_g, ln1_b, w_ff1, w_ff2, ln2_g, ln2_b):
    bp, sp, _ = x_prompt.shape
    bs, ss, _ = x_sample.shape
    depth = w_in.shape[0]
    assert bp == 1 and depth == 1
    past_len = cache_k.shape[2]
    l = 0
    lambda_init = 0.8 - 0.6 * float(np.exp(-0.3 * l))

    w_in_b = w_in[l].astype(_BF16)
    w_out_b = w_out[l].astype(_BF16)
    w_ff1_b = w_ff1[l].astype(_BF16)
    w_ff2_b = w_ff2[l].astype(_BF16)
    lam_vecs = tuple(v[l].reshape(1, QK_DIM) for v in (lambda_q1, lambda_k1, lambda_q2, lambda_k2))
    g_sub = subln_g[l].reshape(1, V_DIM)
    ln = tuple(v[l].reshape(1, D_MODEL) for v in (ln1_g, ln1_b, ln2_g, ln2_b))

    def finish(x2d, yconv, attn):
        return _finish(x2d, yconv, attn, w_out_b, ln[0], ln[1], w_ff1_b, w_ff2_b, ln[2], ln[3])

    xp2 = x_prompt.reshape(sp, D_MODEL)
    yconv_p, k_p, v_p, qa, qb, kb, vt, conv_p = _project_prompt(xp2, w_in_b, conv_w[l])
    attn_p = _attention_prompt(qa, qb, kb, vt, lam_vecs, g_sub, lambda_init)
    y_prompt = finish(xp2, yconv_p, attn_p).reshape(bp, sp, D_MODEL)

    xs2 = x_sample.reshape(bs * ss, D_MODEL)
    st = state_conv[l].astype(_F32)
    pad = lambda a: jnp.pad(a, ((0, 0), (0, ss - a.shape[1]), (0, 0))).reshape(bs * ss, CONV_DIM)
    hist2 = pad(st)
    hist1 = pad(st[:, 1:])
    yconv_s, q_s, k_s, v_s, u_s = _project_sample(xs2, w_in_b, conv_w[l], hist1, hist2, ss)
    ck = cache_k.reshape(depth * bs, past_len, ATTN_DIM)
    cv = cache_v.reshape(depth * bs, past_len, ATTN_DIM)
    attn_s = _attention_sample(q_s, k_s, v_s, ck, cv, lam_vecs, g_sub, lambda_init, ss)
    y_sample = finish(xs2, yconv_s, attn_s).reshape(bs, ss, D_MODEL)

    k_prompt = k_p.reshape(depth, bp, sp, N_HEADS, HEAD_W)
    v_prompt = v_p.reshape(depth, bp, sp, N_HEADS, V_DIM)
    conv_prompt = conv_p.reshape(depth, bp, CONV_W - 1, CONV_DIM)
    k_sample = k_s.reshape(depth, bs, ss, N_HEADS, HEAD_W)
    v_sample = v_s.reshape(depth, bs, ss, N_HEADS, V_DIM)
    conv_sample = u_s.reshape(bs, ss, CONV_DIM)[:, ss - (CONV_W - 1):].reshape(
        depth, bs, CONV_W - 1, CONV_DIM)
    return (y_prompt, y_sample, k_prompt, v_prompt, conv_prompt, k_sample, v_sample, conv_sample)
```

```python
from functools import partial

import jax
import jax.numpy as jnp
import numpy as np
from jax import lax
from jax.experimental import pallas as pl
from jax.experimental.pallas import tpu as pltpu

D_MODEL = 1024
CHUNK = 64
CONV_DIM = 512
ATTN_DIM = 512
QK_DIM = 64
N_HEADS = 4
HEAD_W = 2 * QK_DIM
V_DIM = 128
BF16_SUBLANES = 16
V_EXT = V_DIM + BF16_SUBLANES
LOG2E = 1.4426950408889634
CONV_W = 3
D_FF = 4096
LN_EPS = 1e-5
DEPTH = 1
ALPHA = (2 * DEPTH) ** 0.25
IN_WIDTH = 3 * CONV_DIM + 3 * ATTN_DIM

V7X_MXU_DIM = 256
V7X_VMEM_BYTES = 64 * 1024 * 1024

ROW_TILE = 512
Q_TILE = V7X_MXU_DIM
K_TILE = 4 * V7X_MXU_DIM
PROJ_TILE = K_TILE
FF_CHUNK = 1024

_F32 = jnp.float32
_BF16 = jnp.bfloat16


def _vmem_limit(nbytes):
    assert nbytes <= V7X_VMEM_BYTES - (4 << 20)
    return int(nbytes)


def _dot(a, b):
    return jnp.dot(a, b, preferred_element_type=_F32)


def _dot_nt(a, b):
    return lax.dot_general(a, b, (((1,), (1,)), ((), ())), preferred_element_type=_F32)


def _dot_tn(a, b):
    return lax.dot_general(a, b, (((0,), (0,)), ((), ())), preferred_element_type=_F32)


def _div_pow2(x, n):
    assert n & (n - 1) == 0
    return lax.shift_right_logical(x, n.bit_length() - 1)


def _mod_pow2(x, n):
    assert n & (n - 1) == 0
    return lax.bitwise_and(x, n - 1)


def _layer_norm(x, g, b):
    mu = jnp.mean(x, axis=-1, keepdims=True)
    xc = x - mu
    var = jnp.mean(xc * xc, axis=-1, keepdims=True)
    return xc * lax.rsqrt(var + LN_EPS) * g + b


def _lambda_value(lq1, lk1, lq2, lk2, lambda_init):
    s1 = jnp.sum(lq1 * lk1, axis=-1, keepdims=True)
    s2 = jnp.sum(lq2 * lk2, axis=-1, keepdims=True)
    return jnp.exp(s1) - jnp.exp(s2) + lambda_init


def _project_columns(x_ref, w_ref):
    xb = x_ref[...].astype(_BF16)
    return [_dot(xb, w_ref[:, g * CONV_DIM:(g + 1) * CONV_DIM]) for g in range(IN_WIDTH // CONV_DIM)]


def _conv_from_taps(u, u1, u2, cw_ref):
    return cw_ref[0:1, :] * u2 + cw_ref[1:2, :] * u1 + cw_ref[2:3, :] * u


def _proj_prompt_kernel(x_ref, w_ref, cw_ref,
                        yconv_ref, kf_ref, vf_ref, qa_ref, qb_ref, kb_ref, vt_ref, cstate_ref,
                        carry_ref):
    i = pl.program_id(0)

    @pl.when(i == 0)
    def _():
        carry_ref[...] = jnp.zeros_like(carry_ref)

    gate, cc, hh, q, k, v = _project_columns(x_ref, w_ref)
    u = cc * hh
    rows = lax.broadcasted_iota(jnp.int32, u.shape, 0)
    prev2 = carry_ref[0:1, :]
    prev1 = carry_ref[1:2, :]
    u1 = jnp.where(rows == 0, prev1, pltpu.roll(u, 1, 0))
    u2 = jnp.where(rows == 0, prev2, jnp.where(rows == 1, prev1, pltpu.roll(u, 2, 0)))
    yconv_ref[...] = (gate * _conv_from_taps(u, u1, u2, cw_ref)).astype(_BF16)
    tail = u[PROJ_TILE - (CONV_W - 1):, :]
    carry_ref[0:CONV_W - 1, :] = tail
    cstate_ref[...] = tail

    kf_ref[...] = k
    vf_ref[...] = v
    qs = q * (QK_DIM ** -0.5 * LOG2E)
    lane = lax.broadcasted_iota(jnp.int32, (PROJ_TILE, HEAD_W), 1)
    first = lane < QK_DIM
    ones_row = jnp.where(lax.broadcasted_iota(jnp.int32, (BF16_SUBLANES, K_TILE), 0) == 0,
                         1.0, 0.0).astype(_BF16)
    for h in range(N_HEADS):
        sl = slice(h * HEAD_W, (h + 1) * HEAD_W)
        qh = qs[:, sl]
        qa_ref[h] = jnp.where(first, qh, 0.0).astype(_BF16)
        qb_ref[h] = jnp.where(first, 0.0, qh).astype(_BF16)
        kb_ref[h] = k[:, sl].astype(_BF16)
        vt_ref[h, 0, :V_DIM, :] = v[:, sl].T.astype(_BF16)
        vt_ref[h, 0, V_DIM:, :] = ones_row


def _proj_sample_kernel(x_ref, w_ref, cw_ref, h1_ref, h2_ref,
                        yconv_ref, q_ref, k_ref, v_ref, u_ref, *, period):
    gate, cc, hh, q, k, v = _project_columns(x_ref, w_ref)
    u = cc * hh
    t = _mod_pow2(lax.broadcasted_iota(jnp.int32, u.shape, 0), period)
    u1 = jnp.where(t < 1, h1_ref[...], pltpu.roll(u, 1, 0))
    u2 = jnp.where(t < 2, h2_ref[...], pltpu.roll(u, 2, 0))
    yconv_ref[...] = (gate * _conv_from_taps(u, u1, u2, cw_ref)).astype(_BF16)
    q_ref[...] = q * (QK_DIM ** -0.5)
    k_ref[...] = k
    v_ref[...] = v
    u_ref[...] = u


def _project_prompt(x, w_in, conv_w):
    s = x.shape[0]
    assert s % PROJ_TILE == 0 and PROJ_TILE == K_TILE
    n = s // PROJ_TILE
    row = lambda i: (i, 0)
    head_row = lambda i: (0, i, 0)
    out_shape = (
        jax.ShapeDtypeStruct((s, CONV_DIM), _BF16),
        jax.ShapeDtypeStruct((s, ATTN_DIM), _F32),
        jax.ShapeDtypeStruct((s, ATTN_DIM), _F32),
        jax.ShapeDtypeStruct((N_HEADS, s, HEAD_W), _BF16),
        jax.ShapeDtypeStruct((N_HEADS, s, HEAD_W), _BF16),
        jax.ShapeDtypeStruct((N_HEADS, s, HEAD_W), _BF16),
        jax.ShapeDtypeStruct((N_HEADS, s // K_TILE, V_EXT, K_TILE), _BF16),
        jax.ShapeDtypeStruct((CONV_W - 1, CONV_DIM), _F32),
    )
    out_specs = (
        pl.BlockSpec((PROJ_TILE, CONV_DIM), row),
        pl.BlockSpec((PROJ_TILE, ATTN_DIM), row),
        pl.BlockSpec((PROJ_TILE, ATTN_DIM), row),
        pl.BlockSpec((N_HEADS, PROJ_TILE, HEAD_W), head_row),
        pl.BlockSpec((N_HEADS, PROJ_TILE, HEAD_W), head_row),
        pl.BlockSpec((N_HEADS, PROJ_TILE, HEAD_W), head_row),
        pl.BlockSpec((N_HEADS, 1, V_EXT, K_TILE), lambda i: (0, i, 0, 0)),
        pl.BlockSpec((CONV_W - 1, CONV_DIM), lambda i: (0, 0)),
    )
    return pl.pallas_call(
        _proj_prompt_kernel,
        grid=(n,),
        in_specs=[
            pl.BlockSpec((PROJ_TILE, D_MODEL), row),
            pl.BlockSpec((D_MODEL, IN_WIDTH), lambda i: (0, 0)),
            pl.BlockSpec((CONV_W, CONV_DIM), lambda i: (0, 0)),
        ],
        out_specs=out_specs,
        out_shape=out_shape,
        scratch_shapes=[pltpu.VMEM((8, CONV_DIM), _F32)],
        compiler_params=pltpu.CompilerParams(
            dimension_semantics=("arbitrary",), vmem_limit_bytes=_vmem_limit(48 << 20)),
        name="project_prompt",
    )(x, w_in, conv_w)


def _project_sample(x, w_in, conv_w, hist1, hist2, period):
    r = x.shape[0]
    full = lambda shape: pl.BlockSpec(shape, lambda i: (0,) * len(shape))
    wide = (r, CONV_DIM)
    return pl.pallas_call(
        partial(_proj_sample_kernel, period=period),
        grid=(1,),
        in_specs=[full((r, D_MODEL)), full((D_MODEL, IN_WIDTH)), full((CONV_W, CONV_DIM)),
                  full(wide), full(wide)],
        out_specs=(full(wide),) * 5,
        out_shape=(jax.ShapeDtypeStruct(wide, _BF16),) + (jax.ShapeDtypeStruct(wide, _F32),) * 4,
        compiler_params=pltpu.CompilerParams(
            dimension_semantics=("arbitrary",), vmem_limit_bytes=_vmem_limit(48 << 20)),
        name="project_sample",
    )(x, w_in, conv_w, hist1, hist2)


def _attn_prompt_kernel(qa_ref, qb_ref, k_ref, vt_ref, lq1_ref, lk1_ref, lq2_ref, lk2_ref, g_ref,
                        o_ref, s_even, s_odd, cmax_even, cmax_odd, m_ref, acc_ref, *, lambda_init):
    i = pl.program_id(1)
    ratio = K_TILE // Q_TILE
    chunks_per_ktile = K_TILE // CHUNK
    qs = (qa_ref[0], qb_ref[0])
    bufs = ((s_even, cmax_even), (s_odd, cmax_odd))

    n_full = _div_pow2(i, ratio)
    q_chunk0 = (i - n_full * ratio) * (Q_TILE // CHUNK)

    def col_max(st):
        part = jnp.max(st.reshape(K_TILE // V_DIM, V_DIM, Q_TILE), axis=0)
        return jnp.max(part, axis=0, keepdims=True)

    def scores(t, buf):
        s_ref, cmax_ref = buf
        kt = k_ref[0, pl.ds(pl.multiple_of(t * K_TILE, K_TILE), K_TILE), :]
        for c in range(2):
            st = _dot_nt(kt, qs[c])
            s_ref[c] = st
            cmax_ref[c] = col_max(st)

    def consume(t, buf, q_chunk):
        s_ref, cmax_ref = buf
        vt = vt_ref[0, t]
        for c in range(2):
            st = s_ref[c]
            if q_chunk is None:
                tile_max = cmax_ref[c]
            else:
                kc = _div_pow2(lax.broadcasted_iota(jnp.int32, st.shape, 0), CHUNK)
                qc = _div_pow2(lax.broadcasted_iota(jnp.int32, st.shape, 1), CHUNK) + q_chunk
                st = jnp.where(kc <= qc, st, -jnp.inf)
                tile_max = col_max(st)
            m = m_ref[c]
            m_new = jnp.maximum(m, tile_max)
            alpha = jnp.exp2(m - m_new)
            p = jnp.exp2(st - m_new).astype(_BF16)
            acc_ref[c] = alpha * acc_ref[c] + _dot(vt, p)
            m_ref[c] = m_new

    m_ref[...] = jnp.full(m_ref.shape, -jnp.inf, _F32)
    acc_ref[...] = jnp.zeros(acc_ref.shape, _F32)
    scores(0, bufs[0])

    def pair(p, carry):
        t = 2 * p
        scores(t + 1, bufs[1])
        consume(t, bufs[0], None)
        scores(t + 2, bufs[0])
        consume(t + 1, bufs[1], None)
        return carry

    n_pairs = _div_pow2(n_full, 2)
    lax.fori_loop(0, n_pairs, pair, 0)

    t0 = 2 * n_pairs
    two_left = n_full > t0

    @pl.when(two_left)
    def _():
        scores(n_full, bufs[1])

    consume(t0, bufs[0], q_chunk0 + (n_full - t0) * chunks_per_ktile)

    @pl.when(two_left)
    def _():
        consume(n_full, bufs[1], q_chunk0)

    lam = _lambda_value(lq1_ref[...], lk1_ref[...], lq2_ref[...], lk2_ref[...], lambda_init)
    acc0 = acc_ref[0]
    acc1 = acc_ref[1]
    l0 = acc0[V_DIM:V_DIM + 1, :]
    l1 = acc1[V_DIM:V_DIM + 1, :]
    o = acc0[:V_DIM] / l0 - lam * (acc1[:V_DIM] / l1)
    ms = jnp.mean(o * o, axis=0, keepdims=True)
    on = o * lax.rsqrt(ms + LN_EPS)
    on = on.T * g_ref[...] * (1.0 - lambda_init)
    o_ref[...] = on.astype(_BF16)


def _attention_prompt(qa, qb, kb, vt, lam_vecs, subln_g, lambda_init):
    s = qa.shape[1]
    assert K_TILE % Q_TILE == 0 and s % K_TILE == 0 and Q_TILE % CHUNK == 0
    nq = s // Q_TILE
    vec = pl.BlockSpec((1, QK_DIM), lambda h, i: (0, 0))
    return pl.pallas_call(
        partial(_attn_prompt_kernel, lambda_init=lambda_init),
        grid=(N_HEADS, nq),
        in_specs=[
            pl.BlockSpec((1, Q_TILE, HEAD_W), lambda h, i: (h, i, 0)),
            pl.BlockSpec((1, Q_TILE, HEAD_W), lambda h, i: (h, i, 0)),
            pl.BlockSpec((1, s, HEAD_W), lambda h, i: (h, 0, 0)),
            pl.BlockSpec((1, s // K_TILE, V_EXT, K_TILE), lambda h, i: (h, 0, 0, 0)),
            vec, vec, vec, vec,
            pl.BlockSpec((1, V_DIM), lambda h, i: (0, 0)),
        ],
        out_specs=pl.BlockSpec((Q_TILE, V_DIM), lambda h, i: (i, h)),
        out_shape=jax.ShapeDtypeStruct((s, ATTN_DIM), _BF16),
        scratch_shapes=[
            pltpu.VMEM((2, K_TILE, Q_TILE), _F32), pltpu.VMEM((2, K_TILE, Q_TILE), _F32),
            pltpu.VMEM((2, 1, Q_TILE), _F32), pltpu.VMEM((2, 1, Q_TILE), _F32),
            pltpu.VMEM((2, 1, Q_TILE), _F32), pltpu.VMEM((2, V_EXT, Q_TILE), _F32),
        ],
        compiler_params=pltpu.CompilerParams(
            dimension_semantics=("arbitrary", "arbitrary"), vmem_limit_bytes=_vmem_limit(40 << 20)),
        name="attention_prompt",
    )(qa, qb, kb, vt, *lam_vecs, subln_g)


def _attn_sample_kernel(q_ref, kn_ref, vn_ref, ck_ref, cv_ref, lq1_ref, lk1_ref, lq2_ref, lk2_ref,
                        g_ref, o_ref, *, lambda_init, past_len, n_new):
    group = 2 * N_HEADS
    width = group * n_new
    assert width == HEAD_W
    qb = q_ref[...].astype(_BF16)
    sel_r = lax.broadcasted_iota(jnp.int32, (n_new, width), 0)
    sel_c = lax.broadcasted_iota(jnp.int32, (n_new, width), 1)
    spread = jnp.where(_mod_pow2(sel_c, n_new) == sel_r, 1.0, 0.0).astype(_BF16)
    q_all = _dot_tn(qb, spread)
    blk_r = _div_pow2(lax.broadcasted_iota(jnp.int32, (ATTN_DIM, width), 0), QK_DIM)
    blk_c = _div_pow2(lax.broadcasted_iota(jnp.int32, (ATTN_DIM, width), 1), n_new)
    q_bd = jnp.where(blk_r == blk_c, q_all, 0.0).astype(_BF16)

    s_old = _dot(ck_ref[0].astype(_BF16), q_bd)
    s_new = _dot(kn_ref[...].astype(_BF16), q_bd)

    def chunk_mask(shape, k_off):
        k_pos = k_off + lax.broadcasted_iota(jnp.int32, shape, 0)
        q_pos = past_len + _mod_pow2(lax.broadcasted_iota(jnp.int32, shape, 1), n_new)
        return _div_pow2(k_pos, CHUNK) <= _div_pow2(q_pos, CHUNK)

    s_old = jnp.where(chunk_mask(s_old.shape, 0), s_old, -jnp.inf)
    s_new = jnp.where(chunk_mask(s_new.shape, past_len), s_new, -jnp.inf)
    m = jnp.maximum(jnp.max(s_old, axis=0, keepdims=True), jnp.max(s_new, axis=0, keepdims=True))
    p_old = jnp.exp(s_old - m)
    p_new = jnp.exp(s_new - m)
    inv_l = 1.0 / (jnp.sum(p_old, axis=0, keepdims=True) + jnp.sum(p_new, axis=0, keepdims=True))
    a_old = (p_old * inv_l).astype(_BF16)
    a_new = (p_new * inv_l).astype(_BF16)
    pv = _dot_tn(a_old, cv_ref[0].astype(_BF16)) + _dot_tn(a_new, vn_ref[...].astype(_BF16))

    lam = _lambda_value(lq1_ref[...], lk1_ref[...], lq2_ref[...], lk2_ref[...], lambda_init)
    g = g_ref[...]
    for h in range(N_HEADS):
        cols = slice(h * V_DIM, (h + 1) * V_DIM)
        r0 = h * 2 * n_new
        o = pv[r0:r0 + n_new, cols] - lam * pv[r0 + n_new:r0 + 2 * n_new, cols]
        ms = jnp.mean(o * o, axis=-1, keepdims=True)
        o_ref[:, cols] = (o * lax.rsqrt(ms + LN_EPS) * g * (1.0 - lambda_init)).astype(_BF16)


def _attention_sample(q, k_new, v_new, cache_k, cache_v, lam_vecs, subln_g, lambda_init, n_new):
    nb, past_len = cache_k.shape[:2]
    rows = lambda b: (b, 0)
    vec = pl.BlockSpec((1, QK_DIM), lambda b: (0, 0))
    return pl.pallas_call(
        partial(_attn_sample_kernel, lambda_init=lambda_init, past_len=past_len, n_new=n_new),
        grid=(nb,),
        in_specs=[
            pl.BlockSpec((n_new, ATTN_DIM), rows),
            pl.BlockSpec((n_new, ATTN_DIM), rows),
            pl.BlockSpec((n_new, ATTN_DIM), rows),
            pl.BlockSpec((1, past_len, ATTN_DIM), lambda b: (b, 0, 0)),
            pl.BlockSpec((1, past_len, ATTN_DIM), lambda b: (b, 0, 0)),
            vec, vec, vec, vec,
            pl.BlockSpec((1, V_DIM), lambda b: (0, 0)),
        ],
        out_specs=pl.BlockSpec((n_new, ATTN_DIM), rows),
        out_shape=jax.ShapeDtypeStruct((nb * n_new, ATTN_DIM), _BF16),
        compiler_params=pltpu.CompilerParams(
            dimension_semantics=("arbitrary",), vmem_limit_bytes=_vmem_limit(40 << 20)),
        name="attention_sample",
    )(q, k_new, v_new, cache_k, cache_v, *lam_vecs, subln_g)


def _finish_kernel(x_ref, yconv_ref, attn_ref, wo_ref, g1_ref, b1_ref, w1_ref, w2_ref, g2_ref, b2_ref,
                   y_ref):
    mixed = jnp.concatenate([yconv_ref[...], attn_ref[...]], axis=-1)
    x1 = _layer_norm(ALPHA * x_ref[...] + _dot(mixed, wo_ref[...]), g1_ref[...], b1_ref[...])
    x1b = x1.astype(_BF16)
    ff = jnp.zeros_like(x1)
    for c in range(D_FF // FF_CHUNK):
        cols = slice(c * FF_CHUNK, (c + 1) * FF_CHUNK)
        hdn = jnp.square(jnp.maximum(_dot(x1b, w1_ref[:, cols]), 0.0))
        ff = ff + _dot(hdn.astype(_BF16), w2_ref[cols, :])
    y_ref[...] = _layer_norm(ALPHA * x1 + ff, g2_ref[...], b2_ref[...])


def _finish(x, yconv, attn, w_out, g1, b1, w_ff1, w_ff2, g2, b2):
    r = x.shape[0]
    assert r % ROW_TILE == 0
    row = lambda i: (i, 0)
    const = lambda shape: pl.BlockSpec(shape, lambda i: (0, 0))
    return pl.pallas_call(
        _finish_kernel,
        grid=(r // ROW_TILE,),
        in_specs=[
            pl.BlockSpec((ROW_TILE, D_MODEL), row),
            pl.BlockSpec((ROW_TILE, CONV_DIM), row),
            pl.BlockSpec((ROW_TILE, ATTN_DIM), row),
            const((D_MODEL, D_MODEL)), const((1, D_MODEL)), const((1, D_MODEL)),
            const((D_MODEL, D_FF)), const((D_FF, D_MODEL)), const((1, D_MODEL)), const((1, D_MODEL)),
        ],
        out_specs=pl.BlockSpec((ROW_TILE, D_MODEL), row),
        out_shape=jax.ShapeDtypeStruct((r, D_MODEL), _F32),
        compiler_params=pltpu.CompilerParams(
            dimension_semantics=("arbitrary",), vmem_limit_bytes=_vmem_limit(56 << 20)),
        name="finish_layer",
    )(x, yconv, attn, w_out, g1, b1, w_ff1, w_ff2, g2, b2)


def kernel(x_prompt, x_sample, cache_k, cache_v, state_conv, w_in, conv_w, lambda_q1, lambda_k1,
           lambda_q2, lambda_k2, subln_g, w_out, ln1_g, ln1_b, w_ff1, w_ff2, ln2_g, ln2_b):
    bp, sp, _ = x_prompt.shape
    bs, ss, _ = x_sample.shape
    depth = w_in.shape[0]
    assert bp == 1 and depth == 1
    past_len = cache_k.shape[2]
    l = 0
    lambda_init = 0.8 - 0.6 * float(np.exp(-0.3 * l))

    w_in_b = w_in[l].astype(_BF16)
    w_out_b = w_out[l].astype(_BF16)
    w_ff1_b = w_ff1[l].astype(_BF16)
    w_ff2_b = w_ff2[l].astype(_BF16)
    lam_vecs = tuple(v[l].reshape(1, QK_DIM) for v in (lambda_q1, lambda_k1, lambda_q2, lambda_k2))
    g_sub = subln_g[l].reshape(1, V_DIM)
    ln = tuple(v[l].reshape(1, D_MODEL) for v in (ln1_g, ln1_b, ln2_g, ln2_b))

    def finish(x2d, yconv, attn):
        return _finish(x2d, yconv, attn, w_out_b, ln[0], ln[1], w_ff1_b, w_ff2_b, ln[2], ln[3])

    xp2 = x_prompt.reshape(sp, D_MODEL)
    yconv_p, k_p, v_p, qa, qb, kb, vt, conv_p = _project_prompt(xp2, w_in_b, conv_w[l])
    attn_p = _attention_prompt(qa, qb, kb, vt, lam_vecs, g_sub, lambda_init)
    y_prompt = finish(xp2, yconv_p, attn_p).reshape(bp, sp, D_MODEL)

    xs2 = x_sample.reshape(bs * ss, D_MODEL)
    st = state_conv[l].astype(_F32)
    pad = lambda a: jnp.pad(a, ((0, 0), (0, ss - a.shape[1]), (0, 0))).reshape(bs * ss, CONV_DIM)
    hist2 = pad(st)
    hist1 = pad(st[:, 1:])
    yconv_s, q_s, k_s, v_s, u_s = _project_sample(xs2, w_in_b, conv_w[l], hist1, hist2, ss)
    ck = cache_k.reshape(depth * bs, past_len, ATTN_DIM)
    cv = cache_v.reshape(depth * bs, past_len, ATTN_DIM)
    attn_s = _attention_sample(q_s, k_s, v_s, ck, cv, lam_vecs, g_sub, lambda_init, ss)
    y_sample = finish(xs2, yconv_s, attn_s).reshape(bs, ss, D_MODEL)

    k_prompt = k_p.reshape(depth, bp, sp, N_HEADS, HEAD_W)
    v_prompt = v_p.reshape(depth, bp, sp, N_HEADS, V_DIM)
    conv_prompt = conv_p.reshape(depth, bp, CONV_W - 1, CONV_DIM)
    k_sample = k_s.reshape(depth, bs, ss, N_HEADS, HEAD_W)
    v_sample = v_s.reshape(depth, bs, ss, N_HEADS, V_DIM)
    conv_sample = u_s.reshape(bs, ss, CONV_DIM)[:, ss - (CONV_W - 1):].reshape(
        depth, bs, CONV_W - 1, CONV_DIM)
    return (y_prompt, y_sample, k_prompt, v_prompt, conv_prompt, k_sample, v_sample, conv_sample)
```

```python
from functools import partial

import jax
import jax.numpy as jnp
import numpy as np
from jax import lax
from jax.experimental import pallas as pl
from jax.experimental.pallas import tpu as pltpu

D_MODEL = 1024
CHUNK = 64
CONV_DIM = 512
ATTN_DIM = 512
QK_DIM = 64
N_HEADS = 4
HEAD_W = 2 * QK_DIM
V_DIM = 128
BF16_SUBLANES = 16
V_EXT = V_DIM + BF16_SUBLANES
LOG2E = 1.4426950408889634
CONV_W = 3
D_FF = 4096
LN_EPS = 1e-5
DEPTH = 1
ALPHA = (2 * DEPTH) ** 0.25
IN_WIDTH = 3 * CONV_DIM + 3 * ATTN_DIM

V7X_MXU_DIM = 256
V7X_VMEM_BYTES = 64 * 1024 * 1024

ROW_TILE = 512
Q_TILE = V7X_MXU_DIM
K_TILE = 4 * V7X_MXU_DIM
PROJ_TILE = K_TILE
FF_CHUNK = 1024

_F32 = jnp.float32
_BF16 = jnp.bfloat16


def _vmem_limit(nbytes):
    assert nbytes <= V7X_VMEM_BYTES - (4 << 20)
    return int(nbytes)


def _dot(a, b):
    return jnp.dot(a, b, preferred_element_type=_F32)


def _dot_nt(a, b):
    return lax.dot_general(a, b, (((1,), (1,)), ((), ())), preferred_element_type=_F32)


def _dot_tn(a, b):
    return lax.dot_general(a, b, (((0,), (0,)), ((), ())), preferred_element_type=_F32)


def _div_pow2(x, n):
    assert n & (n - 1) == 0
    return lax.shift_right_logical(x, n.bit_length() - 1)


def _mod_pow2(x, n):
    assert n & (n - 1) == 0
    return lax.bitwise_and(x, n - 1)


def _layer_norm(x, g, b):
    mu = jnp.mean(x, axis=-1, keepdims=True)
    xc = x - mu
    var = jnp.mean(xc * xc, axis=-1, keepdims=True)
    return xc * lax.rsqrt(var + LN_EPS) * g + b


def _lambda_value(lq1, lk1, lq2, lk2, lambda_init):
    s1 = jnp.sum(lq1 * lk1, axis=-1, keepdims=True)
    s2 = jnp.sum(lq2 * lk2, axis=-1, keepdims=True)
    return jnp.exp(s1) - jnp.exp(s2) + lambda_init


def _project_columns(x_ref, w_ref):
    xb = x_ref[...].astype(_BF16)
    return [_dot(xb, w_ref[:, g * CONV_DIM:(g + 1) * CONV_DIM]) for g in range(IN_WIDTH // CONV_DIM)]


def _conv_from_taps(u, u1, u2, cw_ref):
    return cw_ref[0:1, :] * u2 + cw_ref[1:2, :] * u1 + cw_ref[2:3, :] * u


def _proj_prompt_kernel(x_ref, w_ref, cw_ref,
                        yconv_ref, kf_ref, vf_ref, qa_ref, qb_ref, kb_ref, vt_ref, cstate_ref,
                        carry_ref):
    i = pl.program_id(0)

    @pl.when(i == 0)
    def _():
        carry_ref[...] = jnp.zeros_like(carry_ref)

    gate, cc, hh, q, k, v = _project_columns(x_ref, w_ref)
    u = cc * hh
    rows = lax.broadcasted_iota(jnp.int32, u.shape, 0)
    prev2 = carry_ref[0:1, :]
    prev1 = carry_ref[1:2, :]
    u1 = jnp.where(rows == 0, prev1, pltpu.roll(u, 1, 0))
    u2 = jnp.where(rows == 0, prev2, jnp.where(rows == 1, prev1, pltpu.roll(u, 2, 0)))
    yconv_ref[...] = (gate * _conv_from_taps(u, u1, u2, cw_ref)).astype(_BF16)
    tail = u[PROJ_TILE - (CONV_W - 1):, :]
    carry_ref[0:CONV_W - 1, :] = tail
    cstate_ref[...] = tail

    for h in range(N_HEADS):
        sl = slice(h * HEAD_W, (h + 1) * HEAD_W)
        kf_ref[pl.ds(h, PROJ_TILE, stride=N_HEADS), :] = k[:, sl]
        vf_ref[pl.ds(h, PROJ_TILE, stride=N_HEADS), :] = v[:, sl]
    qs = q * (QK_DIM ** -0.5 * LOG2E)
    lane = lax.broadcasted_iota(jnp.int32, (PROJ_TILE, HEAD_W), 1)
    first = lane < QK_DIM
    ones_row = jnp.where(lax.broadcasted_iota(jnp.int32, (BF16_SUBLANES, K_TILE), 0) == 0,
                         1.0, 0.0).astype(_BF16)
    for h in range(N_HEADS):
        sl = slice(h * HEAD_W, (h + 1) * HEAD_W)
        qh = qs[:, sl]
        qa_ref[h] = jnp.where(first, qh, 0.0).astype(_BF16)
        qb_ref[h] = jnp.where(first, 0.0, qh).astype(_BF16)
        kb_ref[h] = k[:, sl].astype(_BF16)
        vt_ref[h, 0, :V_DIM, :] = v[:, sl].T.astype(_BF16)
        vt_ref[h, 0, V_DIM:, :] = ones_row


def _proj_sample_kernel(x_ref, w_ref, cw_ref, h1_ref, h2_ref,
                        yconv_ref, q_ref, k_ref, v_ref, u_ref, *, period):
    gate, cc, hh, q, k, v = _project_columns(x_ref, w_ref)
    u = cc * hh
    t = _mod_pow2(lax.broadcasted_iota(jnp.int32, u.shape, 0), period)
    u1 = jnp.where(t < 1, h1_ref[...], pltpu.roll(u, 1, 0))
    u2 = jnp.where(t < 2, h2_ref[...], pltpu.roll(u, 2, 0))
    yconv_ref[...] = (gate * _conv_from_taps(u, u1, u2, cw_ref)).astype(_BF16)
    q_ref[...] = q * (QK_DIM ** -0.5)
    k_ref[...] = k
    v_ref[...] = v
    u_ref[...] = u


def _project_prompt(x, w_in, conv_w):
    s = x.shape[0]
    assert s % PROJ_TILE == 0 and PROJ_TILE == K_TILE
    n = s // PROJ_TILE
    row = lambda i: (i, 0)
    head_row = lambda i: (0, i, 0)
    out_shape = (
        jax.ShapeDtypeStruct((s, CONV_DIM), _BF16),
        jax.ShapeDtypeStruct((s * N_HEADS, HEAD_W), _F32),
        jax.ShapeDtypeStruct((s * N_HEADS, V_DIM), _F32),
        jax.ShapeDtypeStruct((N_HEADS, s, HEAD_W), _BF16),
        jax.ShapeDtypeStruct((N_HEADS, s, HEAD_W), _BF16),
        jax.ShapeDtypeStruct((N_HEADS, s, HEAD_W), _BF16),
        jax.ShapeDtypeStruct((N_HEADS, s // K_TILE, V_EXT, K_TILE), _BF16),
        jax.ShapeDtypeStruct((CONV_W - 1, CONV_DIM), _F32),
    )
    out_specs = (
        pl.BlockSpec((PROJ_TILE, CONV_DIM), row),
        pl.BlockSpec((PROJ_TILE * N_HEADS, HEAD_W), row),
        pl.BlockSpec((PROJ_TILE * N_HEADS, V_DIM), row),
        pl.BlockSpec((N_HEADS, PROJ_TILE, HEAD_W), head_row),
        pl.BlockSpec((N_HEADS, PROJ_TILE, HEAD_W), head_row),
        pl.BlockSpec((N_HEADS, PROJ_TILE, HEAD_W), head_row),
        pl.BlockSpec((N_HEADS, 1, V_EXT, K_TILE), lambda i: (0, i, 0, 0)),
        pl.BlockSpec((CONV_W - 1, CONV_DIM), lambda i: (0, 0)),
    )
    return pl.pallas_call(
        _proj_prompt_kernel,
        grid=(n,),
        in_specs=[
            pl.BlockSpec((PROJ_TILE, D_MODEL), row),
            pl.BlockSpec((D_MODEL, IN_WIDTH), lambda i: (0, 0)),
            pl.BlockSpec((CONV_W, CONV_DIM), lambda i: (0, 0)),
        ],
        out_specs=out_specs,
        out_shape=out_shape,
        scratch_shapes=[pltpu.VMEM((8, CONV_DIM), _F32)],
        compiler_params=pltpu.CompilerParams(
            dimension_semantics=("arbitrary",), vmem_limit_bytes=_vmem_limit(48 << 20)),
        name="project_prompt",
    )(x, w_in, conv_w)


def _project_sample(x, w_in, conv_w, hist1, hist2, period):
    r = x.shape[0]
    full = lambda shape: pl.BlockSpec(shape, lambda i: (0,) * len(shape))
    wide = (r, CONV_DIM)
    return pl.pallas_call(
        partial(_proj_sample_kernel, period=period),
        grid=(1,),
        in_specs=[full((r, D_MODEL)), full((D_MODEL, IN_WIDTH)), full((CONV_W, CONV_DIM)),
                  full(wide), full(wide)],
        out_specs=(full(wide),) * 5,
        out_shape=(jax.ShapeDtypeStruct(wide, _BF16),) + (jax.ShapeDtypeStruct(wide, _F32),) * 4,
        compiler_params=pltpu.CompilerParams(
            dimension_semantics=("arbitrary",), vmem_limit_bytes=_vmem_limit(48 << 20)),
        name="project_sample",
    )(x, w_in, conv_w, hist1, hist2)


def _attn_prompt_kernel(qa_ref, qb_ref, k_ref, vt_ref, lq1_ref, lk1_ref, lq2_ref, lk2_ref, g_ref,
                        o_ref, s_even, s_odd, cmax_even, cmax_odd, m_ref, acc_ref, *, lambda_init):
    i = pl.program_id(1)
    ratio = K_TILE // Q_TILE
    chunks_per_ktile = K_TILE // CHUNK
    qs = (qa_ref[0], qb_ref[0])
    bufs = ((s_even, cmax_even), (s_odd, cmax_odd))

    n_full = _div_pow2(i, ratio)
    q_chunk0 = (i - n_full * ratio) * (Q_TILE // CHUNK)

    def col_max(st):
        part = jnp.max(st.reshape(K_TILE // V_DIM, V_DIM, Q_TILE), axis=0)
        return jnp.max(part, axis=0, keepdims=True)

    def scores(t, buf):
        s_ref, cmax_ref = buf
        kt = k_ref[0, pl.ds(pl.multiple_of(t * K_TILE, K_TILE), K_TILE), :]
        for c in range(2):
            st = _dot_nt(kt, qs[c])
            s_ref[c] = st
            cmax_ref[c] = col_max(st)

    def consume(t, buf, q_chunk):
        s_ref, cmax_ref = buf
        vt = vt_ref[0, t]
        for c in range(2):
            st = s_ref[c]
            if q_chunk is None:
                tile_max = cmax_ref[c]
            else:
                kc = _div_pow2(lax.broadcasted_iota(jnp.int32, st.shape, 0), CHUNK)
                qc = _div_pow2(lax.broadcasted_iota(jnp.int32, st.shape, 1), CHUNK) + q_chunk
                st = jnp.where(kc <= qc, st, -jnp.inf)
                tile_max = col_max(st)
            m = m_ref[c]
            m_new = jnp.maximum(m, tile_max)
            alpha = jnp.exp2(m - m_new)
            p = jnp.exp2(st - m_new).astype(_BF16)
            acc_ref[c] = alpha * acc_ref[c] + _dot(vt, p)
            m_ref[c] = m_new

    m_ref[...] = jnp.full(m_ref.shape, -jnp.inf, _F32)
    acc_ref[...] = jnp.zeros(acc_ref.shape, _F32)
    scores(0, bufs[0])

    def pair(p, carry):
        t = 2 * p
        scores(t + 1, bufs[1])
        consume(t, bufs[0], None)
        scores(t + 2, bufs[0])
        consume(t + 1, bufs[1], None)
        return carry

    n_pairs = _div_pow2(n_full, 2)
    lax.fori_loop(0, n_pairs, pair, 0)

    t0 = 2 * n_pairs
    two_left = n_full > t0

    @pl.when(two_left)
    def _():
        scores(n_full, bufs[1])

    consume(t0, bufs[0], q_chunk0 + (n_full - t0) * chunks_per_ktile)

    @pl.when(two_left)
    def _():
        consume(n_full, bufs[1], q_chunk0)

    lam = _lambda_value(lq1_ref[...], lk1_ref[...], lq2_ref[...], lk2_ref[...], lambda_init)
    acc0 = acc_ref[0]
    acc1 = acc_ref[1]
    l0 = acc0[V_DIM:V_DIM + 1, :]
    l1 = acc1[V_DIM:V_DIM + 1, :]
    o = acc0[:V_DIM] / l0 - lam * (acc1[:V_DIM] / l1)
    ms = jnp.mean(o * o, axis=0, keepdims=True)
    on = o * lax.rsqrt(ms + LN_EPS)
    on = on.T * g_ref[...] * (1.0 - lambda_init)
    o_ref[...] = on.astype(_BF16)


def _attention_prompt(qa, qb, kb, vt, lam_vecs, subln_g, lambda_init):
    s = qa.shape[1]
    assert K_TILE % Q_TILE == 0 and s % K_TILE == 0 and Q_TILE % CHUNK == 0
    nq = s // Q_TILE
    vec = pl.BlockSpec((1, QK_DIM), lambda h, i: (0, 0))
    return pl.pallas_call(
        partial(_attn_prompt_kernel, lambda_init=lambda_init),
        grid=(N_HEADS, nq),
        in_specs=[
            pl.BlockSpec((1, Q_TILE, HEAD_W), lambda h, i: (h, i, 0)),
            pl.BlockSpec((1, Q_TILE, HEAD_W), lambda h, i: (h, i, 0)),
            pl.BlockSpec((1, s, HEAD_W), lambda h, i: (h, 0, 0)),
            pl.BlockSpec((1, s // K_TILE, V_EXT, K_TILE), lambda h, i: (h, 0, 0, 0)),
            vec, vec, vec, vec,
            pl.BlockSpec((1, V_DIM), lambda h, i: (0, 0)),
        ],
        out_specs=pl.BlockSpec((Q_TILE, V_DIM), lambda h, i: (i, h)),
        out_shape=jax.ShapeDtypeStruct((s, ATTN_DIM), _BF16),
        scratch_shapes=[
            pltpu.VMEM((2, K_TILE, Q_TILE), _F32), pltpu.VMEM((2, K_TILE, Q_TILE), _F32),
            pltpu.VMEM((2, 1, Q_TILE), _F32), pltpu.VMEM((2, 1, Q_TILE), _F32),
            pltpu.VMEM((2, 1, Q_TILE), _F32), pltpu.VMEM((2, V_EXT, Q_TILE), _F32),
        ],
        compiler_params=pltpu.CompilerParams(
            dimension_semantics=("arbitrary", "arbitrary"), vmem_limit_bytes=_vmem_limit(40 << 20)),
        name="attention_prompt",
    )(qa, qb, kb, vt, *lam_vecs, subln_g)


def _attn_sample_kernel(q_ref, kn_ref, vn_ref, ck_ref, cv_ref, lq1_ref, lk1_ref, lq2_ref, lk2_ref,
                        g_ref, o_ref, *, lambda_init, past_len, n_new):
    group = 2 * N_HEADS
    width = group * n_new
    assert width == HEAD_W
    qb = q_ref[...].astype(_BF16)
    sel_r = lax.broadcasted_iota(jnp.int32, (n_new, width), 0)
    sel_c = lax.broadcasted_iota(jnp.int32, (n_new, width), 1)
    spread = jnp.where(_mod_pow2(sel_c, n_new) == sel_r, 1.0, 0.0).astype(_BF16)
    q_all = _dot_tn(qb, spread)
    blk_r = _div_pow2(lax.broadcasted_iota(jnp.int32, (ATTN_DIM, width), 0), QK_DIM)
    blk_c = _div_pow2(lax.broadcasted_iota(jnp.int32, (ATTN_DIM, width), 1), n_new)
    q_bd = jnp.where(blk_r == blk_c, q_all, 0.0).astype(_BF16)

    def heads_on_lanes(ref):
        return jnp.concatenate(
            [ref[0, pl.ds(h, past_len, stride=N_HEADS), :].astype(_BF16) for h in range(N_HEADS)],
            axis=-1)

    s_old = _dot(heads_on_lanes(ck_ref), q_bd)
    s_new = _dot(kn_ref[...].astype(_BF16), q_bd)

    def chunk_mask(shape, k_off):
        k_pos = k_off + lax.broadcasted_iota(jnp.int32, shape, 0)
        q_pos = past_len + _mod_pow2(lax.broadcasted_iota(jnp.int32, shape, 1), n_new)
        return _div_pow2(k_pos, CHUNK) <= _div_pow2(q_pos, CHUNK)

    s_old = jnp.where(chunk_mask(s_old.shape, 0), s_old, -jnp.inf)
    s_new = jnp.where(chunk_mask(s_new.shape, past_len), s_new, -jnp.inf)
    m = jnp.maximum(jnp.max(s_old, axis=0, keepdims=True), jnp.max(s_new, axis=0, keepdims=True))
    p_old = jnp.exp(s_old - m)
    p_new = jnp.exp(s_new - m)
    inv_l = 1.0 / (jnp.sum(p_old, axis=0, keepdims=True) + jnp.sum(p_new, axis=0, keepdims=True))
    a_old = (p_old * inv_l).astype(_BF16)
    a_new = (p_new * inv_l).astype(_BF16)
    pv = _dot_tn(a_old, heads_on_lanes(cv_ref)) + _dot_tn(a_new, vn_ref[...].astype(_BF16))

    lam = _lambda_value(lq1_ref[...], lk1_ref[...], lq2_ref[...], lk2_ref[...], lambda_init)
    g = g_ref[...]
    for h in range(N_HEADS):
        cols = slice(h * V_DIM, (h + 1) * V_DIM)
        r0 = h * 2 * n_new
        o = pv[r0:r0 + n_new, cols] - lam * pv[r0 + n_new:r0 + 2 * n_new, cols]
        ms = jnp.mean(o * o, axis=-1, keepdims=True)
        o_ref[:, cols] = (o * lax.rsqrt(ms + LN_EPS) * g * (1.0 - lambda_init)).astype(_BF16)


def _attention_sample(q, k_new, v_new, cache_k, cache_v, lam_vecs, subln_g, lambda_init, n_new):
    nb, rows_per_stream = cache_k.shape[:2]
    past_len = rows_per_stream // N_HEADS
    cache_block = pl.BlockSpec((1, rows_per_stream, HEAD_W), lambda b: (b, 0, 0))
    rows = lambda b: (b, 0)
    vec = pl.BlockSpec((1, QK_DIM), lambda b: (0, 0))
    return pl.pallas_call(
        partial(_attn_sample_kernel, lambda_init=lambda_init, past_len=past_len, n_new=n_new),
        grid=(nb,),
        in_specs=[
            pl.BlockSpec((n_new, ATTN_DIM), rows),
            pl.BlockSpec((n_new, ATTN_DIM), rows),
            pl.BlockSpec((n_new, ATTN_DIM), rows),
            cache_block, cache_block,
            vec, vec, vec, vec,
            pl.BlockSpec((1, V_DIM), lambda b: (0, 0)),
        ],
        out_specs=pl.BlockSpec((n_new, ATTN_DIM), rows),
        out_shape=jax.ShapeDtypeStruct((nb * n_new, ATTN_DIM), _BF16),
        compiler_params=pltpu.CompilerParams(
            dimension_semantics=("arbitrary",), vmem_limit_bytes=_vmem_limit(40 << 20)),
        name="attention_sample",
    )(q, k_new, v_new, cache_k, cache_v, *lam_vecs, subln_g)


def _finish_kernel(x_ref, yconv_ref, attn_ref, wo_ref, g1_ref, b1_ref, w1_ref, w2_ref, g2_ref, b2_ref,
                   y_ref):
    mixed = jnp.concatenate([yconv_ref[...], attn_ref[...]], axis=-1)
    x1 = _layer_norm(ALPHA * x_ref[...] + _dot(mixed, wo_ref[...]), g1_ref[...], b1_ref[...])
    x1b = x1.astype(_BF16)
    ff = jnp.zeros_like(x1)
    for c in range(D_FF // FF_CHUNK):
        cols = slice(c * FF_CHUNK, (c + 1) * FF_CHUNK)
        hdn = jnp.square(jnp.maximum(_dot(x1b, w1_ref[:, cols]), 0.0))
        ff = ff + _dot(hdn.astype(_BF16), w2_ref[cols, :])
    y_ref[...] = _layer_norm(ALPHA * x1 + ff, g2_ref[...], b2_ref[...])


def _finish(x, yconv, attn, w_out, g1, b1, w_ff1, w_ff2, g2, b2):
    r = x.shape[0]
    assert r % ROW_TILE == 0
    row = lambda i: (i, 0)
    const = lambda shape: pl.BlockSpec(shape, lambda i: (0, 0))
    return pl.pallas_call(
        _finish_kernel,
        grid=(r // ROW_TILE,),
        in_specs=[
            pl.BlockSpec((ROW_TILE, D_MODEL), row),
            pl.BlockSpec((ROW_TILE, CONV_DIM), row),
            pl.BlockSpec((ROW_TILE, ATTN_DIM), row),
            const((D_MODEL, D_MODEL)), const((1, D_MODEL)), const((1, D_MODEL)),
            const((D_MODEL, D_FF)), const((D_FF, D_MODEL)), const((1, D_MODEL)), const((1, D_MODEL)),
        ],
        out_specs=pl.BlockSpec((ROW_TILE, D_MODEL), row),
        out_shape=jax.ShapeDtypeStruct((r, D_MODEL), _F32),
        compiler_params=pltpu.CompilerParams(
            dimension_semantics=("arbitrary",), vmem_limit_bytes=_vmem_limit(56 << 20)),
        name="finish_layer",
    )(x, yconv, attn, w_out, g1, b1, w_ff1, w_ff2, g2, b2)


def kernel(x_prompt, x_sample, cache_k, cache_v, state_conv, w_in, conv_w, lambda_q1, lambda_k1,
           lambda_q2, lambda_k2, subln_g, w_out, ln1_g, ln1_b, w_ff1, w_ff2, ln2_g, ln2_b):
    bp, sp, _ = x_prompt.shape
    bs, ss, _ = x_sample.shape
    depth = w_in.shape[0]
    assert bp == 1 and depth == 1
    past_len = cache_k.shape[2]
    l = 0
    lambda_init = 0.8 - 0.6 * float(np.exp(-0.3 * l))

    w_in_b = w_in[l].astype(_BF16)
    w_out_b = w_out[l].astype(_BF16)
    w_ff1_b = w_ff1[l].astype(_BF16)
    w_ff2_b = w_ff2[l].astype(_BF16)
    lam_vecs = tuple(v[l].reshape(1, QK_DIM) for v in (lambda_q1, lambda_k1, lambda_q2, lambda_k2))
    g_sub = subln_g[l].reshape(1, V_DIM)
    ln = tuple(v[l].reshape(1, D_MODEL) for v in (ln1_g, ln1_b, ln2_g, ln2_b))

    def finish(x2d, yconv, attn):
        return _finish(x2d, yconv, attn, w_out_b, ln[0], ln[1], w_ff1_b, w_ff2_b, ln[2], ln[3])

    xp2 = x_prompt.reshape(sp, D_MODEL)
    yconv_p, k_p, v_p, qa, qb, kb, vt, conv_p = _project_prompt(xp2, w_in_b, conv_w[l])
    attn_p = _attention_prompt(qa, qb, kb, vt, lam_vecs, g_sub, lambda_init)
    y_prompt = finish(xp2, yconv_p, attn_p).reshape(bp, sp, D_MODEL)

    xs2 = x_sample.reshape(bs * ss, D_MODEL)
    st = state_conv[l].astype(_F32)
    pad = lambda a: jnp.pad(a, ((0, 0), (0, ss - a.shape[1]), (0, 0))).reshape(bs * ss, CONV_DIM)
    hist2 = pad(st)
    hist1 = pad(st[:, 1:])
    yconv_s, q_s, k_s, v_s, u_s = _project_sample(xs2, w_in_b, conv_w[l], hist1, hist2, ss)
    ck = cache_k.reshape(depth * bs, past_len * N_HEADS, HEAD_W)
    cv = cache_v.reshape(depth * bs, past_len * N_HEADS, V_DIM)
    attn_s = _attention_sample(q_s, k_s, v_s, ck, cv, lam_vecs, g_sub, lambda_init, ss)
    y_sample = finish(xs2, yconv_s, attn_s).reshape(bs, ss, D_MODEL)

    k_prompt = k_p.reshape(depth, bp, sp, N_HEADS, HEAD_W)
    v_prompt = v_p.reshape(depth, bp, sp, N_HEADS, V_DIM)
    conv_prompt = conv_p.reshape(depth, bp, CONV_W - 1, CONV_DIM)
    k_sample = k_s.reshape(depth, bs, ss, N_HEADS, HEAD_W)
    v_sample = v_s.reshape(depth, bs, ss, N_HEADS, V_DIM)
    conv_sample = u_s.reshape(bs, ss, CONV_DIM)[:, ss - (CONV_W - 1):].reshape(
        depth, bs, CONV_W - 1, CONV_DIM)
    return (y_prompt, y_sample, k_prompt, v_prompt, conv_prompt, k_sample, v_sample, conv_sample)
```

```python
from functools import partial

import jax
import jax.numpy as jnp
import numpy as np
from jax import lax
from jax.experimental import pallas as pl
from jax.experimental.pallas import tpu as pltpu

D_MODEL = 1024
CHUNK = 64
CONV_DIM = 512
ATTN_DIM = 512
QK_DIM = 64
N_HEADS = 4
HEAD_W = 2 * QK_DIM
V_DIM = 128
BF16_SUBLANES = 16
V_EXT = V_DIM + BF16_SUBLANES
LOG2E = 1.4426950408889634
CONV_W = 3
D_FF = 4096
LN_EPS = 1e-5
DEPTH = 1
ALPHA = (2 * DEPTH) ** 0.25
IN_WIDTH = 3 * CONV_DIM + 3 * ATTN_DIM

V7X_MXU_DIM = 256
V7X_VMEM_BYTES = 64 * 1024 * 1024

ROW_TILE = 512
Q_TILE = 2 * V7X_MXU_DIM
K_TILE = 4 * V7X_MXU_DIM
PROJ_TILE = K_TILE
FF_CHUNK = 1024

_F32 = jnp.float32
_BF16 = jnp.bfloat16


def _vmem_limit(nbytes):
    assert nbytes <= V7X_VMEM_BYTES - (4 << 20)
    return int(nbytes)


def _dot(a, b):
    return jnp.dot(a, b, preferred_element_type=_F32)


def _dot_nt(a, b):
    return lax.dot_general(a, b, (((1,), (1,)), ((), ())), preferred_element_type=_F32)


def _dot_tn(a, b):
    return lax.dot_general(a, b, (((0,), (0,)), ((), ())), preferred_element_type=_F32)


def _div_pow2(x, n):
    assert n & (n - 1) == 0
    return lax.shift_right_logical(x, n.bit_length() - 1)


def _mod_pow2(x, n):
    assert n & (n - 1) == 0
    return lax.bitwise_and(x, n - 1)


def _layer_norm(x, g, b):
    mu = jnp.mean(x, axis=-1, keepdims=True)
    xc = x - mu
    var = jnp.mean(xc * xc, axis=-1, keepdims=True)
    return xc * lax.rsqrt(var + LN_EPS) * g + b


def _lambda_value(lq1, lk1, lq2, lk2, lambda_init):
    s1 = jnp.sum(lq1 * lk1, axis=-1, keepdims=True)
    s2 = jnp.sum(lq2 * lk2, axis=-1, keepdims=True)
    return jnp.exp(s1) - jnp.exp(s2) + lambda_init


def _project_columns(x_ref, w_ref):
    xb = x_ref[...].astype(_BF16)
    return [_dot(xb, w_ref[:, g * CONV_DIM:(g + 1) * CONV_DIM]) for g in range(IN_WIDTH // CONV_DIM)]


def _conv_from_taps(u, u1, u2, cw_ref):
    return cw_ref[0:1, :] * u2 + cw_ref[1:2, :] * u1 + cw_ref[2:3, :] * u


def _proj_prompt_kernel(x_ref, w_ref, cw_ref,
                        yconv_ref, kf_ref, vf_ref, qa_ref, qb_ref, ka_ref, kb_ref, vt_ref, cstate_ref,
                        carry_ref):
    i = pl.program_id(0)

    @pl.when(i == 0)
    def _():
        carry_ref[...] = jnp.zeros_like(carry_ref)

    gate, cc, hh, q, k, v = _project_columns(x_ref, w_ref)
    u = cc * hh
    rows = lax.broadcasted_iota(jnp.int32, u.shape, 0)
    prev2 = carry_ref[0:1, :]
    prev1 = carry_ref[1:2, :]
    u1 = jnp.where(rows == 0, prev1, pltpu.roll(u, 1, 0))
    u2 = jnp.where(rows == 0, prev2, jnp.where(rows == 1, prev1, pltpu.roll(u, 2, 0)))
    yconv_ref[...] = (gate * _conv_from_taps(u, u1, u2, cw_ref)).astype(_BF16)
    tail = u[PROJ_TILE - (CONV_W - 1):, :]
    carry_ref[0:CONV_W - 1, :] = tail
    cstate_ref[...] = tail

    for h in range(N_HEADS):
        sl = slice(h * HEAD_W, (h + 1) * HEAD_W)
        kf_ref[pl.ds(h, PROJ_TILE, stride=N_HEADS), :] = k[:, sl]
        vf_ref[pl.ds(h, PROJ_TILE, stride=N_HEADS), :] = v[:, sl]
    qs = q * (QK_DIM ** -0.5 * LOG2E)
    lane = lax.broadcasted_iota(jnp.int32, (PROJ_TILE, HEAD_W), 1)
    first = lane < QK_DIM
    ones_row = jnp.where(lax.broadcasted_iota(jnp.int32, (BF16_SUBLANES, K_TILE), 0) == 0,
                         1.0, 0.0).astype(_BF16)
    key_chunk = _div_pow2(lax.broadcasted_iota(jnp.int32, (PROJ_TILE, HEAD_W), 0), CHUNK)
    chunk_onehot = jnp.where(_mod_pow2(lane, K_TILE // CHUNK) == key_chunk, 1.0, 0.0)
    for h in range(N_HEADS):
        sl = slice(h * HEAD_W, (h + 1) * HEAD_W)
        qh = qs[:, sl]
        kh = k[:, sl]
        qa_ref[h] = jnp.where(first, qh, 0.0).astype(_BF16)
        qb_ref[h] = jnp.where(first, 0.0, qh).astype(_BF16)
        ka_ref[h] = jnp.where(first, kh, chunk_onehot).astype(_BF16)
        kb_ref[h] = jnp.where(first, chunk_onehot, kh).astype(_BF16)
        vt_ref[h, 0, :V_DIM, :] = v[:, sl].T.astype(_BF16)
        vt_ref[h, 0, V_DIM:, :] = ones_row


def _proj_sample_kernel(x_ref, w_ref, cw_ref, h1_ref, h2_ref,
                        yconv_ref, q_ref, k_ref, v_ref, u_ref, *, period):
    gate, cc, hh, q, k, v = _project_columns(x_ref, w_ref)
    u = cc * hh
    t = _mod_pow2(lax.broadcasted_iota(jnp.int32, u.shape, 0), period)
    u1 = jnp.where(t < 1, h1_ref[...], pltpu.roll(u, 1, 0))
    u2 = jnp.where(t < 2, h2_ref[...], pltpu.roll(u, 2, 0))
    yconv_ref[...] = (gate * _conv_from_taps(u, u1, u2, cw_ref)).astype(_BF16)
    q_ref[...] = q * (QK_DIM ** -0.5)
    k_ref[...] = k
    v_ref[...] = v
    u_ref[...] = u


def _project_prompt(x, w_in, conv_w):
    s = x.shape[0]
    assert s % PROJ_TILE == 0 and PROJ_TILE == K_TILE
    n = s // PROJ_TILE
    row = lambda i: (i, 0)
    head_row = lambda i: (0, i, 0)
    out_shape = (
        jax.ShapeDtypeStruct((s, CONV_DIM), _BF16),
        jax.ShapeDtypeStruct((s * N_HEADS, HEAD_W), _F32),
        jax.ShapeDtypeStruct((s * N_HEADS, V_DIM), _F32),
        jax.ShapeDtypeStruct((N_HEADS, s, HEAD_W), _BF16),
        jax.ShapeDtypeStruct((N_HEADS, s, HEAD_W), _BF16),
        jax.ShapeDtypeStruct((N_HEADS, s, HEAD_W), _BF16),
        jax.ShapeDtypeStruct((N_HEADS, s, HEAD_W), _BF16),
        jax.ShapeDtypeStruct((N_HEADS, s // K_TILE, V_EXT, K_TILE), _BF16),
        jax.ShapeDtypeStruct((CONV_W - 1, CONV_DIM), _F32),
    )
    out_specs = (
        pl.BlockSpec((PROJ_TILE, CONV_DIM), row),
        pl.BlockSpec((PROJ_TILE * N_HEADS, HEAD_W), row),
        pl.BlockSpec((PROJ_TILE * N_HEADS, V_DIM), row),
        pl.BlockSpec((N_HEADS, PROJ_TILE, HEAD_W), head_row),
        pl.BlockSpec((N_HEADS, PROJ_TILE, HEAD_W), head_row),
        pl.BlockSpec((N_HEADS, PROJ_TILE, HEAD_W), head_row),
        pl.BlockSpec((N_HEADS, PROJ_TILE, HEAD_W), head_row),
        pl.BlockSpec((N_HEADS, 1, V_EXT, K_TILE), lambda i: (0, i, 0, 0)),
        pl.BlockSpec((CONV_W - 1, CONV_DIM), lambda i: (0, 0)),
    )
    return pl.pallas_call(
        _proj_prompt_kernel,
        grid=(n,),
        in_specs=[
            pl.BlockSpec((PROJ_TILE, D_MODEL), row),
            pl.BlockSpec((D_MODEL, IN_WIDTH), lambda i: (0, 0)),
            pl.BlockSpec((CONV_W, CONV_DIM), lambda i: (0, 0)),
        ],
        out_specs=out_specs,
        out_shape=out_shape,
        scratch_shapes=[pltpu.VMEM((8, CONV_DIM), _F32)],
        compiler_params=pltpu.CompilerParams(
            dimension_semantics=("arbitrary",), vmem_limit_bytes=_vmem_limit(48 << 20)),
        name="project_prompt",
    )(x, w_in, conv_w)


def _project_sample(x, w_in, conv_w, hist1, hist2, period):
    r = x.shape[0]
    full = lambda shape: pl.BlockSpec(shape, lambda i: (0,) * len(shape))
    wide = (r, CONV_DIM)
    return pl.pallas_call(
        partial(_proj_sample_kernel, period=period),
        grid=(1,),
        in_specs=[full((r, D_MODEL)), full((D_MODEL, IN_WIDTH)), full((CONV_W, CONV_DIM)),
                  full(wide), full(wide)],
        out_specs=(full(wide),) * 5,
        out_shape=(jax.ShapeDtypeStruct(wide, _BF16),) + (jax.ShapeDtypeStruct(wide, _F32),) * 4,
        compiler_params=pltpu.CompilerParams(
            dimension_semantics=("arbitrary",), vmem_limit_bytes=_vmem_limit(48 << 20)),
        name="project_sample",
    )(x, w_in, conv_w, hist1, hist2)


def _attn_prompt_kernel(qa_ref, qb_ref, ka_ref, kb_ref, vt_ref, lq1_ref, lk1_ref, lq2_ref, lk2_ref,
                        g_ref, o_ref, s_even, s_odd, cmax_even, cmax_odd, m_ref, acc_ref,
                        *, lambda_init):
    i = pl.program_id(1)
    ratio = K_TILE // Q_TILE
    chunks_per_ktile = K_TILE // CHUNK
    mask_repeat = QK_DIM // chunks_per_ktile
    assert mask_repeat >= 2
    k_refs = (ka_ref, kb_ref)
    bufs = ((s_even, cmax_even), (s_odd, cmax_odd))

    n_full = _div_pow2(i, ratio)
    q_chunk0 = (i - n_full * ratio) * (Q_TILE // CHUNK)

    lane = lax.broadcasted_iota(jnp.int32, (Q_TILE, HEAD_W), 1)
    row_chunk = _div_pow2(lax.broadcasted_iota(jnp.int32, (Q_TILE, HEAD_W), 0), CHUNK) + q_chunk0
    hidden = _mod_pow2(lane, chunks_per_ktile) > row_chunk
    neg = float(jnp.finfo(_BF16).min)
    q_plain = (qa_ref[0], qb_ref[0])
    q_last = (jnp.where(hidden & (lane >= QK_DIM), neg, q_plain[0].astype(_F32)).astype(_BF16),
              jnp.where(hidden & (lane < QK_DIM), neg, q_plain[1].astype(_F32)).astype(_BF16))

    def col_max(st):
        part = jnp.max(st.reshape(K_TILE // V_DIM, V_DIM, Q_TILE), axis=0)
        return jnp.max(part, axis=0, keepdims=True)

    def scores(t, buf):
        s_ref, cmax_ref = buf
        rows = pl.ds(pl.multiple_of(t * K_TILE, K_TILE), K_TILE)
        for c in range(2):
            q = jnp.where(t == n_full, q_last[c], q_plain[c])
            st = _dot_nt(k_refs[c][0, rows, :], q)
            s_ref[c] = st
            cmax_ref[c] = col_max(st)

    def consume(t, buf):
        s_ref, cmax_ref = buf
        vt = vt_ref[0, t]
        for c in range(2):
            m = m_ref[c]
            m_new = jnp.maximum(m, cmax_ref[c])
            alpha = jnp.exp2(m - m_new)
            p = jnp.exp2(s_ref[c] - m_new).astype(_BF16)
            acc_ref[c] = alpha * acc_ref[c] + _dot(vt, p)
            m_ref[c] = m_new

    m_ref[...] = jnp.full(m_ref.shape, -jnp.inf, _F32)
    acc_ref[...] = jnp.zeros(acc_ref.shape, _F32)
    scores(0, bufs[0])

    def pair(p, carry):
        t = 2 * p
        scores(t + 1, bufs[1])
        consume(t, bufs[0])
        scores(t + 2, bufs[0])
        consume(t + 1, bufs[1])
        return carry

    n_pairs = _div_pow2(n_full, 2)
    lax.fori_loop(0, n_pairs, pair, 0)

    t0 = 2 * n_pairs
    two_left = n_full > t0

    @pl.when(two_left)
    def _():
        scores(n_full, bufs[1])
        consume(t0, bufs[0])
        consume(n_full, bufs[1])

    @pl.when(jnp.logical_not(two_left))
    def _():
        consume(t0, bufs[0])

    lam =_lambda_value(lq1_ref[...], lk1_ref[...], lq2_ref[...], lk2_ref[...], lambda_init)
    acc0 = acc_ref[0]
    acc1 = acc_ref[1]
    l0 = acc0[V_DIM:V_DIM + 1, :]
    l1 = acc1[V_DIM:V_DIM + 1, :]
    o = acc0[:V_DIM] / l0 - lam * (acc1[:V_DIM] / l1)
    ms = jnp.mean(o * o, axis=0, keepdims=True)
    on = o * lax.rsqrt(ms + LN_EPS)
    on = on.T * g_ref[...] * (1.0 - lambda_init)
    o_ref[...] = on.astype(_BF16)


def _attention_prompt(qa, qb, ka, kb, vt, lam_vecs, subln_g, lambda_init):
    s = qa.shape[1]
    assert K_TILE % Q_TILE == 0 and s % K_TILE == 0 and Q_TILE % CHUNK == 0
    nq = s // Q_TILE
    vec = pl.BlockSpec((1, QK_DIM), lambda h, i: (0, 0))
    return pl.pallas_call(
        partial(_attn_prompt_kernel, lambda_init=lambda_init),
        grid=(N_HEADS, nq),
        in_specs=[
            pl.BlockSpec((1, Q_TILE, HEAD_W), lambda h, i: (h, i, 0)),
            pl.BlockSpec((1, Q_TILE, HEAD_W), lambda h, i: (h, i, 0)),
            pl.BlockSpec((1, s, HEAD_W), lambda h, i: (h, 0, 0)),
            pl.BlockSpec((1, s, HEAD_W), lambda h, i: (h, 0, 0)),
            pl.BlockSpec((1, s // K_TILE, V_EXT, K_TILE), lambda h, i: (h, 0, 0, 0)),
            vec, vec, vec, vec,
            pl.BlockSpec((1, V_DIM), lambda h, i: (0, 0)),
        ],
        out_specs=pl.BlockSpec((Q_TILE, V_DIM), lambda h, i: (i, h)),
        out_shape=jax.ShapeDtypeStruct((s, ATTN_DIM), _BF16),
        scratch_shapes=[
            pltpu.VMEM((2, K_TILE, Q_TILE), _F32), pltpu.VMEM((2, K_TILE, Q_TILE), _F32),
            pltpu.VMEM((2, 1, Q_TILE), _F32), pltpu.VMEM((2, 1, Q_TILE), _F32),
            pltpu.VMEM((2, 1, Q_TILE), _F32), pltpu.VMEM((2, V_EXT, Q_TILE), _F32),
        ],
        compiler_params=pltpu.CompilerParams(
            dimension_semantics=("arbitrary", "arbitrary"), vmem_limit_bytes=_vmem_limit(48 << 20)),
        name="attention_prompt",
    )(qa, qb, ka, kb, vt, *lam_vecs, subln_g)


def _attn_sample_kernel(q_ref, kn_ref, vn_ref, ck_ref, cv_ref, lq1_ref, lk1_ref, lq2_ref, lk2_ref,
                        g_ref, o_ref, *, lambda_init, past_len, n_new):
    group = 2 * N_HEADS
    width = group * n_new
    assert width == HEAD_W
    qb = q_ref[...].astype(_BF16)
    sel_r = lax.broadcasted_iota(jnp.int32, (n_new, width), 0)
    sel_c = lax.broadcasted_iota(jnp.int32, (n_new, width), 1)
    spread = jnp.where(_mod_pow2(sel_c, n_new) == sel_r, 1.0, 0.0).astype(_BF16)
    q_all = _dot_tn(qb, spread)
    blk_r = _div_pow2(lax.broadcasted_iota(jnp.int32, (ATTN_DIM, width), 0), QK_DIM)
    blk_c = _div_pow2(lax.broadcasted_iota(jnp.int32, (ATTN_DIM, width), 1), n_new)
    q_bd = jnp.where(blk_r == blk_c, q_all, 0.0).astype(_BF16)

    def heads_on_lanes(ref):
        return jnp.concatenate(
            [ref[0, pl.ds(h, past_len, stride=N_HEADS), :].astype(_BF16) for h in range(N_HEADS)],
            axis=-1)

    s_old = _dot(heads_on_lanes(ck_ref), q_bd)
    s_new = _dot(kn_ref[...].astype(_BF16), q_bd)

    def chunk_mask(shape, k_off):
        k_pos = k_off + lax.broadcasted_iota(jnp.int32, shape, 0)
        q_pos = past_len + _mod_pow2(lax.broadcasted_iota(jnp.int32, shape, 1), n_new)
        return _div_pow2(k_pos, CHUNK) <= _div_pow2(q_pos, CHUNK)

    s_old = jnp.where(chunk_mask(s_old.shape, 0), s_old, -jnp.inf)
    s_new = jnp.where(chunk_mask(s_new.shape, past_len), s_new, -jnp.inf)
    m = jnp.maximum(jnp.max(s_old, axis=0, keepdims=True), jnp.max(s_new, axis=0, keepdims=True))
    p_old = jnp.exp(s_old - m)
    p_new = jnp.exp(s_new - m)
    inv_l = 1.0 / (jnp.sum(p_old, axis=0, keepdims=True) + jnp.sum(p_new, axis=0, keepdims=True))
    a_old = (p_old * inv_l).astype(_BF16)
    a_new = (p_new * inv_l).astype(_BF16)
    pv = _dot_tn(a_old, heads_on_lanes(cv_ref)) + _dot_tn(a_new, vn_ref[...].astype(_BF16))

    lam = _lambda_value(lq1_ref[...], lk1_ref[...], lq2_ref[...], lk2_ref[...], lambda_init)
    g = g_ref[...]
    for h in range(N_HEADS):
        cols = slice(h * V_DIM, (h + 1) * V_DIM)
        r0 = h * 2 * n_new
        o = pv[r0:r0 + n_new, cols] - lam * pv[r0 + n_new:r0 + 2 * n_new, cols]
        ms = jnp.mean(o * o, axis=-1, keepdims=True)
        o_ref[:, cols] = (o * lax.rsqrt(ms + LN_EPS) * g * (1.0 - lambda_init)).astype(_BF16)


def _attention_sample(q, k_new, v_new, cache_k, cache_v, lam_vecs, subln_g, lambda_init, n_new):
    nb, rows_per_stream = cache_k.shape[:2]
    past_len = rows_per_stream // N_HEADS
    cache_block = pl.BlockSpec((1, rows_per_stream, HEAD_W), lambda b: (b, 0, 0))
    rows = lambda b: (b, 0)
    vec = pl.BlockSpec((1, QK_DIM), lambda b: (0, 0))
    return pl.pallas_call(
        partial(_attn_sample_kernel, lambda_init=lambda_init, past_len=past_len, n_new=n_new),
        grid=(nb,),
        in_specs=[
            pl.BlockSpec((n_new, ATTN_DIM), rows),
            pl.BlockSpec((n_new, ATTN_DIM), rows),
            pl.BlockSpec((n_new, ATTN_DIM), rows),
            cache_block, cache_block,
            vec, vec, vec, vec,
            pl.BlockSpec((1, V_DIM), lambda b: (0, 0)),
        ],
        out_specs=pl.BlockSpec((n_new, ATTN_DIM), rows),
        out_shape=jax.ShapeDtypeStruct((nb * n_new, ATTN_DIM), _BF16),
        compiler_params=pltpu.CompilerParams(
            dimension_semantics=("arbitrary",), vmem_limit_bytes=_vmem_limit(40 << 20)),
        name="attention_sample",
    )(q, k_new, v_new, cache_k, cache_v, *lam_vecs, subln_g)


def _finish_kernel(x_ref, yconv_ref, attn_ref, wo_ref, g1_ref, b1_ref, w1_ref, w2_ref, g2_ref, b2_ref,
                   y_ref):
    mixed = jnp.concatenate([yconv_ref[...], attn_ref[...]], axis=-1)
    x1 = _layer_norm(ALPHA * x_ref[...] + _dot(mixed, wo_ref[...]), g1_ref[...], b1_ref[...])
    x1b = x1.astype(_BF16)
    ff = jnp.zeros_like(x1)
    for c in range(D_FF // FF_CHUNK):
        cols = slice(c * FF_CHUNK, (c + 1) * FF_CHUNK)
        hdn = jnp.square(jnp.maximum(_dot(x1b, w1_ref[:, cols]), 0.0))
        ff = ff + _dot(hdn.astype(_BF16), w2_ref[cols, :])
    y_ref[...] = _layer_norm(ALPHA * x1 + ff, g2_ref[...], b2_ref[...])


def _finish(x, yconv, attn, w_out, g1, b1, w_ff1, w_ff2, g2, b2):
    r = x.shape[0]
    assert r % ROW_TILE == 0
    row = lambda i: (i, 0)
    const = lambda shape: pl.BlockSpec(shape, lambda i: (0, 0))
    return pl.pallas_call(
        _finish_kernel,
        grid=(r // ROW_TILE,),
        in_specs=[
            pl.BlockSpec((ROW_TILE, D_MODEL), row),
            pl.BlockSpec((ROW_TILE, CONV_DIM), row),
            pl.BlockSpec((ROW_TILE, ATTN_DIM), row),
            const((D_MODEL, D_MODEL)), const((1, D_MODEL)), const((1, D_MODEL)),
            const((D_MODEL, D_FF)), const((D_FF, D_MODEL)), const((1, D_MODEL)), const((1, D_MODEL)),
        ],
        out_specs=pl.BlockSpec((ROW_TILE, D_MODEL), row),
        out_shape=jax.ShapeDtypeStruct((r, D_MODEL), _F32),
        compiler_params=pltpu.CompilerParams(
            dimension_semantics=("arbitrary",), vmem_limit_bytes=_vmem_limit(56 << 20)),
        name="finish_layer",
    )(x, yconv, attn, w_out, g1, b1, w_ff1, w_ff2, g2, b2)


def kernel(x_prompt, x_sample, cache_k, cache_v, state_conv, w_in, conv_w, lambda_q1, lambda_k1,
           lambda_q2, lambda_k2, subln_g, w_out, ln1_g, ln1_b, w_ff1, w_ff2, ln2_g, ln2_b):
    bp, sp, _ = x_prompt.shape
    bs, ss, _ = x_sample.shape
    depth = w_in.shape[0]
    assert bp == 1 and depth == 1
    past_len = cache_k.shape[2]
    l = 0
    lambda_init = 0.8 - 0.6 * float(np.exp(-0.3 * l))

    w_in_b = w_in[l].astype(_BF16)
    w_out_b = w_out[l].astype(_BF16)
    w_ff1_b = w_ff1[l].astype(_BF16)
    w_ff2_b = w_ff2[l].astype(_BF16)
    lam_vecs = tuple(v[l].reshape(1, QK_DIM) for v in (lambda_q1, lambda_k1, lambda_q2, lambda_k2))
    g_sub = subln_g[l].reshape(1, V_DIM)
    ln = tuple(v[l].reshape(1, D_MODEL) for v in (ln1_g, ln1_b, ln2_g, ln2_b))

    def finish(x2d, yconv, attn):
        return _finish(x2d, yconv, attn, w_out_b, ln[0], ln[1], w_ff1_b, w_ff2_b, ln[2], ln[3])

    xp2 = x_prompt.reshape(sp, D_MODEL)
    yconv_p, k_p, v_p, qa, qb, ka, kb, vt, conv_p = _project_prompt(xp2, w_in_b, conv_w[l])
    attn_p = _attention_prompt(qa, qb, ka, kb, vt, lam_vecs, g_sub, lambda_init)
    y_prompt = finish(xp2, yconv_p, attn_p).reshape(bp, sp, D_MODEL)

    xs2 = x_sample.reshape(bs * ss, D_MODEL)
    st = state_conv[l].astype(_F32)
    pad = lambda a: jnp.pad(a, ((0, 0), (0, ss - a.shape[1]), (0, 0))).reshape(bs * ss, CONV_DIM)
    hist2 = pad(st)
    hist1 = pad(st[:, 1:])
    yconv_s, q_s, k_s, v_s, u_s = _project_sample(xs2, w_in_b, conv_w[l], hist1, hist2, ss)
    ck = cache_k.reshape(depth * bs, past_len * N_HEADS, HEAD_W)
    cv = cache_v.reshape(depth * bs, past_len * N_HEADS, V_DIM)
    attn_s = _attention_sample(q_s, k_s, v_s, ck, cv, lam_vecs, g_sub, lambda_init, ss)
    y_sample = finish(xs2, yconv_s, attn_s).reshape(bs, ss, D_MODEL)

    k_prompt = k_p.reshape(depth, bp, sp, N_HEADS, HEAD_W)
    v_prompt = v_p.reshape(depth, bp, sp, N_HEADS, V_DIM)
    conv_prompt = conv_p.reshape(depth, bp, CONV_W - 1, CONV_DIM)
    k_sample = k_s.reshape(depth, bs, ss, N_HEADS, HEAD_W)
    v_sample = v_s.reshape(depth, bs, ss, N_HEADS, V_DIM)
    conv_sample = u_s.reshape(bs, ss, CONV_DIM)[:, ss - (CONV_W - 1):].reshape(
        depth, bs, CONV_W - 1, CONV_DIM)
    return (y_prompt, y_sample, k_prompt, v_prompt, conv_prompt, k_sample, v_sample, conv_sample)
```

```python
from functools import partial

import jax
import jax.numpy as jnp
import numpy as np
from jax import lax
from jax.experimental import pallas as pl
from jax.experimental.pallas import tpu as pltpu

D_MODEL = 1024
CHUNK = 64
CONV_DIM = 512
ATTN_DIM = 512
QK_DIM = 64
N_HEADS = 4
HEAD_W = 2 * QK_DIM
V_DIM = 128
F32_SUBLANES = 8
BF16_SUBLANES = 16
V_EXT = V_DIM + BF16_SUBLANES
LOG2E = 1.4426950408889634
CONV_W = 3
D_FF = 4096
LN_EPS = 1e-5
DEPTH = 1
ALPHA = (2 * DEPTH) ** 0.25
IN_WIDTH = 3 * CONV_DIM + 3 * ATTN_DIM

V7X_MXU_DIM = 256
V7X_VMEM_BYTES = 64 * 1024 * 1024

ROW_TILE = 1024
FINISH_GROUP = 512
Q_TILE = 2 * V7X_MXU_DIM
K_TILE = 4 * V7X_MXU_DIM
PROJ_TILE = K_TILE
FF_CHUNK = 1024

_F32 = jnp.float32
_BF16 = jnp.bfloat16


def _vmem_limit(nbytes):
    assert nbytes <= V7X_VMEM_BYTES - (4 << 20)
    return int(nbytes)


def _dot(a, b):
    return jnp.dot(a, b, preferred_element_type=_F32)


def _dot_nt(a, b):
    return lax.dot_general(a, b, (((1,), (1,)), ((), ())), preferred_element_type=_F32)


def _dot_tn(a, b):
    return lax.dot_general(a, b, (((0,), (0,)), ((), ())), preferred_element_type=_F32)


def _div_pow2(x, n):
    assert n & (n - 1) == 0
    return lax.shift_right_logical(x, n.bit_length() - 1)


def _mod_pow2(x, n):
    assert n & (n - 1) == 0
    return lax.bitwise_and(x, n - 1)


def _layer_norm(x, g, b):
    mu = jnp.mean(x, axis=-1, keepdims=True)
    xc = x - mu
    var = jnp.mean(xc * xc, axis=-1, keepdims=True)
    return xc * lax.rsqrt(var + LN_EPS) * g + b


def _lambda_value(lq1, lk1, lq2, lk2, lambda_init):
    s1 = jnp.sum(lq1 * lk1, axis=-1, keepdims=True)
    s2 = jnp.sum(lq2 * lk2, axis=-1, keepdims=True)
    return jnp.exp(s1) - jnp.exp(s2) + lambda_init


def _project_columns(x_ref, w_ref):
    xb = x_ref[...].astype(_BF16)
    return [_dot(xb, w_ref[:, g * CONV_DIM:(g + 1) * CONV_DIM]) for g in range(IN_WIDTH // CONV_DIM)]


def _conv_from_taps(u, u1, u2, cw_ref):
    return cw_ref[0:1, :] * u2 + cw_ref[1:2, :] * u1 + cw_ref[2:3, :] * u


def _proj_prompt_kernel(x_ref, w_ref, cw_ref,
                        yconv_ref, kf_ref, vf_ref, qa_ref, qb_ref, ka_ref, kb_ref, vt_ref, cstate_ref,
                        carry_ref):
    i = pl.program_id(0)

    @pl.when(i == 0)
    def _():
        carry_ref[...] = jnp.zeros_like(carry_ref)

    gate, cc, hh, q, k, v = _project_columns(x_ref, w_ref)
    u = cc * hh
    rows = lax.broadcasted_iota(jnp.int32, u.shape, 0)
    prev2 = carry_ref[0:1, :]
    prev1 = carry_ref[1:2, :]
    u1 = jnp.where(rows == 0, prev1, pltpu.roll(u, 1, 0))
    u2 = jnp.where(rows == 0, prev2, jnp.where(rows == 1, prev1, pltpu.roll(u, 2, 0)))
    yconv_ref[...] = (gate * _conv_from_taps(u, u1, u2, cw_ref)).astype(_BF16)
    tail = u[PROJ_TILE - (CONV_W - 1):, :]
    carry_ref[0:CONV_W - 1, :] = tail
    cstate_ref[...] = tail

    for h in range(N_HEADS):
        sl = slice(h * HEAD_W, (h + 1) * HEAD_W)
        kf_ref[pl.ds(h, PROJ_TILE, stride=N_HEADS), :] = k[:, sl]
        vf_ref[pl.ds(h, PROJ_TILE, stride=N_HEADS), :] = v[:, sl]
    qs = q * (QK_DIM ** -0.5 * LOG2E)
    lane = lax.broadcasted_iota(jnp.int32, (PROJ_TILE, HEAD_W), 1)
    first = lane < QK_DIM
    ones_row = jnp.where(lax.broadcasted_iota(jnp.int32, (BF16_SUBLANES, K_TILE), 0) == 0,
                         1.0, 0.0).astype(_BF16)
    key_chunk = _div_pow2(lax.broadcasted_iota(jnp.int32, (PROJ_TILE, HEAD_W), 0), CHUNK)
    chunk_onehot = jnp.where(_mod_pow2(lane, K_TILE // CHUNK) == key_chunk, 1.0, 0.0)
    for h in range(N_HEADS):
        sl = slice(h * HEAD_W, (h + 1) * HEAD_W)
        qh = qs[:, sl]
        kh = k[:, sl]
        qa_ref[h] = jnp.where(first, qh, 0.0).astype(_BF16)
        qb_ref[h] = jnp.where(first, 0.0, qh).astype(_BF16)
        ka_ref[h] = jnp.where(first, kh, chunk_onehot).astype(_BF16)
        kb_ref[h] = jnp.where(first, chunk_onehot, kh).astype(_BF16)
        vt_ref[h, 0, :V_DIM, :] = v[:, sl].T.astype(_BF16)
        vt_ref[h, 0, V_DIM:, :] = ones_row


def _proj_sample_kernel(x_ref, w_ref, cw_ref, h1_ref, h2_ref,
                        yconv_ref, q_ref, k_ref, v_ref, u_ref, *, period):
    gate, cc, hh, q, k, v = _project_columns(x_ref, w_ref)
    u = cc * hh
    t = _mod_pow2(lax.broadcasted_iota(jnp.int32, u.shape, 0), period)
    u1 = jnp.where(t < 1, h1_ref[...], pltpu.roll(u, 1, 0))
    u2 = jnp.where(t < 2, h2_ref[...], pltpu.roll(u, 2, 0))
    yconv_ref[...] = (gate * _conv_from_taps(u, u1, u2, cw_ref)).astype(_BF16)
    q_ref[...] = q * (QK_DIM ** -0.5)
    k_ref[...] = k
    v_ref[...] = v
    u_ref[...] = u


def _project_prompt(x, w_in, conv_w):
    s = x.shape[0]
    assert s % PROJ_TILE == 0 and PROJ_TILE == K_TILE
    n = s // PROJ_TILE
    row = lambda i: (i, 0)
    head_row = lambda i: (0, i, 0)
    out_shape = (
        jax.ShapeDtypeStruct((s, CONV_DIM), _BF16),
        jax.ShapeDtypeStruct((s * N_HEADS, HEAD_W), _F32),
        jax.ShapeDtypeStruct((s * N_HEADS, V_DIM), _F32),
        jax.ShapeDtypeStruct((N_HEADS, s, HEAD_W), _BF16),
        jax.ShapeDtypeStruct((N_HEADS, s, HEAD_W), _BF16),
        jax.ShapeDtypeStruct((N_HEADS, s, HEAD_W), _BF16),
        jax.ShapeDtypeStruct((N_HEADS, s, HEAD_W), _BF16),
        jax.ShapeDtypeStruct((N_HEADS, s // K_TILE, V_EXT, K_TILE), _BF16),
        jax.ShapeDtypeStruct((CONV_W - 1, CONV_DIM), _F32),
    )
    out_specs = (
        pl.BlockSpec((PROJ_TILE, CONV_DIM), row),
        pl.BlockSpec((PROJ_TILE * N_HEADS, HEAD_W), row),
        pl.BlockSpec((PROJ_TILE * N_HEADS, V_DIM), row),
        pl.BlockSpec((N_HEADS, PROJ_TILE, HEAD_W), head_row),
        pl.BlockSpec((N_HEADS, PROJ_TILE, HEAD_W), head_row),
        pl.BlockSpec((N_HEADS, PROJ_TILE, HEAD_W), head_row),
        pl.BlockSpec((N_HEADS, PROJ_TILE, HEAD_W), head_row),
        pl.BlockSpec((N_HEADS, 1, V_EXT, K_TILE), lambda i: (0, i, 0, 0)),
        pl.BlockSpec((CONV_W - 1, CONV_DIM), lambda i: (0, 0)),
    )
    return pl.pallas_call(
        _proj_prompt_kernel,
        grid=(n,),
        in_specs=[
            pl.BlockSpec((PROJ_TILE, D_MODEL), row),
            pl.BlockSpec((D_MODEL, IN_WIDTH), lambda i: (0, 0)),
            pl.BlockSpec((CONV_W, CONV_DIM), lambda i: (0, 0)),
        ],
        out_specs=out_specs,
        out_shape=out_shape,
        scratch_shapes=[pltpu.VMEM((8, CONV_DIM), _F32)],
        compiler_params=pltpu.CompilerParams(
            dimension_semantics=("arbitrary",), vmem_limit_bytes=_vmem_limit(48 << 20)),
        name="project_prompt",
    )(x, w_in, conv_w)


def _project_sample(x, w_in, conv_w, hist1, hist2, period):
    r = x.shape[0]
    full = lambda shape: pl.BlockSpec(shape, lambda i: (0,) * len(shape))
    wide = (r, CONV_DIM)
    return pl.pallas_call(
        partial(_proj_sample_kernel, period=period),
        grid=(1,),
        in_specs=[full((r, D_MODEL)), full((D_MODEL, IN_WIDTH)), full((CONV_W, CONV_DIM)),
                  full(wide), full(wide)],
        out_specs=(full(wide),) * 5,
        out_shape=(jax.ShapeDtypeStruct(wide, _BF16),) + (jax.ShapeDtypeStruct(wide, _F32),) * 4,
        compiler_params=pltpu.CompilerParams(
            dimension_semantics=("arbitrary",), vmem_limit_bytes=_vmem_limit(48 << 20)),
        name="project_sample",
    )(x, w_in, conv_w, hist1, hist2)


def _attn_prompt_kernel(qa_ref, qb_ref, ka_ref, kb_ref, vt_ref, lq1_ref, lk1_ref, lq2_ref, lk2_ref,
                        g_ref, o_ref, s_even, s_odd, cmax_even, cmax_odd, m_ref, acc_ref,
                        *, lambda_init):
    i = pl.program_id(1)
    ratio = K_TILE // Q_TILE
    chunks_per_ktile = K_TILE // CHUNK
    mask_repeat = QK_DIM // chunks_per_ktile
    assert mask_repeat >= 2
    k_refs = (ka_ref, kb_ref)
    bufs = ((s_even, cmax_even), (s_odd, cmax_odd))

    n_full = _div_pow2(i, ratio)
    q_chunk0 = (i - n_full * ratio) * (Q_TILE // CHUNK)

    lane = lax.broadcasted_iota(jnp.int32, (Q_TILE, HEAD_W), 1)
    row_chunk = _div_pow2(lax.broadcasted_iota(jnp.int32, (Q_TILE, HEAD_W), 0), CHUNK) + q_chunk0
    hidden = _mod_pow2(lane, chunks_per_ktile) > row_chunk
    neg = float(jnp.finfo(_BF16).min)
    q_plain = (qa_ref[0], qb_ref[0])
    q_last = (jnp.where(hidden & (lane >= QK_DIM), neg, q_plain[0].astype(_F32)).astype(_BF16),
              jnp.where(hidden & (lane < QK_DIM), neg, q_plain[1].astype(_F32)).astype(_BF16))

    def col_max(st):
        part = jnp.max(st.reshape(K_TILE // F32_SUBLANES, F32_SUBLANES, Q_TILE), axis=0)
        return jnp.max(part, axis=0, keepdims=True)

    def scores(t, buf):
        s_ref, cmax_ref = buf
        rows = pl.ds(pl.multiple_of(t * K_TILE, K_TILE), K_TILE)
        for c in range(2):
            q = jnp.where(t == n_full, q_last[c], q_plain[c])
            st = _dot_nt(k_refs[c][0, rows, :], q)
            s_ref[c] = st
            cmax_ref[c] = col_max(st)

    def consume(t, buf):
        s_ref, cmax_ref = buf
        vt = vt_ref[0, t]
        for c in range(2):
            m = m_ref[c]
            m_new = jnp.maximum(m, cmax_ref[c])
            alpha = jnp.exp2(m - m_new)
            p = jnp.exp2(s_ref[c] - m_new).astype(_BF16)
            acc_ref[c] = alpha * acc_ref[c] + _dot(vt, p)
            m_ref[c] = m_new

    m_ref[...] = jnp.full(m_ref.shape, -jnp.inf, _F32)
    acc_ref[...] = jnp.zeros(acc_ref.shape, _F32)
    scores(0, bufs[0])

    def pair(p, carry):
        t = 2 * p
        scores(t + 1, bufs[1])
        consume(t, bufs[0])
        scores(t + 2, bufs[0])
        consume(t + 1, bufs[1])
        return carry

    n_pairs = _div_pow2(n_full, 2)
    lax.fori_loop(0, n_pairs, pair, 0)

    t0 = 2 * n_pairs
    two_left = n_full > t0

    @pl.when(two_left)
    def _():
        scores(n_full, bufs[1])
        consume(t0, bufs[0])
        consume(n_full, bufs[1])

    @pl.when(jnp.logical_not(two_left))
    def _():
        consume(t0, bufs[0])

    lam =_lambda_value(lq1_ref[...], lk1_ref[...], lq2_ref[...], lk2_ref[...], lambda_init)
    acc0 = acc_ref[0]
    acc1 = acc_ref[1]
    l0 = acc0[V_DIM:V_DIM + 1, :]
    l1 = acc1[V_DIM:V_DIM + 1, :]
    o = acc0[:V_DIM] / l0 - lam * (acc1[:V_DIM] / l1)
    ms = jnp.mean(o * o, axis=0, keepdims=True)
    on = o * lax.rsqrt(ms + LN_EPS)
    on = on.T * g_ref[...] * (1.0 - lambda_init)
    o_ref[...] = on.astype(_BF16)


def _attention_prompt(qa, qb, ka, kb, vt, lam_vecs, subln_g, lambda_init):
    s = qa.shape[1]
    assert K_TILE % Q_TILE == 0 and s % K_TILE == 0 and Q_TILE % CHUNK == 0
    nq = s // Q_TILE
    vec = pl.BlockSpec((1, QK_DIM), lambda h, i: (0, 0))
    q_tile = pl.BlockSpec((1, Q_TILE, HEAD_W), lambda h, i: (h, i, 0))
    return pl.pallas_call(
        partial(_attn_prompt_kernel, lambda_init=lambda_init),
        grid=(N_HEADS, nq),
        in_specs=[
            q_tile, q_tile,
            pl.BlockSpec((1, s, HEAD_W), lambda h, i: (h, 0, 0)),
            pl.BlockSpec((1, s, HEAD_W), lambda h, i: (h, 0, 0)),
            pl.BlockSpec((1, s // K_TILE, V_EXT, K_TILE), lambda h, i: (h, 0, 0, 0)),
            vec, vec, vec, vec,
            pl.BlockSpec((1, V_DIM), lambda h, i: (0, 0)),
        ],
        out_specs=pl.BlockSpec((Q_TILE, V_DIM), lambda h, i: (i, h)),
        out_shape=jax.ShapeDtypeStruct((s, ATTN_DIM), _BF16),
        scratch_shapes=[
            pltpu.VMEM((2, K_TILE, Q_TILE), _F32), pltpu.VMEM((2, K_TILE, Q_TILE), _F32),
            pltpu.VMEM((2, 1, Q_TILE), _F32), pltpu.VMEM((2, 1, Q_TILE), _F32),
            pltpu.VMEM((2, 1, Q_TILE), _F32), pltpu.VMEM((2, V_EXT, Q_TILE), _F32),
        ],
        compiler_params=pltpu.CompilerParams(
            dimension_semantics=("arbitrary", "arbitrary"), vmem_limit_bytes=_vmem_limit(48 << 20)),
        name="attention_prompt",
    )(qa, qb, ka, kb, vt, *lam_vecs, subln_g)


def _attn_sample_kernel(q_ref, kn_ref, vn_ref, ck_ref, cv_ref, lq1_ref, lk1_ref, lq2_ref, lk2_ref,
                        g_ref, o_ref, *, lambda_init, past_len, n_new):
    group = 2 * N_HEADS
    width = group * n_new
    assert width == HEAD_W
    qb = q_ref[...].astype(_BF16)
    sel_r = lax.broadcasted_iota(jnp.int32, (n_new, width), 0)
    sel_c = lax.broadcasted_iota(jnp.int32, (n_new, width), 1)
    spread = jnp.where(_mod_pow2(sel_c, n_new) == sel_r, 1.0, 0.0).astype(_BF16)
    q_all = _dot_tn(qb, spread)
    blk_r = _div_pow2(lax.broadcasted_iota(jnp.int32, (ATTN_DIM, width), 0), QK_DIM)
    blk_c = _div_pow2(lax.broadcasted_iota(jnp.int32, (ATTN_DIM, width), 1), n_new)
    q_bd = jnp.where(blk_r == blk_c, q_all, 0.0).astype(_BF16)

    def heads_on_lanes(ref):
        return jnp.concatenate(
            [ref[0, pl.ds(h, past_len, stride=N_HEADS), :].astype(_BF16) for h in range(N_HEADS)],
            axis=-1)

    s_old = _dot(heads_on_lanes(ck_ref), q_bd)
    s_new = _dot(kn_ref[...].astype(_BF16), q_bd)

    def chunk_mask(shape, k_off):
        k_pos = k_off + lax.broadcasted_iota(jnp.int32, shape, 0)
        q_pos = past_len + _mod_pow2(lax.broadcasted_iota(jnp.int32, shape, 1), n_new)
        return _div_pow2(k_pos, CHUNK) <= _div_pow2(q_pos, CHUNK)

    s_old = jnp.where(chunk_mask(s_old.shape, 0), s_old, -jnp.inf)
    s_new = jnp.where(chunk_mask(s_new.shape, past_len), s_new, -jnp.inf)
    m = jnp.maximum(jnp.max(s_old, axis=0, keepdims=True), jnp.max(s_new, axis=0, keepdims=True))
    p_old = jnp.exp(s_old - m)
    p_new = jnp.exp(s_new - m)
    inv_l = 1.0 / (jnp.sum(p_old, axis=0, keepdims=True) + jnp.sum(p_new, axis=0, keepdims=True))
    a_old = (p_old * inv_l).astype(_BF16)
    a_new = (p_new * inv_l).astype(_BF16)
    pv = _dot_tn(a_old, heads_on_lanes(cv_ref)) + _dot_tn(a_new, vn_ref[...].astype(_BF16))

    lam = _lambda_value(lq1_ref[...], lk1_ref[...], lq2_ref[...], lk2_ref[...], lambda_init)
    g = g_ref[...]
    for h in range(N_HEADS):
        cols = slice(h * V_DIM, (h + 1) * V_DIM)
        r0 = h * 2 * n_new
        o = pv[r0:r0 + n_new, cols] - lam * pv[r0 + n_new:r0 + 2 * n_new, cols]
        ms = jnp.mean(o * o, axis=-1, keepdims=True)
        o_ref[:, cols] = (o * lax.rsqrt(ms + LN_EPS) * g * (1.0 - lambda_init)).astype(_BF16)


def _attention_sample(q, k_new, v_new, cache_k, cache_v, lam_vecs, subln_g, lambda_init, n_new):
    nb, rows_per_stream = cache_k.shape[:2]
    past_len = rows_per_stream // N_HEADS
    cache_block = pl.BlockSpec((1, rows_per_stream, HEAD_W), lambda b: (b, 0, 0))
    rows = lambda b: (b, 0)
    vec = pl.BlockSpec((1, QK_DIM), lambda b: (0, 0))
    return pl.pallas_call(
        partial(_attn_sample_kernel, lambda_init=lambda_init, past_len=past_len, n_new=n_new),
        grid=(nb,),
        in_specs=[
            pl.BlockSpec((n_new, ATTN_DIM), rows),
            pl.BlockSpec((n_new, ATTN_DIM), rows),
            pl.BlockSpec((n_new, ATTN_DIM), rows),
            cache_block, cache_block,
            vec, vec, vec, vec,
            pl.BlockSpec((1, V_DIM), lambda b: (0, 0)),
        ],
        out_specs=pl.BlockSpec((n_new, ATTN_DIM), rows),
        out_shape=jax.ShapeDtypeStruct((nb * n_new, ATTN_DIM), _BF16),
        compiler_params=pltpu.CompilerParams(
            dimension_semantics=("arbitrary",), vmem_limit_bytes=_vmem_limit(40 << 20)),
        name="attention_sample",
    )(q, k_new, v_new, cache_k, cache_v, *lam_vecs, subln_g)


def _finish_kernel(x_ref, yconv_ref, attn_ref, wo_ref, g1_ref, b1_ref, w1_ref, w2_ref, g2_ref, b2_ref,
                   y_ref):
    tile = x_ref.shape[0]
    group = min(tile, FINISH_GROUP)
    groups = [slice(r0, r0 + group) for r0 in range(0, tile, group)]
    pre = [ALPHA * x_ref[rows, :]
           + _dot(jnp.concatenate([yconv_ref[rows, :], attn_ref[rows, :]], axis=-1), wo_ref[...])
           for rows in groups]
    for rows, z in zip(groups, pre):
        x1 = _layer_norm(z, g1_ref[...], b1_ref[...])
        x1b = x1.astype(_BF16)
        ff = jnp.zeros_like(x1)
        for c in range(D_FF // FF_CHUNK):
            cols = slice(c * FF_CHUNK, (c + 1) * FF_CHUNK)
            hdn = jnp.square(jnp.maximum(_dot(x1b, w1_ref[:, cols]), 0.0))
            ff = ff + _dot(hdn.astype(_BF16), w2_ref[cols, :])
        y_ref[rows, :] = _layer_norm(ALPHA * x1 + ff, g2_ref[...], b2_ref[...])


def _finish(x, yconv, attn, w_out, g1, b1, w_ff1, w_ff2, g2, b2):
    r = x.shape[0]
    tile = min(ROW_TILE, r)
    assert r % tile == 0
    row = lambda i: (i, 0)
    const = lambda shape: pl.BlockSpec(shape, lambda i: (0, 0))
    return pl.pallas_call(
        _finish_kernel,
        grid=(r // tile,),
        in_specs=[
            pl.BlockSpec((tile, D_MODEL), row),
            pl.BlockSpec((tile, CONV_DIM), row),
            pl.BlockSpec((tile, ATTN_DIM), row),
            const((D_MODEL, D_MODEL)), const((1, D_MODEL)), const((1, D_MODEL)),
            const((D_MODEL, D_FF)), const((D_FF, D_MODEL)), const((1, D_MODEL)), const((1, D_MODEL)),
        ],
        out_specs=pl.BlockSpec((tile, D_MODEL), row),
        out_shape=jax.ShapeDtypeStruct((r, D_MODEL), _F32),
        compiler_params=pltpu.CompilerParams(
            dimension_semantics=("arbitrary",), vmem_limit_bytes=_vmem_limit(56 << 20)),
        name="finish_layer",
    )(x, yconv, attn, w_out, g1, b1, w_ff1, w_ff2, g2, b2)


def kernel(x_prompt, x_sample, cache_k, cache_v, state_conv, w_in, conv_w, lambda_q1, lambda_k1,
           lambda_q2, lambda_k2, subln_g, w_out, ln1_g, ln1_b, w_ff1, w_ff2, ln2_g, ln2_b):
    bp, sp, _ = x_prompt.shape
    bs, ss, _ = x_sample.shape
    depth = w_in.shape[0]
    assert bp == 1 and depth == 1
    past_len = cache_k.shape[2]
    l = 0
    lambda_init = 0.8 - 0.6 * float(np.exp(-0.3 * l))

    w_in_b = w_in[l].astype(_BF16)
    w_out_b = w_out[l].astype(_BF16)
    w_ff1_b = w_ff1[l].astype(_BF16)
    w_ff2_b = w_ff2[l].astype(_BF16)
    lam_vecs = tuple(v[l].reshape(1, QK_DIM) for v in (lambda_q1, lambda_k1, lambda_q2, lambda_k2))
    g_sub = subln_g[l].reshape(1, V_DIM)
    ln = tuple(v[l].reshape(1, D_MODEL) for v in (ln1_g, ln1_b, ln2_g, ln2_b))

    def finish(x2d, yconv, attn):
        return _finish(x2d, yconv, attn, w_out_b, ln[0], ln[1], w_ff1_b, w_ff2_b, ln[2], ln[3])

    xp2 = x_prompt.reshape(sp, D_MODEL)
    yconv_p, k_p, v_p, qa, qb, ka, kb, vt, conv_p = _project_prompt(xp2, w_in_b, conv_w[l])
    attn_p = _attention_prompt(qa, qb, ka, kb, vt, lam_vecs, g_sub, lambda_init)
    y_prompt = finish(xp2, yconv_p, attn_p).reshape(bp, sp, D_MODEL)

    xs2 = x_sample.reshape(bs * ss, D_MODEL)
    st = state_conv[l].astype(_F32)
    pad = lambda a: jnp.pad(a, ((0, 0), (0, ss - a.shape[1]), (0, 0))).reshape(bs * ss, CONV_DIM)
    hist2 = pad(st)
    hist1 = pad(st[:, 1:])
    yconv_s, q_s, k_s, v_s, u_s = _project_sample(xs2, w_in_b, conv_w[l], hist1, hist2, ss)
    ck = cache_k.reshape(depth * bs, past_len * N_HEADS, HEAD_W)
    cv = cache_v.reshape(depth * bs, past_len * N_HEADS, V_DIM)
    attn_s = _attention_sample(q_s, k_s, v_s, ck, cv, lam_vecs, g_sub, lambda_init, ss)
    y_sample = finish(xs2, yconv_s, attn_s).reshape(bs, ss, D_MODEL)

    k_prompt = k_p.reshape(depth, bp, sp, N_HEADS, HEAD_W)
    v_prompt = v_p.reshape(depth, bp, sp, N_HEADS, V_DIM)
    conv_prompt = conv_p.reshape(depth, bp, CONV_W - 1, CONV_DIM)
    k_sample = k_s.reshape(depth, bs, ss, N_HEADS, HEAD_W)
    v_sample = v_s.reshape(depth, bs, ss, N_HEADS, V_DIM)
    conv_sample = u_s.reshape(bs, ss, CONV_DIM)[:, ss - (CONV_W - 1):].reshape(
        depth, bs, CONV_W - 1, CONV_DIM)
    return (y_prompt, y_sample, k_prompt, v_prompt, conv_prompt, k_sample, v_sample, conv_sample)
```

```python
from functools import partial

import jax
import jax.numpy as jnp
import numpy as np
from jax import lax
from jax.experimental import pallas as pl
from jax.experimental.pallas import tpu as pltpu

D_MODEL = 1024
CHUNK = 64
CONV_DIM = 512
ATTN_DIM = 512
QK_DIM = 64
N_HEADS = 4
HEAD_W = 2 * QK_DIM
V_DIM = 128
F32_SUBLANES = 8
BF16_SUBLANES = 16
V_EXT = V_DIM + BF16_SUBLANES
LOG2E = 1.4426950408889634
CONV_W = 3
D_FF = 4096
LN_EPS = 1e-5
DEPTH = 1
ALPHA = (2 * DEPTH) ** 0.25
IN_WIDTH = 3 * CONV_DIM + 3 * ATTN_DIM

V7X_MXU_DIM = 256
V7X_VMEM_BYTES = 64 * 1024 * 1024

ROW_TILE = 1024
FINISH_GROUP = 512
Q_TILE = 2 * V7X_MXU_DIM
K_TILE = 2 * V7X_MXU_DIM
PROJ_TILE = 1024
PAIRS_PER_TRIP = 2
FF_CHUNK = 1024

_F32 = jnp.float32
_BF16 = jnp.bfloat16


def _vmem_limit(nbytes):
    assert nbytes <= V7X_VMEM_BYTES - (4 << 20)
    return int(nbytes)


def _dot(a, b):
    return jnp.dot(a, b, preferred_element_type=_F32)


def _dot_nt(a, b):
    return lax.dot_general(a, b, (((1,), (1,)), ((), ())), preferred_element_type=_F32)


def _dot_tn(a, b):
    return lax.dot_general(a, b, (((0,), (0,)), ((), ())), preferred_element_type=_F32)


def _div_pow2(x, n):
    assert n & (n - 1) == 0
    return lax.shift_right_logical(x, n.bit_length() - 1)


def _mod_pow2(x, n):
    assert n & (n - 1) == 0
    return lax.bitwise_and(x, n - 1)


def _layer_norm(x, g, b):
    mu = jnp.mean(x, axis=-1, keepdims=True)
    xc = x - mu
    var = jnp.mean(xc * xc, axis=-1, keepdims=True)
    return xc * lax.rsqrt(var + LN_EPS) * g + b


def _lambda_value(lq1, lk1, lq2, lk2, lambda_init):
    s1 = jnp.sum(lq1 * lk1, axis=-1, keepdims=True)
    s2 = jnp.sum(lq2 * lk2, axis=-1, keepdims=True)
    return jnp.exp(s1) - jnp.exp(s2) + lambda_init


def _project_columns(x_ref, w_ref):
    xb = x_ref[...].astype(_BF16)
    return [_dot(xb, w_ref[:, g * CONV_DIM:(g + 1) * CONV_DIM]) for g in range(IN_WIDTH // CONV_DIM)]


def _conv_from_taps(u, u1, u2, cw_ref):
    return cw_ref[0:1, :] * u2 + cw_ref[1:2, :] * u1 + cw_ref[2:3, :] * u


def _proj_prompt_kernel(x_ref, w_ref, cw_ref,
                        yconv_ref, kf_ref, vf_ref, qa_ref, qb_ref, ka_ref, kb_ref, vt_ref, cstate_ref,
                        carry_ref):
    i = pl.program_id(0)

    @pl.when(i == 0)
    def _():
        carry_ref[...] = jnp.zeros_like(carry_ref)

    gate, cc, hh, q, k, v = _project_columns(x_ref, w_ref)
    u = cc * hh
    rows = lax.broadcasted_iota(jnp.int32, u.shape, 0)
    prev2 = carry_ref[0:1, :]
    prev1 = carry_ref[1:2, :]
    u1 = jnp.where(rows == 0, prev1, pltpu.roll(u, 1, 0))
    u2 = jnp.where(rows == 0, prev2, jnp.where(rows == 1, prev1, pltpu.roll(u, 2, 0)))
    yconv_ref[...] = (gate * _conv_from_taps(u, u1, u2, cw_ref)).astype(_BF16)
    tail = u[PROJ_TILE - (CONV_W - 1):, :]
    carry_ref[0:CONV_W - 1, :] = tail
    cstate_ref[...] = tail

    for h in range(N_HEADS):
        sl = slice(h * HEAD_W, (h + 1) * HEAD_W)
        kf_ref[pl.ds(h, PROJ_TILE, stride=N_HEADS), :] = k[:, sl]
        vf_ref[pl.ds(h, PROJ_TILE, stride=N_HEADS), :] = v[:, sl]
    qs = q * (QK_DIM ** -0.5 * LOG2E)
    lane = lax.broadcasted_iota(jnp.int32, (PROJ_TILE, HEAD_W), 1)
    first = lane < QK_DIM
    ones_row = jnp.where(lax.broadcasted_iota(jnp.int32, (BF16_SUBLANES, K_TILE), 0) == 0,
                         1.0, 0.0).astype(_BF16)
    row_in_ktile = _mod_pow2(lax.broadcasted_iota(jnp.int32, (PROJ_TILE, HEAD_W), 0), K_TILE)
    key_chunk = _div_pow2(row_in_ktile, CHUNK)
    chunk_onehot = jnp.where(_mod_pow2(lane, K_TILE // CHUNK) == key_chunk, 1.0, 0.0)
    for h in range(N_HEADS):
        sl = slice(h * HEAD_W, (h + 1) * HEAD_W)
        qh = qs[:, sl]
        kh = k[:, sl]
        qa_ref[h] = jnp.where(first, qh, 0.0).astype(_BF16)
        qb_ref[h] = jnp.where(first, 0.0, qh).astype(_BF16)
        ka_ref[h] = jnp.where(first, kh, chunk_onehot).astype(_BF16)
        kb_ref[h] = jnp.where(first, chunk_onehot, kh).astype(_BF16)
        for s in range(PROJ_TILE // K_TILE):
            vt_ref[h, s, :V_DIM, :] = v[s * K_TILE:(s + 1) * K_TILE, sl].T.astype(_BF16)
            vt_ref[h, s, V_DIM:, :] = ones_row


def _proj_sample_kernel(x_ref, w_ref, cw_ref, h1_ref, h2_ref,
                        yconv_ref, q_ref, k_ref, v_ref, u_ref, *, period):
    gate, cc, hh, q, k, v = _project_columns(x_ref, w_ref)
    u = cc * hh
    t = _mod_pow2(lax.broadcasted_iota(jnp.int32, u.shape, 0), period)
    u1 = jnp.where(t < 1, h1_ref[...], pltpu.roll(u, 1, 0))
    u2 = jnp.where(t < 2, h2_ref[...], pltpu.roll(u, 2, 0))
    yconv_ref[...] = (gate * _conv_from_taps(u, u1, u2, cw_ref)).astype(_BF16)
    q_ref[...] = q * (QK_DIM ** -0.5)
    k_ref[...] = k
    v_ref[...] = v
    u_ref[...] = u


def _project_prompt(x, w_in, conv_w):
    s = x.shape[0]
    assert s % PROJ_TILE == 0 and PROJ_TILE % K_TILE == 0
    n = s // PROJ_TILE
    row = lambda i: (i, 0)
    head_row = lambda i: (0, i, 0)
    out_shape = (
        jax.ShapeDtypeStruct((s, CONV_DIM), _BF16),
        jax.ShapeDtypeStruct((s * N_HEADS, HEAD_W), _F32),
        jax.ShapeDtypeStruct((s * N_HEADS, V_DIM), _F32),
        jax.ShapeDtypeStruct((N_HEADS, s, HEAD_W), _BF16),
        jax.ShapeDtypeStruct((N_HEADS, s, HEAD_W), _BF16),
        jax.ShapeDtypeStruct((N_HEADS, s, HEAD_W), _BF16),
        jax.ShapeDtypeStruct((N_HEADS, s, HEAD_W), _BF16),
        jax.ShapeDtypeStruct((N_HEADS, s // K_TILE, V_EXT, K_TILE), _BF16),
        jax.ShapeDtypeStruct((CONV_W - 1, CONV_DIM), _F32),
    )
    out_specs = (
        pl.BlockSpec((PROJ_TILE, CONV_DIM), row),
        pl.BlockSpec((PROJ_TILE * N_HEADS, HEAD_W), row),
        pl.BlockSpec((PROJ_TILE * N_HEADS, V_DIM), row),
        pl.BlockSpec((N_HEADS, PROJ_TILE, HEAD_W), head_row),
        pl.BlockSpec((N_HEADS, PROJ_TILE, HEAD_W), head_row),
        pl.BlockSpec((N_HEADS, PROJ_TILE, HEAD_W), head_row),
        pl.BlockSpec((N_HEADS, PROJ_TILE, HEAD_W), head_row),
        pl.BlockSpec((N_HEADS, PROJ_TILE // K_TILE, V_EXT, K_TILE), lambda i: (0, i, 0, 0)),
        pl.BlockSpec((CONV_W - 1, CONV_DIM), lambda i: (0, 0)),
    )
    return pl.pallas_call(
        _proj_prompt_kernel,
        grid=(n,),
        in_specs=[
            pl.BlockSpec((PROJ_TILE, D_MODEL), row),
            pl.BlockSpec((D_MODEL, IN_WIDTH), lambda i: (0, 0)),
            pl.BlockSpec((CONV_W, CONV_DIM), lambda i: (0, 0)),
        ],
        out_specs=out_specs,
        out_shape=out_shape,
        scratch_shapes=[pltpu.VMEM((8, CONV_DIM), _F32)],
        compiler_params=pltpu.CompilerParams(
            dimension_semantics=("arbitrary",), vmem_limit_bytes=_vmem_limit(48 << 20)),
        name="project_prompt",
    )(x, w_in, conv_w)


def _project_sample(x, w_in, conv_w, hist1, hist2, period):
    r = x.shape[0]
    full = lambda shape: pl.BlockSpec(shape, lambda i: (0,) * len(shape))
    wide = (r, CONV_DIM)
    return pl.pallas_call(
        partial(_proj_sample_kernel, period=period),
        grid=(1,),
        in_specs=[full((r, D_MODEL)), full((D_MODEL, IN_WIDTH)), full((CONV_W, CONV_DIM)),
                  full(wide), full(wide)],
        out_specs=(full(wide),) * 5,
        out_shape=(jax.ShapeDtypeStruct(wide, _BF16),) + (jax.ShapeDtypeStruct(wide, _F32),) * 4,
        compiler_params=pltpu.CompilerParams(
            dimension_semantics=("arbitrary",), vmem_limit_bytes=_vmem_limit(48 << 20)),
        name="project_sample",
    )(x, w_in, conv_w, hist1, hist2)


def _attn_prompt_kernel(qa_ref, qb_ref, ka_ref, kb_ref, vt_ref, lq1_ref, lk1_ref, lq2_ref, lk2_ref,
                        g_ref, o_ref, s_even, s_odd, cmax_even, cmax_odd, m_ref, acc_ref,
                        *, lambda_init):
    i = pl.program_id(1)
    ratio = K_TILE // Q_TILE
    chunks_per_ktile = K_TILE // CHUNK
    mask_repeat = QK_DIM // chunks_per_ktile
    assert mask_repeat >= 2
    k_refs = (ka_ref, kb_ref)
    bufs = ((s_even, cmax_even), (s_odd, cmax_odd))

    n_full = _div_pow2(i, ratio)
    q_chunk0 = (i - n_full * ratio) * (Q_TILE // CHUNK)

    lane = lax.broadcasted_iota(jnp.int32, (Q_TILE, HEAD_W), 1)
    row_chunk = _div_pow2(lax.broadcasted_iota(jnp.int32, (Q_TILE, HEAD_W), 0), CHUNK) + q_chunk0
    hidden = _mod_pow2(lane, chunks_per_ktile) > row_chunk
    neg = float(jnp.finfo(_BF16).min)
    q_plain = (qa_ref[0], qb_ref[0])
    q_last = (jnp.where(hidden & (lane >= QK_DIM), neg, q_plain[0].astype(_F32)).astype(_BF16),
              jnp.where(hidden & (lane < QK_DIM), neg, q_plain[1].astype(_F32)).astype(_BF16))

    def col_max(st):
        part = jnp.max(st.reshape(K_TILE // F32_SUBLANES, F32_SUBLANES, Q_TILE), axis=0)
        return jnp.max(part, axis=0, keepdims=True)

    def scores(t, buf):
        s_ref, cmax_ref = buf
        rows = pl.ds(pl.multiple_of(t * K_TILE, K_TILE), K_TILE)
        for c in range(2):
            q = jnp.where(t == n_full, q_last[c], q_plain[c])
            st = _dot_nt(k_refs[c][0, rows, :], q)
            s_ref[c] = st
            cmax_ref[c] = col_max(st)

    def consume(t, buf):
        s_ref, cmax_ref = buf
        vt = vt_ref[0, t]
        for c in range(2):
            m = m_ref[c]
            m_new = jnp.maximum(m, cmax_ref[c])
            alpha = jnp.exp2(m - m_new)
            p = jnp.exp2(s_ref[c] - m_new).astype(_BF16)
            acc_ref[c] = alpha * acc_ref[c] + _dot(vt, p)
            m_ref[c] = m_new

    m_ref[...] = jnp.full(m_ref.shape, -jnp.inf, _F32)
    acc_ref[...] = jnp.zeros(acc_ref.shape, _F32)
    scores(0, bufs[0])

    def pairs(t, count):
        for k in range(count):
            scores(t + 2 * k + 1, bufs[1])
            consume(t + 2 * k, bufs[0])
            scores(t + 2 * k + 2, bufs[0])
            consume(t + 2 * k + 1, bufs[1])

    def long_trip(p, carry):
        pairs(2 * PAIRS_PER_TRIP * p, PAIRS_PER_TRIP)
        return carry

    n_pairs = _div_pow2(n_full, 2)
    n_long = _div_pow2(n_pairs, PAIRS_PER_TRIP)
    lax.fori_loop(0, n_long, long_trip, 0)

    def short_trip(p, carry):
        pairs(2 * p, 1)
        return carry

    lax.fori_loop(n_long * PAIRS_PER_TRIP, n_pairs, short_trip, 0)

    t0 = 2 * n_pairs
    two_left = n_full > t0

    @pl.when(two_left)
    def _():
        scores(n_full, bufs[1])
        consume(t0, bufs[0])
        consume(n_full, bufs[1])

    @pl.when(jnp.logical_not(two_left))
    def _():
        consume(t0, bufs[0])

    lam =_lambda_value(lq1_ref[...], lk1_ref[...], lq2_ref[...], lk2_ref[...], lambda_init)
    acc0 = acc_ref[0]
    acc1 = acc_ref[1]
    l0 = acc0[V_DIM:V_DIM + 1, :]
    l1 = acc1[V_DIM:V_DIM + 1, :]
    o = acc0[:V_DIM] / l0 - lam * (acc1[:V_DIM] / l1)
    ms = jnp.mean(o * o, axis=0, keepdims=True)
    on = o * lax.rsqrt(ms + LN_EPS)
    on = on.T * g_ref[...] * (1.0 - lambda_init)
    o_ref[...] = on.astype(_BF16)


def _attention_prompt(qa, qb, ka, kb, vt, lam_vecs, subln_g, lambda_init):
    s = qa.shape[1]
    assert K_TILE % Q_TILE == 0 and s % K_TILE == 0 and Q_TILE % CHUNK == 0
    nq = s // Q_TILE
    vec = pl.BlockSpec((1, QK_DIM), lambda h, i: (0, 0))
    q_tile = pl.BlockSpec((1, Q_TILE, HEAD_W), lambda h, i: (h, i, 0))
    return pl.pallas_call(
        partial(_attn_prompt_kernel, lambda_init=lambda_init),
        grid=(N_HEADS, nq),
        in_specs=[
            q_tile, q_tile,
            pl.BlockSpec((1, s, HEAD_W), lambda h, i: (h, 0, 0)),
            pl.BlockSpec((1, s, HEAD_W), lambda h, i: (h, 0, 0)),
            pl.BlockSpec((1, s // K_TILE, V_EXT, K_TILE), lambda h, i: (h, 0, 0, 0)),
            vec, vec, vec, vec,
            pl.BlockSpec((1, V_DIM), lambda h, i: (0, 0)),
        ],
        out_specs=pl.BlockSpec((Q_TILE, V_DIM), lambda h, i: (i, h)),
        out_shape=jax.ShapeDtypeStruct((s, ATTN_DIM), _BF16),
        scratch_shapes=[
            pltpu.VMEM((2, K_TILE, Q_TILE), _F32), pltpu.VMEM((2, K_TILE, Q_TILE), _F32),
            pltpu.VMEM((2, 1, Q_TILE), _F32), pltpu.VMEM((2, 1, Q_TILE), _F32),
            pltpu.VMEM((2, 1, Q_TILE), _F32), pltpu.VMEM((2, V_EXT, Q_TILE), _F32),
        ],
        compiler_params=pltpu.CompilerParams(
            dimension_semantics=("arbitrary", "arbitrary"), vmem_limit_bytes=_vmem_limit(48 << 20)),
        name="attention_prompt",
    )(qa, qb, ka, kb, vt, *lam_vecs, subln_g)


def _attn_sample_kernel(q_ref, kn_ref, vn_ref, ck_ref, cv_ref, lq1_ref, lk1_ref, lq2_ref, lk2_ref,
                        g_ref, o_ref, *, lambda_init, past_len, n_new):
    group = 2 * N_HEADS
    width = group * n_new
    assert width == HEAD_W
    qb = q_ref[...].astype(_BF16)
    sel_r = lax.broadcasted_iota(jnp.int32, (n_new, width), 0)
    sel_c = lax.broadcasted_iota(jnp.int32, (n_new, width), 1)
    spread = jnp.where(_mod_pow2(sel_c, n_new) == sel_r, 1.0, 0.0).astype(_BF16)
    q_all = _dot_tn(qb, spread)
    blk_r = _div_pow2(lax.broadcasted_iota(jnp.int32, (ATTN_DIM, width), 0), QK_DIM)
    blk_c = _div_pow2(lax.broadcasted_iota(jnp.int32, (ATTN_DIM, width), 1), n_new)
    q_bd = jnp.where(blk_r == blk_c, q_all, 0.0).astype(_BF16)

    def heads_on_lanes(ref):
        return jnp.concatenate(
            [ref[0, pl.ds(h, past_len, stride=N_HEADS), :].astype(_BF16) for h in range(N_HEADS)],
            axis=-1)

    s_old = _dot(heads_on_lanes(ck_ref), q_bd)
    s_new = _dot(kn_ref[...].astype(_BF16), q_bd)

    def chunk_mask(shape, k_off):
        k_pos = k_off + lax.broadcasted_iota(jnp.int32, shape, 0)
        q_pos = past_len + _mod_pow2(lax.broadcasted_iota(jnp.int32, shape, 1), n_new)
        return _div_pow2(k_pos, CHUNK) <= _div_pow2(q_pos, CHUNK)

    s_old = jnp.where(chunk_mask(s_old.shape, 0), s_old, -jnp.inf)
    s_new = jnp.where(chunk_mask(s_new.shape, past_len), s_new, -jnp.inf)
    m = jnp.maximum(jnp.max(s_old, axis=0, keepdims=True), jnp.max(s_new, axis=0, keepdims=True))
    p_old = jnp.exp(s_old - m)
    p_new = jnp.exp(s_new - m)
    inv_l = 1.0 / (jnp.sum(p_old, axis=0, keepdims=True) + jnp.sum(p_new, axis=0, keepdims=True))
    a_old = (p_old * inv_l).astype(_BF16)
    a_new = (p_new * inv_l).astype(_BF16)
    pv = _dot_tn(a_old, heads_on_lanes(cv_ref)) + _dot_tn(a_new, vn_ref[...].astype(_BF16))

    lam = _lambda_value(lq1_ref[...], lk1_ref[...], lq2_ref[...], lk2_ref[...], lambda_init)
    g = g_ref[...]
    for h in range(N_HEADS):
        cols = slice(h * V_DIM, (h + 1) * V_DIM)
        r0 = h * 2 * n_new
        o = pv[r0:r0 + n_new, cols] - lam * pv[r0 + n_new:r0 + 2 * n_new, cols]
        ms = jnp.mean(o * o, axis=-1, keepdims=True)
        o_ref[:, cols] = (o * lax.rsqrt(ms + LN_EPS) * g * (1.0 - lambda_init)).astype(_BF16)


def _attention_sample(q, k_new, v_new, cache_k, cache_v, lam_vecs, subln_g, lambda_init, n_new):
    nb, rows_per_stream = cache_k.shape[:2]
    past_len = rows_per_stream // N_HEADS
    cache_block = pl.BlockSpec((1, rows_per_stream, HEAD_W), lambda b: (b, 0, 0))
    rows = lambda b: (b, 0)
    vec = pl.BlockSpec((1, QK_DIM), lambda b: (0, 0))
    return pl.pallas_call(
        partial(_attn_sample_kernel, lambda_init=lambda_init, past_len=past_len, n_new=n_new),
        grid=(nb,),
        in_specs=[
            pl.BlockSpec((n_new, ATTN_DIM), rows),
            pl.BlockSpec((n_new, ATTN_DIM), rows),
            pl.BlockSpec((n_new, ATTN_DIM), rows),
            cache_block, cache_block,
            vec, vec, vec, vec,
            pl.BlockSpec((1, V_DIM), lambda b: (0, 0)),
        ],
        out_specs=pl.BlockSpec((n_new, ATTN_DIM), rows),
        out_shape=jax.ShapeDtypeStruct((nb * n_new, ATTN_DIM), _BF16),
        compiler_params=pltpu.CompilerParams(
            dimension_semantics=("arbitrary",), vmem_limit_bytes=_vmem_limit(40 << 20)),
        name="attention_sample",
    )(q, k_new, v_new, cache_k, cache_v, *lam_vecs, subln_g)


def _finish_kernel(x_ref, yconv_ref, attn_ref, wo_ref, g1_ref, b1_ref, w1_ref, w2_ref, g2_ref, b2_ref,
                   y_ref):
    tile = x_ref.shape[0]
    group = min(tile, FINISH_GROUP)
    groups = [slice(r0, r0 + group) for r0 in range(0, tile, group)]
    pre = [ALPHA * x_ref[rows, :]
           + _dot(jnp.concatenate([yconv_ref[rows, :], attn_ref[rows, :]], axis=-1), wo_ref[...])
           for rows in groups]
    for rows, z in zip(groups, pre):
        x1 = _layer_norm(z, g1_ref[...], b1_ref[...])
        x1b = x1.astype(_BF16)
        ff = jnp.zeros_like(x1)
        for c in range(D_FF // FF_CHUNK):
            cols = slice(c * FF_CHUNK, (c + 1) * FF_CHUNK)
            hdn = jnp.square(jnp.maximum(_dot(x1b, w1_ref[:, cols]), 0.0))
            ff = ff + _dot(hdn.astype(_BF16), w2_ref[cols, :])
        y_ref[rows, :] = _layer_norm(ALPHA * x1 + ff, g2_ref[...], b2_ref[...])


def _finish(x, yconv, attn, w_out, g1, b1, w_ff1, w_ff2, g2, b2):
    r = x.shape[0]
    tile = min(ROW_TILE, r)
    assert r % tile == 0
    row = lambda i: (i, 0)
    const = lambda shape: pl.BlockSpec(shape, lambda i: (0, 0))
    return pl.pallas_call(
        _finish_kernel,
        grid=(r // tile,),
        in_specs=[
            pl.BlockSpec((tile, D_MODEL), row),
            pl.BlockSpec((tile, CONV_DIM), row),
            pl.BlockSpec((tile, ATTN_DIM), row),
            const((D_MODEL, D_MODEL)), const((1, D_MODEL)), const((1, D_MODEL)),
            const((D_MODEL, D_FF)), const((D_FF, D_MODEL)), const((1, D_MODEL)), const((1, D_MODEL)),
        ],
        out_specs=pl.BlockSpec((tile, D_MODEL), row),
        out_shape=jax.ShapeDtypeStruct((r, D_MODEL), _F32),
        compiler_params=pltpu.CompilerParams(
            dimension_semantics=("arbitrary",), vmem_limit_bytes=_vmem_limit(56 << 20)),
        name="finish_layer",
    )(x, yconv, attn, w_out, g1, b1, w_ff1, w_ff2, g2, b2)


def kernel(x_prompt, x_sample, cache_k, cache_v, state_conv, w_in, conv_w, lambda_q1, lambda_k1,
           lambda_q2, lambda_k2, subln_g, w_out, ln1_g, ln1_b, w_ff1, w_ff2, ln2_g, ln2_b):
    bp, sp, _ = x_prompt.shape
    bs, ss, _ = x_sample.shape
    depth = w_in.shape[0]
    assert bp == 1 and depth == 1
    past_len = cache_k.shape[2]
    l = 0
    lambda_init = 0.8 - 0.6 * float(np.exp(-0.3 * l))

    w_in_b = w_in[l].astype(_BF16)
    w_out_b = w_out[l].astype(_BF16)
    w_ff1_b = w_ff1[l].astype(_BF16)
    w_ff2_b = w_ff2[l].astype(_BF16)
    lam_vecs = tuple(v[l].reshape(1, QK_DIM) for v in (lambda_q1, lambda_k1, lambda_q2, lambda_k2))
    g_sub = subln_g[l].reshape(1, V_DIM)
    ln = tuple(v[l].reshape(1, D_MODEL) for v in (ln1_g, ln1_b, ln2_g, ln2_b))

    def finish(x2d, yconv, attn):
        return _finish(x2d, yconv, attn, w_out_b, ln[0], ln[1], w_ff1_b, w_ff2_b, ln[2], ln[3])

    xp2 = x_prompt.reshape(sp, D_MODEL)
    yconv_p, k_p, v_p, qa, qb, ka, kb, vt, conv_p = _project_prompt(xp2, w_in_b, conv_w[l])
    attn_p = _attention_prompt(qa, qb, ka, kb, vt, lam_vecs, g_sub, lambda_init)
    y_prompt = finish(xp2, yconv_p, attn_p).reshape(bp, sp, D_MODEL)

    xs2 = x_sample.reshape(bs * ss, D_MODEL)
    st = state_conv[l].astype(_F32)
    pad = lambda a: jnp.pad(a, ((0, 0), (0, ss - a.shape[1]), (0, 0))).reshape(bs * ss, CONV_DIM)
    hist2 = pad(st)
    hist1 = pad(st[:, 1:])
    yconv_s, q_s, k_s, v_s, u_s = _project_sample(xs2, w_in_b, conv_w[l], hist1, hist2, ss)
    ck = cache_k.reshape(depth * bs, past_len * N_HEADS, HEAD_W)
    cv = cache_v.reshape(depth * bs, past_len * N_HEADS, V_DIM)
    attn_s = _attention_sample(q_s, k_s, v_s, ck, cv, lam_vecs, g_sub, lambda_init, ss)
    y_sample = finish(xs2, yconv_s, attn_s).reshape(bs, ss, D_MODEL)

    k_prompt = k_p.reshape(depth, bp, sp, N_HEADS, HEAD_W)
    v_prompt = v_p.reshape(depth, bp, sp, N_HEADS, V_DIM)
    conv_prompt = conv_p.reshape(depth, bp, CONV_W - 1, CONV_DIM)
    k_sample = k_s.reshape(depth, bs, ss, N_HEADS, HEAD_W)
    v_sample = v_s.reshape(depth, bs, ss, N_HEADS, V_DIM)
    conv_sample = u_s.reshape(bs, ss, CONV_DIM)[:, ss - (CONV_W - 1):].reshape(
        depth, bs, CONV_W - 1, CONV_DIM)
    return (y_prompt, y_sample, k_prompt, v_prompt, conv_prompt, k_sample, v_sample, conv_sample)
```

```python
from functools import partial

import jax
import jax.numpy as jnp
import numpy as np
from jax import lax
from jax.experimental import pallas as pl
from jax.experimental.pallas import tpu as pltpu

D_MODEL = 1024
CHUNK = 64
CONV_DIM = 512
ATTN_DIM = 512
QK_DIM = 64
N_HEADS = 4
HEAD_W = 2 * QK_DIM
V_DIM = 128
F32_SUBLANES = 8
BF16_SUBLANES = 16
V_EXT = V_DIM + BF16_SUBLANES
LOG2E = 1.4426950408889634
CONV_W = 3
D_FF = 4096
LN_EPS = 1e-5
DEPTH = 1
ALPHA = (2 * DEPTH) ** 0.25
IN_WIDTH = 3 * CONV_DIM + 3 * ATTN_DIM

V7X_MXU_DIM = 256
V7X_VMEM_BYTES = 64 * 1024 * 1024

ROW_TILE = 1024
FINISH_GROUP = 512
Q_TILE = 2 * V7X_MXU_DIM
K_TILE = 2 * V7X_MXU_DIM
PROJ_TILE = 1024
TRIP_PAIRS = (4, 2, 1)
FF_CHUNK = 1024

_F32 = jnp.float32
_BF16 = jnp.bfloat16


def _vmem_limit(nbytes):
    assert nbytes <= V7X_VMEM_BYTES - (4 << 20)
    return int(nbytes)


def _dot(a, b):
    return jnp.dot(a, b, preferred_element_type=_F32)


def _dot_nt(a, b):
    return lax.dot_general(a, b, (((1,), (1,)), ((), ())), preferred_element_type=_F32)


def _dot_tn(a, b):
    return lax.dot_general(a, b, (((0,), (0,)), ((), ())), preferred_element_type=_F32)


def _div_pow2(x, n):
    assert n & (n - 1) == 0
    return lax.shift_right_logical(x, n.bit_length() - 1)


def _mod_pow2(x, n):
    assert n & (n - 1) == 0
    return lax.bitwise_and(x, n - 1)


def _layer_norm(x, g, b):
    mu = jnp.mean(x, axis=-1, keepdims=True)
    xc = x - mu
    var = jnp.mean(xc * xc, axis=-1, keepdims=True)
    return xc * lax.rsqrt(var + LN_EPS) * g + b


def _lambda_value(lq1, lk1, lq2, lk2, lambda_init):
    s1 = jnp.sum(lq1 * lk1, axis=-1, keepdims=True)
    s2 = jnp.sum(lq2 * lk2, axis=-1, keepdims=True)
    return jnp.exp(s1) - jnp.exp(s2) + lambda_init


def _project_columns(x_ref, w_ref):
    xb = x_ref[...].astype(_BF16)
    return [_dot(xb, w_ref[:, g * CONV_DIM:(g + 1) * CONV_DIM]) for g in range(IN_WIDTH // CONV_DIM)]


def _conv_from_taps(u, u1, u2, cw_ref):
    return cw_ref[0:1, :] * u2 + cw_ref[1:2, :] * u1 + cw_ref[2:3, :] * u


def _proj_prompt_kernel(x_ref, w_ref, cw_ref,
                        yconv_ref, kf_ref, vf_ref, qa_ref, qb_ref, ka_ref, kb_ref, vt_ref, cstate_ref,
                        carry_ref):
    i = pl.program_id(0)

    @pl.when(i == 0)
    def _():
        carry_ref[...] = jnp.zeros_like(carry_ref)

    gate, cc, hh, q, k, v = _project_columns(x_ref, w_ref)
    u = cc * hh
    rows = lax.broadcasted_iota(jnp.int32, u.shape, 0)
    prev2 = carry_ref[0:1, :]
    prev1 = carry_ref[1:2, :]
    u1 = jnp.where(rows == 0, prev1, pltpu.roll(u, 1, 0))
    u2 = jnp.where(rows == 0, prev2, jnp.where(rows == 1, prev1, pltpu.roll(u, 2, 0)))
    yconv_ref[...] = (gate * _conv_from_taps(u, u1, u2, cw_ref)).astype(_BF16)
    tail = u[PROJ_TILE - (CONV_W - 1):, :]
    carry_ref[0:CONV_W - 1, :] = tail
    cstate_ref[...] = tail

    for h in range(N_HEADS):
        sl = slice(h * HEAD_W, (h + 1) * HEAD_W)
        kf_ref[pl.ds(h, PROJ_TILE, stride=N_HEADS), :] = k[:, sl]
        vf_ref[pl.ds(h, PROJ_TILE, stride=N_HEADS), :] = v[:, sl]
    qs = q * (QK_DIM ** -0.5 * LOG2E)
    lane = lax.broadcasted_iota(jnp.int32, (PROJ_TILE, HEAD_W), 1)
    first = lane < QK_DIM
    ones_row = jnp.where(lax.broadcasted_iota(jnp.int32, (BF16_SUBLANES, K_TILE), 0) == 0,
                         1.0, 0.0).astype(_BF16)
    chunks_per_ktile = K_TILE // CHUNK
    assert QK_DIM // chunks_per_ktile >= 2
    row = lax.broadcasted_iota(jnp.int32, (PROJ_TILE, HEAD_W), 0)
    key_chunk = _div_pow2(_mod_pow2(row, K_TILE), CHUNK)
    lane_chunk = _mod_pow2(lane, chunks_per_ktile)
    chunk_onehot = jnp.where(lane_chunk == key_chunk, 1.0, 0.0)
    q_tile = _div_pow2(i * PROJ_TILE + row, Q_TILE)
    tile_offset = _mod_pow2(q_tile, K_TILE // Q_TILE) * (Q_TILE // CHUNK)
    query_chunk = _div_pow2(_mod_pow2(row, Q_TILE), CHUNK) + tile_offset
    mask_weight = jnp.where(lane_chunk > query_chunk, float(jnp.finfo(_BF16).min), 0.0)
    for h in range(N_HEADS):
        sl = slice(h * HEAD_W, (h + 1) * HEAD_W)
        qh = qs[:, sl]
        kh = k[:, sl]
        qa_ref[h, 0] = jnp.where(first, qh, 0.0).astype(_BF16)
        qb_ref[h, 0] = jnp.where(first, 0.0, qh).astype(_BF16)
        qa_ref[h, 1] = jnp.where(first, qh, mask_weight).astype(_BF16)
        qb_ref[h, 1] = jnp.where(first, mask_weight, qh).astype(_BF16)
        ka_ref[h] = jnp.where(first, kh, chunk_onehot).astype(_BF16)
        kb_ref[h] = jnp.where(first, chunk_onehot, kh).astype(_BF16)
        for s in range(PROJ_TILE // K_TILE):
            vt_ref[h, s, :V_DIM, :] = v[s * K_TILE:(s + 1) * K_TILE, sl].T.astype(_BF16)
            vt_ref[h, s, V_DIM:, :] = ones_row


def _proj_sample_kernel(x_ref, w_ref, cw_ref, h1_ref, h2_ref,
                        yconv_ref, q_ref, k_ref, v_ref, u_ref, *, period):
    gate, cc, hh, q, k, v = _project_columns(x_ref, w_ref)
    u = cc * hh
    t = _mod_pow2(lax.broadcasted_iota(jnp.int32, u.shape, 0), period)
    u1 = jnp.where(t < 1, h1_ref[...], pltpu.roll(u, 1, 0))
    u2 = jnp.where(t < 2, h2_ref[...], pltpu.roll(u, 2, 0))
    yconv_ref[...] = (gate * _conv_from_taps(u, u1, u2, cw_ref)).astype(_BF16)
    q_ref[...] = q * (QK_DIM ** -0.5)
    k_ref[...] = k
    v_ref[...] = v
    u_ref[...] = u


def _project_prompt(x, w_in, conv_w):
    s = x.shape[0]
    assert s % PROJ_TILE == 0 and PROJ_TILE % K_TILE == 0
    n = s // PROJ_TILE
    row = lambda i: (i, 0)
    head_row = lambda i: (0, i, 0)
    out_shape = (
        jax.ShapeDtypeStruct((s, CONV_DIM), _BF16),
        jax.ShapeDtypeStruct((s * N_HEADS, HEAD_W), _F32),
        jax.ShapeDtypeStruct((s * N_HEADS, V_DIM), _F32),
        jax.ShapeDtypeStruct((N_HEADS, 2, s, HEAD_W), _BF16),
        jax.ShapeDtypeStruct((N_HEADS, 2, s, HEAD_W), _BF16),
        jax.ShapeDtypeStruct((N_HEADS, s, HEAD_W), _BF16),
        jax.ShapeDtypeStruct((N_HEADS, s, HEAD_W), _BF16),
        jax.ShapeDtypeStruct((N_HEADS, s // K_TILE, V_EXT, K_TILE), _BF16),
        jax.ShapeDtypeStruct((CONV_W - 1, CONV_DIM), _F32),
    )
    out_specs = (
        pl.BlockSpec((PROJ_TILE, CONV_DIM), row),
        pl.BlockSpec((PROJ_TILE * N_HEADS, HEAD_W), row),
        pl.BlockSpec((PROJ_TILE * N_HEADS, V_DIM), row),
        pl.BlockSpec((N_HEADS, 2, PROJ_TILE, HEAD_W), lambda i: (0, 0, i, 0)),
        pl.BlockSpec((N_HEADS, 2, PROJ_TILE, HEAD_W), lambda i: (0, 0, i, 0)),
        pl.BlockSpec((N_HEADS, PROJ_TILE, HEAD_W), head_row),
        pl.BlockSpec((N_HEADS, PROJ_TILE, HEAD_W), head_row),
        pl.BlockSpec((N_HEADS, PROJ_TILE // K_TILE, V_EXT, K_TILE), lambda i: (0, i, 0, 0)),
        pl.BlockSpec((CONV_W - 1, CONV_DIM), lambda i: (0, 0)),
    )
    return pl.pallas_call(
        _proj_prompt_kernel,
        grid=(n,),
        in_specs=[
            pl.BlockSpec((PROJ_TILE, D_MODEL), row),
            pl.BlockSpec((D_MODEL, IN_WIDTH), lambda i: (0, 0)),
            pl.BlockSpec((CONV_W, CONV_DIM), lambda i: (0, 0)),
        ],
        out_specs=out_specs,
        out_shape=out_shape,
        scratch_shapes=[pltpu.VMEM((8, CONV_DIM), _F32)],
        compiler_params=pltpu.CompilerParams(
            dimension_semantics=("arbitrary",), vmem_limit_bytes=_vmem_limit(48 << 20)),
        name="project_prompt",
    )(x, w_in, conv_w)


def _project_sample(x, w_in, conv_w, hist1, hist2, period):
    r = x.shape[0]
    full = lambda shape: pl.BlockSpec(shape, lambda i: (0,) * len(shape))
    wide = (r, CONV_DIM)
    return pl.pallas_call(
        partial(_proj_sample_kernel, period=period),
        grid=(1,),
        in_specs=[full((r, D_MODEL)), full((D_MODEL, IN_WIDTH)), full((CONV_W, CONV_DIM)),
                  full(wide), full(wide)],
        out_specs=(full(wide),) * 5,
        out_shape=(jax.ShapeDtypeStruct(wide, _BF16),) + (jax.ShapeDtypeStruct(wide, _F32),) * 4,
        compiler_params=pltpu.CompilerParams(
            dimension_semantics=("arbitrary",), vmem_limit_bytes=_vmem_limit(48 << 20)),
        name="project_sample",
    )(x, w_in, conv_w, hist1, hist2)


def _attn_prompt_kernel(qa_ref, qb_ref, ka_ref, kb_ref, vt_ref, lq1_ref, lk1_ref, lq2_ref, lk2_ref,
                        g_ref, o_ref, s_even, s_odd, cmax_even, cmax_odd, m_ref, acc_ref,
                        *, lambda_init):
    i = pl.program_id(1)
    q_refs = (qa_ref, qb_ref)
    k_refs = (ka_ref, kb_ref)
    bufs = ((s_even, cmax_even), (s_odd, cmax_odd))
    n_full = _div_pow2(i, K_TILE // Q_TILE)

    def col_max(st):
        part = jnp.max(st.reshape(K_TILE // F32_SUBLANES, F32_SUBLANES, Q_TILE), axis=0)
        return jnp.max(part, axis=0, keepdims=True)

    def scores(t, buf):
        s_ref, cmax_ref = buf
        rows = pl.ds(pl.multiple_of(t * K_TILE, K_TILE), K_TILE)
        variant = (t == n_full).astype(jnp.int32)
        for c in range(2):
            st = _dot_nt(k_refs[c][0, rows, :], q_refs[c][0, variant])
            s_ref[c] = st
            cmax_ref[c] = col_max(st)

    def consume(t, buf):
        s_ref, cmax_ref = buf
        vt = vt_ref[0, t]
        for c in range(2):
            m = m_ref[c]
            m_new = jnp.maximum(m, cmax_ref[c])
            alpha = jnp.exp2(m - m_new)
            p = jnp.exp2(s_ref[c] - m_new).astype(_BF16)
            acc_ref[c] = alpha * acc_ref[c] + _dot(vt, p)
            m_ref[c] = m_new

    m_ref[...] = jnp.full(m_ref.shape, -jnp.inf, _F32)
    acc_ref[...] = jnp.zeros(acc_ref.shape, _F32)
    scores(0, bufs[0])

    def pairs(t, count):
        for k in range(count):
            scores(t + 2 * k + 1, bufs[1])
            consume(t + 2 * k, bufs[0])
            scores(t + 2 * k + 2, bufs[0])
            consume(t + 2 * k + 1, bufs[1])

    n_pairs = _div_pow2(n_full, 2)
    pairs_done = 0
    for size in TRIP_PAIRS:
        n_trips = _div_pow2(n_pairs - pairs_done, size)

        def trip(p, carry, size=size, first=pairs_done):
            pairs(2 * (first + p * size), size)
            return carry

        lax.fori_loop(0, n_trips, trip, 0)
        pairs_done = pairs_done + n_trips * size

    t0 = 2 * n_pairs
    two_left = n_full > t0

    @pl.when(two_left)
    def _():
        scores(n_full, bufs[1])
        consume(t0, bufs[0])
        consume(n_full, bufs[1])

    @pl.when(jnp.logical_not(two_left))
    def _():
        consume(t0, bufs[0])

    lam =_lambda_value(lq1_ref[...], lk1_ref[...], lq2_ref[...], lk2_ref[...], lambda_init)
    acc0 = acc_ref[0]
    acc1 = acc_ref[1]
    l0 = acc0[V_DIM:V_DIM + 1, :]
    l1 = acc1[V_DIM:V_DIM + 1, :]
    o = acc0[:V_DIM] / l0 - lam * (acc1[:V_DIM] / l1)
    ms = jnp.mean(o * o, axis=0, keepdims=True)
    on = o * lax.rsqrt(ms + LN_EPS)
    on = on.T * g_ref[...] * (1.0 - lambda_init)
    o_ref[...] = on.astype(_BF16)


def _attention_prompt(qa, qb, ka, kb, vt, lam_vecs, subln_g, lambda_init):
    s = ka.shape[1]
    assert K_TILE % Q_TILE == 0 and s % K_TILE == 0 and Q_TILE % CHUNK == 0
    nq = s // Q_TILE
    vec = pl.BlockSpec((1, QK_DIM), lambda h, i: (0, 0))
    q_tile = pl.BlockSpec((1, 2, Q_TILE, HEAD_W), lambda h, i: (h, 0, i, 0))
    return pl.pallas_call(
        partial(_attn_prompt_kernel, lambda_init=lambda_init),
        grid=(N_HEADS, nq),
        in_specs=[
            q_tile, q_tile,
            pl.BlockSpec((1, s, HEAD_W), lambda h, i: (h, 0, 0)),
            pl.BlockSpec((1, s, HEAD_W), lambda h, i: (h, 0, 0)),
            pl.BlockSpec((1, s // K_TILE, V_EXT, K_TILE), lambda h, i: (h, 0, 0, 0)),
            vec, vec, vec, vec,
            pl.BlockSpec((1, V_DIM), lambda h, i: (0, 0)),
        ],
        out_specs=pl.BlockSpec((Q_TILE, V_DIM), lambda h, i: (i, h)),
        out_shape=jax.ShapeDtypeStruct((s, ATTN_DIM), _BF16),
        scratch_shapes=[
            pltpu.VMEM((2, K_TILE, Q_TILE), _F32), pltpu.VMEM((2, K_TILE, Q_TILE), _F32),
            pltpu.VMEM((2, 1, Q_TILE), _F32), pltpu.VMEM((2, 1, Q_TILE), _F32),
            pltpu.VMEM((2, 1, Q_TILE), _F32), pltpu.VMEM((2, V_EXT, Q_TILE), _F32),
        ],
        compiler_params=pltpu.CompilerParams(
            dimension_semantics=("arbitrary", "arbitrary"), vmem_limit_bytes=_vmem_limit(48 << 20)),
        name="attention_prompt",
    )(qa, qb, ka, kb, vt, *lam_vecs, subln_g)


def _attn_sample_kernel(q_ref, kn_ref, vn_ref, ck_ref, cv_ref, lq1_ref, lk1_ref, lq2_ref, lk2_ref,
                        g_ref, o_ref, *, lambda_init, past_len, n_new):
    group = 2 * N_HEADS
    width = group * n_new
    assert width == HEAD_W
    qb = q_ref[...].astype(_BF16)
    sel_r = lax.broadcasted_iota(jnp.int32, (n_new, width), 0)
    sel_c = lax.broadcasted_iota(jnp.int32, (n_new, width), 1)
    spread = jnp.where(_mod_pow2(sel_c, n_new) == sel_r, 1.0, 0.0).astype(_BF16)
    q_all = _dot_tn(qb, spread)
    blk_r = _div_pow2(lax.broadcasted_iota(jnp.int32, (ATTN_DIM, width), 0), QK_DIM)
    blk_c = _div_pow2(lax.broadcasted_iota(jnp.int32, (ATTN_DIM, width), 1), n_new)
    q_bd = jnp.where(blk_r == blk_c, q_all, 0.0).astype(_BF16)

    def heads_on_lanes(ref):
        return jnp.concatenate(
            [ref[0, pl.ds(h, past_len, stride=N_HEADS), :].astype(_BF16) for h in range(N_HEADS)],
            axis=-1)

    s_old = _dot(heads_on_lanes(ck_ref), q_bd)
    s_new = _dot(kn_ref[...].astype(_BF16), q_bd)

    def chunk_mask(shape, k_off):
        k_pos = k_off + lax.broadcasted_iota(jnp.int32, shape, 0)
        q_pos = past_len + _mod_pow2(lax.broadcasted_iota(jnp.int32, shape, 1), n_new)
        return _div_pow2(k_pos, CHUNK) <= _div_pow2(q_pos, CHUNK)

    s_old = jnp.where(chunk_mask(s_old.shape, 0), s_old, -jnp.inf)
    s_new = jnp.where(chunk_mask(s_new.shape, past_len), s_new, -jnp.inf)
    m = jnp.maximum(jnp.max(s_old, axis=0, keepdims=True), jnp.max(s_new, axis=0, keepdims=True))
    p_old = jnp.exp(s_old - m)
    p_new = jnp.exp(s_new - m)
    inv_l = 1.0 / (jnp.sum(p_old, axis=0, keepdims=True) + jnp.sum(p_new, axis=0, keepdims=True))
    a_old = (p_old * inv_l).astype(_BF16)
    a_new = (p_new * inv_l).astype(_BF16)
    pv = _dot_tn(a_old, heads_on_lanes(cv_ref)) + _dot_tn(a_new, vn_ref[...].astype(_BF16))

    lam = _lambda_value(lq1_ref[...], lk1_ref[...], lq2_ref[...], lk2_ref[...], lambda_init)
    g = g_ref[...]
    for h in range(N_HEADS):
        cols = slice(h * V_DIM, (h + 1) * V_DIM)
        r0 = h * 2 * n_new
        o = pv[r0:r0 + n_new, cols] - lam * pv[r0 + n_new:r0 + 2 * n_new, cols]
        ms = jnp.mean(o * o, axis=-1, keepdims=True)
        o_ref[:, cols] = (o * lax.rsqrt(ms + LN_EPS) * g * (1.0 - lambda_init)).astype(_BF16)


def _attention_sample(q, k_new, v_new, cache_k, cache_v, lam_vecs, subln_g, lambda_init, n_new):
    nb, rows_per_stream = cache_k.shape[:2]
    past_len = rows_per_stream // N_HEADS
    cache_block = pl.BlockSpec((1, rows_per_stream, HEAD_W), lambda b: (b, 0, 0))
    rows = lambda b: (b, 0)
    vec = pl.BlockSpec((1, QK_DIM), lambda b: (0, 0))
    return pl.pallas_call(
        partial(_attn_sample_kernel, lambda_init=lambda_init, past_len=past_len, n_new=n_new),
        grid=(nb,),
        in_specs=[
            pl.BlockSpec((n_new, ATTN_DIM), rows),
            pl.BlockSpec((n_new, ATTN_DIM), rows),
            pl.BlockSpec((n_new, ATTN_DIM), rows),
            cache_block, cache_block,
            vec, vec, vec, vec,
            pl.BlockSpec((1, V_DIM), lambda b: (0, 0)),
        ],
        out_specs=pl.BlockSpec((n_new, ATTN_DIM), rows),
        out_shape=jax.ShapeDtypeStruct((nb * n_new, ATTN_DIM), _BF16),
        compiler_params=pltpu.CompilerParams(
            dimension_semantics=("arbitrary",), vmem_limit_bytes=_vmem_limit(40 << 20)),
        name="attention_sample",
    )(q, k_new, v_new, cache_k, cache_v, *lam_vecs, subln_g)


def _finish_kernel(x_ref, yconv_ref, attn_ref, wo_ref, g1_ref, b1_ref, w1_ref, w2_ref, g2_ref, b2_ref,
                   y_ref):
    tile = x_ref.shape[0]
    group = min(tile, FINISH_GROUP)
    groups = [slice(r0, r0 + group) for r0 in range(0, tile, group)]
    pre = [ALPHA * x_ref[rows, :]
           + _dot(jnp.concatenate([yconv_ref[rows, :], attn_ref[rows, :]], axis=-1), wo_ref[...])
           for rows in groups]
    for rows, z in zip(groups, pre):
        x1 = _layer_norm(z, g1_ref[...], b1_ref[...])
        x1b = x1.astype(_BF16)
        ff = jnp.zeros_like(x1)
        for c in range(D_FF // FF_CHUNK):
            cols = slice(c * FF_CHUNK, (c + 1) * FF_CHUNK)
            hdn = jnp.square(jnp.maximum(_dot(x1b, w1_ref[:, cols]), 0.0))
            ff = ff + _dot(hdn.astype(_BF16), w2_ref[cols, :])
        y_ref[rows, :] = _layer_norm(ALPHA * x1 + ff, g2_ref[...], b2_ref[...])


def _finish(x, yconv, attn, w_out, g1, b1, w_ff1, w_ff2, g2, b2):
    r = x.shape[0]
    tile = min(ROW_TILE, r)
    assert r % tile == 0
    row = lambda i: (i, 0)
    const = lambda shape: pl.BlockSpec(shape, lambda i: (0, 0))
    return pl.pallas_call(
        _finish_kernel,
        grid=(r // tile,),
        in_specs=[
            pl.BlockSpec((tile, D_MODEL), row),
            pl.BlockSpec((tile, CONV_DIM), row),
            pl.BlockSpec((tile, ATTN_DIM), row),
            const((D_MODEL, D_MODEL)), const((1, D_MODEL)), const((1, D_MODEL)),
            const((D_MODEL, D_FF)), const((D_FF, D_MODEL)), const((1, D_MODEL)), const((1, D_MODEL)),
        ],
        out_specs=pl.BlockSpec((tile, D_MODEL), row),
        out_shape=jax.ShapeDtypeStruct((r, D_MODEL), _F32),
        compiler_params=pltpu.CompilerParams(
            dimension_semantics=("arbitrary",), vmem_limit_bytes=_vmem_limit(56 << 20)),
        name="finish_layer",
    )(x, yconv, attn, w_out, g1, b1, w_ff1, w_ff2, g2, b2)


def kernel(x_prompt, x_sample, cache_k, cache_v, state_conv, w_in, conv_w, lambda_q1, lambda_k1,
           lambda_q2, lambda_k2, subln_g, w_out, ln1_g, ln1_b, w_ff1, w_ff2, ln2_g, ln2_b):
    bp, sp, _ = x_prompt.shape
    bs, ss, _ = x_sample.shape
    depth = w_in.shape[0]
    assert bp == 1 and depth == 1
    past_len = cache_k.shape[2]
    l = 0
    lambda_init = 0.8 - 0.6 * float(np.exp(-0.3 * l))

    w_in_b = w_in[l].astype(_BF16)
    w_out_b = w_out[l].astype(_BF16)
    w_ff1_b = w_ff1[l].astype(_BF16)
    w_ff2_b = w_ff2[l].astype(_BF16)
    lam_vecs = tuple(v[l].reshape(1, QK_DIM) for v in (lambda_q1, lambda_k1, lambda_q2, lambda_k2))
    g_sub = subln_g[l].reshape(1, V_DIM)
    ln = tuple(v[l].reshape(1, D_MODEL) for v in (ln1_g, ln1_b, ln2_g, ln2_b))

    def finish(x2d, yconv, attn):
        return _finish(x2d, yconv, attn, w_out_b, ln[0], ln[1], w_ff1_b, w_ff2_b, ln[2], ln[3])

    xp2 = x_prompt.reshape(sp, D_MODEL)
    yconv_p, k_p, v_p, qa, qb, ka, kb, vt, conv_p = _project_prompt(xp2, w_in_b, conv_w[l])
    attn_p = _attention_prompt(qa, qb, ka, kb, vt, lam_vecs, g_sub, lambda_init)
    y_prompt = finish(xp2, yconv_p, attn_p).reshape(bp, sp, D_MODEL)

    xs2 = x_sample.reshape(bs * ss, D_MODEL)
    st = state_conv[l].astype(_F32)
    pad = lambda a: jnp.pad(a, ((0, 0), (0, ss - a.shape[1]), (0, 0))).reshape(bs * ss, CONV_DIM)
    hist2 = pad(st)
    hist1 = pad(st[:, 1:])
    yconv_s, q_s, k_s, v_s, u_s = _project_sample(xs2, w_in_b, conv_w[l], hist1, hist2, ss)
    ck = cache_k.reshape(depth * bs, past_len * N_HEADS, HEAD_W)
    cv = cache_v.reshape(depth * bs, past_len * N_HEADS, V_DIM)
    attn_s = _attention_sample(q_s, k_s, v_s, ck, cv, lam_vecs, g_sub, lambda_init, ss)
    y_sample = finish(xs2, yconv_s, attn_s).reshape(bs, ss, D_MODEL)

    k_prompt = k_p.reshape(depth, bp, sp, N_HEADS, HEAD_W)
    v_prompt = v_p.reshape(depth, bp, sp, N_HEADS, V_DIM)
    conv_prompt = conv_p.reshape(depth, bp, CONV_W - 1, CONV_DIM)
    k_sample = k_s.reshape(depth, bs, ss, N_HEADS, HEAD_W)
    v_sample = v_s.reshape(depth, bs, ss, N_HEADS, V_DIM)
    conv_sample = u_s.reshape(bs, ss, CONV_DIM)[:, ss - (CONV_W - 1):].reshape(
        depth, bs, CONV_W - 1, CONV_DIM)
    return (y_prompt, y_sample, k_prompt, v_prompt, conv_prompt, k_sample, v_sample, conv_sample)
```

```python
from functools import partial

import jax
import jax.numpy as jnp
import numpy as np
from jax import lax
from jax.experimental import pallas as pl
from jax.experimental.pallas import tpu as pltpu

D_MODEL = 1024
CHUNK = 64
CONV_DIM = 512
ATTN_DIM = 512
QK_DIM = 64
N_HEADS = 4
HEAD_W = 2 * QK_DIM
V_DIM = 128
F32_SUBLANES = 8
BF16_SUBLANES = 16
V_EXT = V_DIM + BF16_SUBLANES
LOG2E = 1.4426950408889634
CONV_W = 3
D_FF = 4096
LN_EPS = 1e-5
DEPTH = 1
ALPHA = (2 * DEPTH) ** 0.25
IN_WIDTH = 3 * CONV_DIM + 3 * ATTN_DIM

V7X_MXU_DIM = 256
V7X_VMEM_BYTES = 64 * 1024 * 1024

ROW_TILE = 1024
FINISH_GROUP = 512
Q_TILE = 2 * V7X_MXU_DIM
K_TILE = 2 * V7X_MXU_DIM
PROJ_TILE = 1024
TRIP_PAIRS = (4, 2, 1)
FF_CHUNK = 1024

_F32 = jnp.float32
_BF16 = jnp.bfloat16


def _vmem_limit(nbytes):
    assert nbytes <= V7X_VMEM_BYTES - (4 << 20)
    return int(nbytes)


def _dot(a, b):
    return jnp.dot(a, b, preferred_element_type=_F32)


def _dot_nt(a, b):
    return lax.dot_general(a, b, (((1,), (1,)), ((), ())), preferred_element_type=_F32)


def _dot_tn(a, b):
    return lax.dot_general(a, b, (((0,), (0,)), ((), ())), preferred_element_type=_F32)


def _div_pow2(x, n):
    assert n & (n - 1) == 0
    return lax.shift_right_logical(x, n.bit_length() - 1)


def _mod_pow2(x, n):
    assert n & (n - 1) == 0
    return lax.bitwise_and(x, n - 1)


def _layer_norm(x, g, b):
    mu = jnp.mean(x, axis=-1, keepdims=True)
    xc = x - mu
    var = jnp.mean(xc * xc, axis=-1, keepdims=True)
    return xc * lax.rsqrt(var + LN_EPS) * g + b


def _lambda_value(lq1, lk1, lq2, lk2, lambda_init):
    s1 = jnp.sum(lq1 * lk1, axis=-1, keepdims=True)
    s2 = jnp.sum(lq2 * lk2, axis=-1, keepdims=True)
    return jnp.exp(s1) - jnp.exp(s2) + lambda_init


def _project_columns(x_ref, w_ref):
    xb = x_ref[...].astype(_BF16)
    return [_dot(xb, w_ref[:, g * CONV_DIM:(g + 1) * CONV_DIM]) for g in range(IN_WIDTH // CONV_DIM)]


def _conv_from_taps(u, u1, u2, cw_ref):
    return cw_ref[0:1, :] * u2 + cw_ref[1:2, :] * u1 + cw_ref[2:3, :] * u


def _proj_prompt_kernel(x_ref, w_ref, cw_ref, wo_ref, w1_ref, w2_ref,
                        yconv_ref, kf_ref, vf_ref, qa_ref, qb_ref, ka_ref, kb_ref, vt_ref, cstate_ref,
                        wo_b_ref, w1_b_ref, w2_b_ref, carry_ref):
    i = pl.program_id(0)

    wo_b_ref[...] = wo_ref[...].astype(_BF16)
    w1_b_ref[...] = w1_ref[...].astype(_BF16)
    w2_b_ref[...] = w2_ref[...].astype(_BF16)

    @pl.when(i == 0)
    def _():
        carry_ref[...] = jnp.zeros_like(carry_ref)

    gate, cc, hh, q, k, v = _project_columns(x_ref, w_ref)
    u = cc * hh
    rows = lax.broadcasted_iota(jnp.int32, u.shape, 0)
    prev2 = carry_ref[0:1, :]
    prev1 = carry_ref[1:2, :]
    u1 = jnp.where(rows == 0, prev1, pltpu.roll(u, 1, 0))
    u2 = jnp.where(rows == 0, prev2, jnp.where(rows == 1, prev1, pltpu.roll(u, 2, 0)))
    yconv_ref[...] = (gate * _conv_from_taps(u, u1, u2, cw_ref)).astype(_BF16)
    tail = u[PROJ_TILE - (CONV_W - 1):, :]
    carry_ref[0:CONV_W - 1, :] = tail
    cstate_ref[...] = tail

    for h in range(N_HEADS):
        sl = slice(h * HEAD_W, (h + 1) * HEAD_W)
        kf_ref[pl.ds(h, PROJ_TILE, stride=N_HEADS), :] = k[:, sl]
        vf_ref[pl.ds(h, PROJ_TILE, stride=N_HEADS), :] = v[:, sl]
    qs = q * (QK_DIM ** -0.5 * LOG2E)
    lane = lax.broadcasted_iota(jnp.int32, (PROJ_TILE, HEAD_W), 1)
    first = lane < QK_DIM
    ones_row = jnp.where(lax.broadcasted_iota(jnp.int32, (BF16_SUBLANES, K_TILE), 0) == 0,
                         1.0, 0.0).astype(_BF16)
    chunks_per_ktile = K_TILE // CHUNK
    assert QK_DIM // chunks_per_ktile >= 2
    row = lax.broadcasted_iota(jnp.int32, (PROJ_TILE, HEAD_W), 0)
    key_chunk = _div_pow2(_mod_pow2(row, K_TILE), CHUNK)
    lane_chunk = _mod_pow2(lane, chunks_per_ktile)
    chunk_onehot = jnp.where(lane_chunk == key_chunk, 1.0, 0.0)
    q_tile = _div_pow2(i * PROJ_TILE + row, Q_TILE)
    tile_offset = _mod_pow2(q_tile, K_TILE // Q_TILE) * (Q_TILE // CHUNK)
    query_chunk = _div_pow2(_mod_pow2(row, Q_TILE), CHUNK) + tile_offset
    mask_weight = jnp.where(lane_chunk > query_chunk, float(jnp.finfo(_BF16).min), 0.0)
    for h in range(N_HEADS):
        sl = slice(h * HEAD_W, (h + 1) * HEAD_W)
        qh = qs[:, sl]
        kh = k[:, sl]
        qa_ref[h, 0] = jnp.where(first, qh, 0.0).astype(_BF16)
        qb_ref[h, 0] = jnp.where(first, 0.0, qh).astype(_BF16)
        qa_ref[h, 1] = jnp.where(first, qh, mask_weight).astype(_BF16)
        qb_ref[h, 1] = jnp.where(first, mask_weight, qh).astype(_BF16)
        ka_ref[h] = jnp.where(first, kh, chunk_onehot).astype(_BF16)
        kb_ref[h] = jnp.where(first, chunk_onehot, kh).astype(_BF16)
        for s in range(PROJ_TILE // K_TILE):
            vt_ref[h, s, :V_DIM, :] = v[s * K_TILE:(s + 1) * K_TILE, sl].T.astype(_BF16)
            vt_ref[h, s, V_DIM:, :] = ones_row


def _proj_sample_kernel(x_ref, w_ref, cw_ref, h1_ref, h2_ref,
                        yconv_ref, q_ref, k_ref, v_ref, u_ref, *, period):
    gate, cc, hh, q, k, v = _project_columns(x_ref, w_ref)
    u = cc * hh
    t = _mod_pow2(lax.broadcasted_iota(jnp.int32, u.shape, 0), period)
    u1 = jnp.where(t < 1, h1_ref[...], pltpu.roll(u, 1, 0))
    u2 = jnp.where(t < 2, h2_ref[...], pltpu.roll(u, 2, 0))
    yconv_ref[...] = (gate * _conv_from_taps(u, u1, u2, cw_ref)).astype(_BF16)
    q_ref[...] = q * (QK_DIM ** -0.5)
    k_ref[...] = k
    v_ref[...] = v
    u_ref[...] = u


def _project_prompt(x, w_in, conv_w, later_weights):
    s = x.shape[0]
    assert s % PROJ_TILE == 0 and PROJ_TILE % K_TILE == 0
    n = s // PROJ_TILE
    row = lambda i: (i, 0)
    head_row = lambda i: (0, i, 0)
    for w in later_weights:
        assert w.shape[0] % (n * BF16_SUBLANES) == 0
    weight_specs = [pl.BlockSpec((w.shape[0] // n, w.shape[1]), row) for w in later_weights]
    out_shape = (
        jax.ShapeDtypeStruct((s, CONV_DIM), _BF16),
        jax.ShapeDtypeStruct((s * N_HEADS, HEAD_W), _F32),
        jax.ShapeDtypeStruct((s * N_HEADS, V_DIM), _F32),
        jax.ShapeDtypeStruct((N_HEADS, 2, s, HEAD_W), _BF16),
        jax.ShapeDtypeStruct((N_HEADS, 2, s, HEAD_W), _BF16),
        jax.ShapeDtypeStruct((N_HEADS, s, HEAD_W), _BF16),
        jax.ShapeDtypeStruct((N_HEADS, s, HEAD_W), _BF16),
        jax.ShapeDtypeStruct((N_HEADS, s // K_TILE, V_EXT, K_TILE), _BF16),
        jax.ShapeDtypeStruct((CONV_W - 1, CONV_DIM), _F32),
    ) + tuple(jax.ShapeDtypeStruct(w.shape, _BF16) for w in later_weights)
    out_specs = (
        pl.BlockSpec((PROJ_TILE, CONV_DIM), row),
        pl.BlockSpec((PROJ_TILE * N_HEADS, HEAD_W), row),
        pl.BlockSpec((PROJ_TILE * N_HEADS, V_DIM), row),
        pl.BlockSpec((N_HEADS, 2, PROJ_TILE, HEAD_W), lambda i: (0, 0, i, 0)),
        pl.BlockSpec((N_HEADS, 2, PROJ_TILE, HEAD_W), lambda i: (0, 0, i, 0)),
        pl.BlockSpec((N_HEADS, PROJ_TILE, HEAD_W), head_row),
        pl.BlockSpec((N_HEADS, PROJ_TILE, HEAD_W), head_row),
        pl.BlockSpec((N_HEADS, PROJ_TILE // K_TILE, V_EXT, K_TILE), lambda i: (0, i, 0, 0)),
        pl.BlockSpec((CONV_W - 1, CONV_DIM), lambda i: (0, 0)),
    ) + tuple(weight_specs)
    return pl.pallas_call(
        _proj_prompt_kernel,
        grid=(n,),
        in_specs=[
            pl.BlockSpec((PROJ_TILE, D_MODEL), row),
            pl.BlockSpec((D_MODEL, IN_WIDTH), lambda i: (0, 0)),
            pl.BlockSpec((CONV_W, CONV_DIM), lambda i: (0, 0)),
        ] + weight_specs,
        out_specs=out_specs,
        out_shape=out_shape,
        scratch_shapes=[pltpu.VMEM((8, CONV_DIM), _F32)],
        compiler_params=pltpu.CompilerParams(
            dimension_semantics=("arbitrary",), vmem_limit_bytes=_vmem_limit(56 << 20)),
        name="project_prompt",
    )(x, w_in, conv_w, *later_weights)


def _project_sample(x, w_in, conv_w, hist1, hist2, period):
    r = x.shape[0]
    full = lambda shape: pl.BlockSpec(shape, lambda i: (0,) * len(shape))
    wide = (r, CONV_DIM)
    return pl.pallas_call(
        partial(_proj_sample_kernel, period=period),
        grid=(1,),
        in_specs=[full((r, D_MODEL)), full((D_MODEL, IN_WIDTH)), full((CONV_W, CONV_DIM)),
                  full(wide), full(wide)],
        out_specs=(full(wide),) * 5,
        out_shape=(jax.ShapeDtypeStruct(wide, _BF16),) + (jax.ShapeDtypeStruct(wide, _F32),) * 4,
        compiler_params=pltpu.CompilerParams(
            dimension_semantics=("arbitrary",), vmem_limit_bytes=_vmem_limit(48 << 20)),
        name="project_sample",
    )(x, w_in, conv_w, hist1, hist2)


def _attn_prompt_kernel(qa_ref, qb_ref, ka_ref, kb_ref, vt_ref, lq1_ref, lk1_ref, lq2_ref, lk2_ref,
                        g_ref, o_ref, s_even, s_odd, cmax_even, cmax_odd, m_ref, acc_ref,
                        *, lambda_init):
    i = pl.program_id(1)
    q_refs = (qa_ref, qb_ref)
    k_refs = (ka_ref, kb_ref)
    bufs = ((s_even, cmax_even), (s_odd, cmax_odd))
    n_full = _div_pow2(i, K_TILE // Q_TILE)

    def col_max(st):
        part = jnp.max(st.reshape(K_TILE // F32_SUBLANES, F32_SUBLANES, Q_TILE), axis=0)
        return jnp.max(part, axis=0, keepdims=True)

    def scores(t, buf):
        s_ref, cmax_ref = buf
        rows = pl.ds(pl.multiple_of(t * K_TILE, K_TILE), K_TILE)
        variant = (t == n_full).astype(jnp.int32)
        for c in range(2):
            st = _dot_nt(k_refs[c][0, rows, :], q_refs[c][0, variant])
            s_ref[c] = st
            cmax_ref[c] = col_max(st)

    def consume(t, buf):
        s_ref, cmax_ref = buf
        vt = vt_ref[0, t]
        for c in range(2):
            m = m_ref[c]
            m_new = jnp.maximum(m, cmax_ref[c])
            alpha = jnp.exp2(m - m_new)
            p = jnp.exp2(s_ref[c] - m_new).astype(_BF16)
            acc_ref[c] = alpha * acc_ref[c] + _dot(vt, p)
            m_ref[c] = m_new

    m_ref[...] = jnp.full(m_ref.shape, -jnp.inf, _F32)
    acc_ref[...] = jnp.zeros(acc_ref.shape, _F32)
    scores(0, bufs[0])

    def pairs(t, count):
        for k in range(count):
            scores(t + 2 * k + 1, bufs[1])
            consume(t + 2 * k, bufs[0])
            scores(t + 2 * k + 2, bufs[0])
            consume(t + 2 * k + 1, bufs[1])

    n_pairs = _div_pow2(n_full, 2)
    pairs_done = 0
    for size in TRIP_PAIRS:
        n_trips = _div_pow2(n_pairs - pairs_done, size)

        def trip(p, carry, size=size, first=pairs_done):
            pairs(2 * (first + p * size), size)
            return carry

        lax.fori_loop(0, n_trips, trip, 0)
        pairs_done = pairs_done + n_trips * size

    t0 = 2 * n_pairs
    two_left = n_full > t0

    @pl.when(two_left)
    def _():
        scores(n_full, bufs[1])
        consume(t0, bufs[0])
        consume(n_full, bufs[1])

    @pl.when(jnp.logical_not(two_left))
    def _():
        consume(t0, bufs[0])

    lam =_lambda_value(lq1_ref[...], lk1_ref[...], lq2_ref[...], lk2_ref[...], lambda_init)
    acc0 = acc_ref[0]
    acc1 = acc_ref[1]
    l0 = acc0[V_DIM:V_DIM + 1, :]
    l1 = acc1[V_DIM:V_DIM + 1, :]
    o = acc0[:V_DIM] / l0 - lam * (acc1[:V_DIM] / l1)
    ms = jnp.mean(o * o, axis=0, keepdims=True)
    on = o * lax.rsqrt(ms + LN_EPS)
    on = on.T * g_ref[...] * (1.0 - lambda_init)
    o_ref[...] = on.astype(_BF16)


def _attention_prompt(qa, qb, ka, kb, vt, lam_vecs, subln_g, lambda_init):
    s = ka.shape[1]
    assert K_TILE % Q_TILE == 0 and s % K_TILE == 0 and Q_TILE % CHUNK == 0
    nq = s // Q_TILE
    vec = pl.BlockSpec((1, QK_DIM), lambda h, i: (0, 0))
    q_tile = pl.BlockSpec((1, 2, Q_TILE, HEAD_W), lambda h, i: (h, 0, i, 0))
    return pl.pallas_call(
        partial(_attn_prompt_kernel, lambda_init=lambda_init),
        grid=(N_HEADS, nq),
        in_specs=[
            q_tile, q_tile,
            pl.BlockSpec((1, s, HEAD_W), lambda h, i: (h, 0, 0)),
            pl.BlockSpec((1, s, HEAD_W), lambda h, i: (h, 0, 0)),
            pl.BlockSpec((1, s // K_TILE, V_EXT, K_TILE), lambda h, i: (h, 0, 0, 0)),
            vec, vec, vec, vec,
            pl.BlockSpec((1, V_DIM), lambda h, i: (0, 0)),
        ],
        out_specs=pl.BlockSpec((Q_TILE, V_DIM), lambda h, i: (i, h)),
        out_shape=jax.ShapeDtypeStruct((s, ATTN_DIM), _BF16),
        scratch_shapes=[
            pltpu.VMEM((2, K_TILE, Q_TILE), _F32), pltpu.VMEM((2, K_TILE, Q_TILE), _F32),
            pltpu.VMEM((2, 1, Q_TILE), _F32), pltpu.VMEM((2, 1, Q_TILE), _F32),
            pltpu.VMEM((2, 1, Q_TILE), _F32), pltpu.VMEM((2, V_EXT, Q_TILE), _F32),
        ],
        compiler_params=pltpu.CompilerParams(
            dimension_semantics=("arbitrary", "arbitrary"), vmem_limit_bytes=_vmem_limit(48 << 20)),
        name="attention_prompt",
    )(qa, qb, ka, kb, vt, *lam_vecs, subln_g)


def _attn_sample_kernel(q_ref, kn_ref, vn_ref, ck_ref, cv_ref, lq1_ref, lk1_ref, lq2_ref, lk2_ref,
                        g_ref, o_ref, *, lambda_init, past_len, n_new):
    group = 2 * N_HEADS
    width = group * n_new
    assert width == HEAD_W
    qb = q_ref[...].astype(_BF16)
    sel_r = lax.broadcasted_iota(jnp.int32, (n_new, width), 0)
    sel_c = lax.broadcasted_iota(jnp.int32, (n_new, width), 1)
    spread = jnp.where(_mod_pow2(sel_c, n_new) == sel_r, 1.0, 0.0).astype(_BF16)
    q_all = _dot_tn(qb, spread)
    blk_r = _div_pow2(lax.broadcasted_iota(jnp.int32, (ATTN_DIM, width), 0), QK_DIM)
    blk_c = _div_pow2(lax.broadcasted_iota(jnp.int32, (ATTN_DIM, width), 1), n_new)
    q_bd = jnp.where(blk_r == blk_c, q_all, 0.0).astype(_BF16)

    def heads_on_lanes(ref):
        return jnp.concatenate(
            [ref[0, pl.ds(h, past_len, stride=N_HEADS), :].astype(_BF16) for h in range(N_HEADS)],
            axis=-1)

    s_old = _dot(heads_on_lanes(ck_ref), q_bd)
    s_new = _dot(kn_ref[...].astype(_BF16), q_bd)

    def chunk_mask(shape, k_off):
        k_pos = k_off + lax.broadcasted_iota(jnp.int32, shape, 0)
        q_pos = past_len + _mod_pow2(lax.broadcasted_iota(jnp.int32, shape, 1), n_new)
        return _div_pow2(k_pos, CHUNK) <= _div_pow2(q_pos, CHUNK)

    s_old = jnp.where(chunk_mask(s_old.shape, 0), s_old, -jnp.inf)
    s_new = jnp.where(chunk_mask(s_new.shape, past_len), s_new, -jnp.inf)
    m = jnp.maximum(jnp.max(s_old, axis=0, keepdims=True), jnp.max(s_new, axis=0, keepdims=True))
    p_old = jnp.exp(s_old - m)
    p_new = jnp.exp(s_new - m)
    inv_l = 1.0 / (jnp.sum(p_old, axis=0, keepdims=True) + jnp.sum(p_new, axis=0, keepdims=True))
    a_old = (p_old * inv_l).astype(_BF16)
    a_new = (p_new * inv_l).astype(_BF16)
    pv = _dot_tn(a_old, heads_on_lanes(cv_ref)) + _dot_tn(a_new, vn_ref[...].astype(_BF16))

    lam = _lambda_value(lq1_ref[...], lk1_ref[...], lq2_ref[...], lk2_ref[...], lambda_init)
    g = g_ref[...]
    for h in range(N_HEADS):
        cols = slice(h * V_DIM, (h + 1) * V_DIM)
        r0 = h * 2 * n_new
        o = pv[r0:r0 + n_new, cols] - lam * pv[r0 + n_new:r0 + 2 * n_new, cols]
        ms = jnp.mean(o * o, axis=-1, keepdims=True)
        o_ref[:, cols] = (o * lax.rsqrt(ms + LN_EPS) * g * (1.0 - lambda_init)).astype(_BF16)


def _attention_sample(q, k_new, v_new, cache_k, cache_v, lam_vecs, subln_g, lambda_init, n_new):
    nb, rows_per_stream = cache_k.shape[:2]
    past_len = rows_per_stream // N_HEADS
    cache_block = pl.BlockSpec((1, rows_per_stream, HEAD_W), lambda b: (b, 0, 0))
    rows = lambda b: (b, 0)
    vec = pl.BlockSpec((1, QK_DIM), lambda b: (0, 0))
    return pl.pallas_call(
        partial(_attn_sample_kernel, lambda_init=lambda_init, past_len=past_len, n_new=n_new),
        grid=(nb,),
        in_specs=[
            pl.BlockSpec((n_new, ATTN_DIM), rows),
            pl.BlockSpec((n_new, ATTN_DIM), rows),
            pl.BlockSpec((n_new, ATTN_DIM), rows),
            cache_block, cache_block,
            vec, vec, vec, vec,
            pl.BlockSpec((1, V_DIM), lambda b: (0, 0)),
        ],
        out_specs=pl.BlockSpec((n_new, ATTN_DIM), rows),
        out_shape=jax.ShapeDtypeStruct((nb * n_new, ATTN_DIM), _BF16),
        compiler_params=pltpu.CompilerParams(
            dimension_semantics=("arbitrary",), vmem_limit_bytes=_vmem_limit(40 << 20)),
        name="attention_sample",
    )(q, k_new, v_new, cache_k, cache_v, *lam_vecs, subln_g)


def _finish_kernel(x_ref, yconv_ref, attn_ref, wo_ref, g1_ref, b1_ref, w1_ref, w2_ref, g2_ref, b2_ref,
                   y_ref):
    tile = x_ref.shape[0]
    group = min(tile, FINISH_GROUP)
    groups = [slice(r0, r0 + group) for r0 in range(0, tile, group)]
    pre = [ALPHA * x_ref[rows, :]
           + _dot(jnp.concatenate([yconv_ref[rows, :], attn_ref[rows, :]], axis=-1), wo_ref[...])
           for rows in groups]
    for rows, z in zip(groups, pre):
        x1 = _layer_norm(z, g1_ref[...], b1_ref[...])
        x1b = x1.astype(_BF16)
        ff = jnp.zeros_like(x1)
        for c in range(D_FF // FF_CHUNK):
            cols = slice(c * FF_CHUNK, (c + 1) * FF_CHUNK)
            hdn = jnp.square(jnp.maximum(_dot(x1b, w1_ref[:, cols]), 0.0))
            ff = ff + _dot(hdn.astype(_BF16), w2_ref[cols, :])
        y_ref[rows, :] = _layer_norm(ALPHA * x1 + ff, g2_ref[...], b2_ref[...])


def _finish(x, yconv, attn, w_out, g1, b1, w_ff1, w_ff2, g2, b2):
    r = x.shape[0]
    tile = min(ROW_TILE, r)
    assert r % tile == 0
    row = lambda i: (i, 0)
    const = lambda shape: pl.BlockSpec(shape, lambda i: (0, 0))
    return pl.pallas_call(
        _finish_kernel,
        grid=(r // tile,),
        in_specs=[
            pl.BlockSpec((tile, D_MODEL), row),
            pl.BlockSpec((tile, CONV_DIM), row),
            pl.BlockSpec((tile, ATTN_DIM), row),
            const((D_MODEL, D_MODEL)), const((1, D_MODEL)), const((1, D_MODEL)),
            const((D_MODEL, D_FF)), const((D_FF, D_MODEL)), const((1, D_MODEL)), const((1, D_MODEL)),
        ],
        out_specs=pl.BlockSpec((tile, D_MODEL), row),
        out_shape=jax.ShapeDtypeStruct((r, D_MODEL), _F32),
        compiler_params=pltpu.CompilerParams(
            dimension_semantics=("arbitrary",), vmem_limit_bytes=_vmem_limit(56 << 20)),
        name="finish_layer",
    )(x, yconv, attn, w_out, g1, b1, w_ff1, w_ff2, g2, b2)


def kernel(x_prompt, x_sample, cache_k, cache_v, state_conv, w_in, conv_w, lambda_q1, lambda_k1,
           lambda_q2, lambda_k2, subln_g, w_out, ln1_g, ln1_b, w_ff1, w_ff2, ln2_g, ln2_b):
    bp, sp, _ = x_prompt.shape
    bs, ss, _ = x_sample.shape
    depth = w_in.shape[0]
    assert bp == 1 and depth == 1
    past_len = cache_k.shape[2]
    l = 0
    lambda_init = 0.8 - 0.6 * float(np.exp(-0.3 * l))

    w_in_b = w_in[l].astype(_BF16)
    lam_vecs = tuple(v[l].reshape(1, QK_DIM) for v in (lambda_q1, lambda_k1, lambda_q2, lambda_k2))
    g_sub = subln_g[l].reshape(1, V_DIM)
    ln = tuple(v[l].reshape(1, D_MODEL) for v in (ln1_g, ln1_b, ln2_g, ln2_b))

    xp2 = x_prompt.reshape(sp, D_MODEL)
    (yconv_p, k_p, v_p, qa, qb, ka, kb, vt, conv_p, w_out_b, w_ff1_b, w_ff2_b) = _project_prompt(
        xp2, w_in_b, conv_w[l], (w_out[l], w_ff1[l], w_ff2[l]))

    def finish(x2d, yconv, attn):
        return _finish(x2d, yconv, attn, w_out_b, ln[0], ln[1], w_ff1_b, w_ff2_b, ln[2], ln[3])

    attn_p = _attention_prompt(qa, qb, ka, kb, vt, lam_vecs, g_sub, lambda_init)
    y_prompt = finish(xp2, yconv_p, attn_p).reshape(bp, sp, D_MODEL)

    xs2 = x_sample.reshape(bs * ss, D_MODEL)
    st = state_conv[l].astype(_F32)
    pad = lambda a: jnp.pad(a, ((0, 0), (0, ss - a.shape[1]), (0, 0))).reshape(bs * ss, CONV_DIM)
    hist2 = pad(st)
    hist1 = pad(st[:, 1:])
    yconv_s, q_s, k_s, v_s, u_s = _project_sample(xs2, w_in_b, conv_w[l], hist1, hist2, ss)
    ck = cache_k.reshape(depth * bs, past_len * N_HEADS, HEAD_W)
    cv = cache_v.reshape(depth * bs, past_len * N_HEADS, V_DIM)
    attn_s = _attention_sample(q_s, k_s, v_s, ck, cv, lam_vecs, g_sub, lambda_init, ss)
    y_sample = finish(xs2, yconv_s, attn_s).reshape(bs, ss, D_MODEL)

    k_prompt = k_p.reshape(depth, bp, sp, N_HEADS, HEAD_W)
    v_prompt = v_p.reshape(depth, bp, sp, N_HEADS, V_DIM)
    conv_prompt = conv_p.reshape(depth, bp, CONV_W - 1, CONV_DIM)
    k_sample = k_s.reshape(depth, bs, ss, N_HEADS, HEAD_W)
    v_sample = v_s.reshape(depth, bs, ss, N_HEADS, V_DIM)
    conv_sample = u_s.reshape(bs, ss, CONV_DIM)[:, ss - (CONV_W - 1):].reshape(
        depth, bs, CONV_W - 1, CONV_DIM)
    return (y_prompt, y_sample, k_prompt, v_prompt, conv_prompt, k_sample, v_sample, conv_sample)
```

```python
from functools import partial

import jax
import jax.numpy as jnp
import numpy as np
from jax import lax
from jax.experimental import pallas as pl
from jax.experimental.pallas import tpu as pltpu

D_MODEL = 1024
CHUNK = 64
CONV_DIM = 512
ATTN_DIM = 512
QK_DIM = 64
N_HEADS = 4
HEAD_W = 2 * QK_DIM
V_DIM = 128
F32_SUBLANES = 8
BF16_SUBLANES = 16
V_EXT = V_DIM + BF16_SUBLANES
LOG2E = 1.4426950408889634
CONV_W = 3
D_FF = 4096
LN_EPS = 1e-5
DEPTH = 1
ALPHA = (2 * DEPTH) ** 0.25
IN_WIDTH = 3 * CONV_DIM + 3 * ATTN_DIM

V7X_MXU_DIM = 256
V7X_VMEM_BYTES = 64 * 1024 * 1024

ROW_TILE = 1024
FINISH_GROUP = 256
Q_TILE = 2 * V7X_MXU_DIM
K_TILE = 2 * V7X_MXU_DIM
PROJ_TILE = 1024
TRIP_PAIRS = (4, 2, 1)
FF_CHUNK = 1024

_F32 = jnp.float32
_BF16 = jnp.bfloat16


def _vmem_limit(nbytes):
    assert nbytes <= V7X_VMEM_BYTES - (4 << 20)
    return int(nbytes)


def _dot(a, b):
    return jnp.dot(a, b, preferred_element_type=_F32)


def _dot_nt(a, b):
    return lax.dot_general(a, b, (((1,), (1,)), ((), ())), preferred_element_type=_F32)


def _dot_tn(a, b):
    return lax.dot_general(a, b, (((0,), (0,)), ((), ())), preferred_element_type=_F32)


def _div_pow2(x, n):
    assert n & (n - 1) == 0
    return lax.shift_right_logical(x, n.bit_length() - 1)


def _mod_pow2(x, n):
    assert n & (n - 1) == 0
    return lax.bitwise_and(x, n - 1)


def _layer_norm(x, g, b):
    mu = jnp.mean(x, axis=-1, keepdims=True)
    xc = x - mu
    var = jnp.mean(xc * xc, axis=-1, keepdims=True)
    return xc * lax.rsqrt(var + LN_EPS) * g + b


def _lambda_value(lq1, lk1, lq2, lk2, lambda_init):
    s1 = jnp.sum(lq1 * lk1, axis=-1, keepdims=True)
    s2 = jnp.sum(lq2 * lk2, axis=-1, keepdims=True)
    return jnp.exp(s1) - jnp.exp(s2) + lambda_init


def _project_columns(x_ref, w_ref):
    xb = x_ref[...].astype(_BF16)
    return [_dot(xb, w_ref[:, g * CONV_DIM:(g + 1) * CONV_DIM]) for g in range(IN_WIDTH // CONV_DIM)]


def _conv_from_taps(u, u1, u2, cw_ref):
    return cw_ref[0:1, :] * u2 + cw_ref[1:2, :] * u1 + cw_ref[2:3, :] * u


def _proj_prompt_kernel(x_ref, w_ref, cw_ref, wo_ref, w1_ref, w2_ref,
                        yconv_ref, kf_ref, vf_ref, qa_ref, qb_ref, ka_ref, kb_ref, vt_ref, cstate_ref,
                        wo_b_ref, w1_b_ref, w2_b_ref, carry_ref):
    i = pl.program_id(0)

    wo_b_ref[...] = wo_ref[...].astype(_BF16)
    w1_b_ref[...] = w1_ref[...].astype(_BF16)
    w2_b_ref[...] = w2_ref[...].astype(_BF16)

    @pl.when(i == 0)
    def _():
        carry_ref[...] = jnp.zeros_like(carry_ref)

    gate, cc, hh, q, k, v = _project_columns(x_ref, w_ref)
    u = cc * hh
    rows = lax.broadcasted_iota(jnp.int32, u.shape, 0)
    prev2 = carry_ref[0:1, :]
    prev1 = carry_ref[1:2, :]
    u1 = jnp.where(rows == 0, prev1, pltpu.roll(u, 1, 0))
    u2 = jnp.where(rows == 0, prev2, jnp.where(rows == 1, prev1, pltpu.roll(u, 2, 0)))
    yconv_ref[...] = (gate * _conv_from_taps(u, u1, u2, cw_ref)).astype(_BF16)
    tail = u[PROJ_TILE - (CONV_W - 1):, :]
    carry_ref[0:CONV_W - 1, :] = tail
    cstate_ref[...] = tail

    for h in range(N_HEADS):
        sl = slice(h * HEAD_W, (h + 1) * HEAD_W)
        kf_ref[pl.ds(h, PROJ_TILE, stride=N_HEADS), :] = k[:, sl]
        vf_ref[pl.ds(h, PROJ_TILE, stride=N_HEADS), :] = v[:, sl]
    qs = q * (QK_DIM ** -0.5 * LOG2E)
    lane = lax.broadcasted_iota(jnp.int32, (PROJ_TILE, HEAD_W), 1)
    first = lane < QK_DIM
    ones_row = jnp.where(lax.broadcasted_iota(jnp.int32, (BF16_SUBLANES, K_TILE), 0) == 0,
                         1.0, 0.0).astype(_BF16)
    chunks_per_ktile = K_TILE // CHUNK
    assert QK_DIM // chunks_per_ktile >= 2
    row = lax.broadcasted_iota(jnp.int32, (PROJ_TILE, HEAD_W), 0)
    key_chunk = _div_pow2(_mod_pow2(row, K_TILE), CHUNK)
    lane_chunk = _mod_pow2(lane, chunks_per_ktile)
    chunk_onehot = jnp.where(lane_chunk == key_chunk, 1.0, 0.0)
    q_tile = _div_pow2(i * PROJ_TILE + row, Q_TILE)
    tile_offset = _mod_pow2(q_tile, K_TILE // Q_TILE) * (Q_TILE // CHUNK)
    query_chunk = _div_pow2(_mod_pow2(row, Q_TILE), CHUNK) + tile_offset
    mask_weight = jnp.where(lane_chunk > query_chunk, float(jnp.finfo(_BF16).min), 0.0)
    for h in range(N_HEADS):
        sl = slice(h * HEAD_W, (h + 1) * HEAD_W)
        qh = qs[:, sl]
        kh = k[:, sl]
        qa_ref[h, 0] = jnp.where(first, qh, 0.0).astype(_BF16)
        qb_ref[h, 0] = jnp.where(first, 0.0, qh).astype(_BF16)
        qa_ref[h, 1] = jnp.where(first, qh, mask_weight).astype(_BF16)
        qb_ref[h, 1] = jnp.where(first, mask_weight, qh).astype(_BF16)
        ka_ref[h] = jnp.where(first, kh, chunk_onehot).astype(_BF16)
        kb_ref[h] = jnp.where(first, chunk_onehot, kh).astype(_BF16)
        for s in range(PROJ_TILE // K_TILE):
            vt_ref[h, s, :V_DIM, :] = v[s * K_TILE:(s + 1) * K_TILE, sl].T.astype(_BF16)
            vt_ref[h, s, V_DIM:, :] = ones_row


def _proj_sample_kernel(x_ref, w_ref, cw_ref, h1_ref, h2_ref,
                        yconv_ref, q_ref, k_ref, v_ref, u_ref, *, period):
    gate, cc, hh, q, k, v = _project_columns(x_ref, w_ref)
    u = cc * hh
    t = _mod_pow2(lax.broadcasted_iota(jnp.int32, u.shape, 0), period)
    u1 = jnp.where(t < 1, h1_ref[...], pltpu.roll(u, 1, 0))
    u2 = jnp.where(t < 2, h2_ref[...], pltpu.roll(u, 2, 0))
    yconv_ref[...] = (gate * _conv_from_taps(u, u1, u2, cw_ref)).astype(_BF16)
    q_ref[...] = q * (QK_DIM ** -0.5)
    k_ref[...] = k
    v_ref[...] = v
    u_ref[...] = u


def _project_prompt(x, w_in, conv_w, later_weights):
    s = x.shape[0]
    assert s % PROJ_TILE == 0 and PROJ_TILE % K_TILE == 0
    n = s // PROJ_TILE
    row = lambda i: (i, 0)
    head_row = lambda i: (0, i, 0)
    for w in later_weights:
        assert w.shape[0] % (n * BF16_SUBLANES) == 0
    weight_specs = [pl.BlockSpec((w.shape[0] // n, w.shape[1]), row) for w in later_weights]
    out_shape = (
        jax.ShapeDtypeStruct((s, CONV_DIM), _BF16),
        jax.ShapeDtypeStruct((s * N_HEADS, HEAD_W), _F32),
        jax.ShapeDtypeStruct((s * N_HEADS, V_DIM), _F32),
        jax.ShapeDtypeStruct((N_HEADS, 2, s, HEAD_W), _BF16),
        jax.ShapeDtypeStruct((N_HEADS, 2, s, HEAD_W), _BF16),
        jax.ShapeDtypeStruct((N_HEADS, s, HEAD_W), _BF16),
        jax.ShapeDtypeStruct((N_HEADS, s, HEAD_W), _BF16),
        jax.ShapeDtypeStruct((N_HEADS, s // K_TILE, V_EXT, K_TILE), _BF16),
        jax.ShapeDtypeStruct((CONV_W - 1, CONV_DIM), _F32),
    ) + tuple(jax.ShapeDtypeStruct(w.shape, _BF16) for w in later_weights)
    out_specs = (
        pl.BlockSpec((PROJ_TILE, CONV_DIM), row),
        pl.BlockSpec((PROJ_TILE * N_HEADS, HEAD_W), row),
        pl.BlockSpec((PROJ_TILE * N_HEADS, V_DIM), row),
        pl.BlockSpec((N_HEADS, 2, PROJ_TILE, HEAD_W), lambda i: (0, 0, i, 0)),
        pl.BlockSpec((N_HEADS, 2, PROJ_TILE, HEAD_W), lambda i: (0, 0, i, 0)),
        pl.BlockSpec((N_HEADS, PROJ_TILE, HEAD_W), head_row),
        pl.BlockSpec((N_HEADS, PROJ_TILE, HEAD_W), head_row),
        pl.BlockSpec((N_HEADS, PROJ_TILE // K_TILE, V_EXT, K_TILE), lambda i: (0, i, 0, 0)),
        pl.BlockSpec((CONV_W - 1, CONV_DIM), lambda i: (0, 0)),
    ) + tuple(weight_specs)
    return pl.pallas_call(
        _proj_prompt_kernel,
        grid=(n,),
        in_specs=[
            pl.BlockSpec((PROJ_TILE, D_MODEL), row),
            pl.BlockSpec((D_MODEL, IN_WIDTH), lambda i: (0, 0)),
            pl.BlockSpec((CONV_W, CONV_DIM), lambda i: (0, 0)),
        ] + weight_specs,
        out_specs=out_specs,
        out_shape=out_shape,
        scratch_shapes=[pltpu.VMEM((8, CONV_DIM), _F32)],
        compiler_params=pltpu.CompilerParams(
            dimension_semantics=("arbitrary",), vmem_limit_bytes=_vmem_limit(56 << 20)),
        name="project_prompt",
    )(x, w_in, conv_w, *later_weights)


def _project_sample(x, w_in, conv_w, hist1, hist2, period):
    r = x.shape[0]
    full = lambda shape: pl.BlockSpec(shape, lambda i: (0,) * len(shape))
    wide = (r, CONV_DIM)
    return pl.pallas_call(
        partial(_proj_sample_kernel, period=period),
        grid=(1,),
        in_specs=[full((r, D_MODEL)), full((D_MODEL, IN_WIDTH)), full((CONV_W, CONV_DIM)),
                  full(wide), full(wide)],
        out_specs=(full(wide),) * 5,
        out_shape=(jax.ShapeDtypeStruct(wide, _BF16),) + (jax.ShapeDtypeStruct(wide, _F32),) * 4,
        compiler_params=pltpu.CompilerParams(
            dimension_semantics=("arbitrary",), vmem_limit_bytes=_vmem_limit(48 << 20)),
        name="project_sample",
    )(x, w_in, conv_w, hist1, hist2)


def _attn_prompt_kernel(qa_ref, qb_ref, ka_ref, kb_ref, vt_ref, lq1_ref, lk1_ref, lq2_ref, lk2_ref,
                        g_ref, o_ref, s_even, s_odd, cmax_even, cmax_odd, m_ref, acc_ref,
                        *, lambda_init):
    i = pl.program_id(1)
    q_refs = (qa_ref, qb_ref)
    k_refs = (ka_ref, kb_ref)
    bufs = ((s_even, cmax_even), (s_odd, cmax_odd))
    n_full = _div_pow2(i, K_TILE // Q_TILE)

    def col_max(st):
        part = jnp.max(st.reshape(K_TILE // F32_SUBLANES, F32_SUBLANES, Q_TILE), axis=0)
        return jnp.max(part, axis=0, keepdims=True)

    def scores(t, buf):
        s_ref, cmax_ref = buf
        rows = pl.ds(pl.multiple_of(t * K_TILE, K_TILE), K_TILE)
        variant = (t == n_full).astype(jnp.int32)
        for c in range(2):
            st = _dot_nt(k_refs[c][0, rows, :], q_refs[c][0, variant])
            s_ref[c] = st
            cmax_ref[c] = col_max(st)

    def consume(t, buf):
        s_ref, cmax_ref = buf
        vt = vt_ref[0, t]
        for c in range(2):
            m = m_ref[c]
            m_new = jnp.maximum(m, cmax_ref[c])
            alpha = jnp.exp2(m - m_new)
            p = jnp.exp2(s_ref[c] - m_new).astype(_BF16)
            acc_ref[c] = alpha * acc_ref[c] + _dot(vt, p)
            m_ref[c] = m_new

    m_ref[...] = jnp.full(m_ref.shape, -jnp.inf, _F32)
    acc_ref[...] = jnp.zeros(acc_ref.shape, _F32)
    scores(0, bufs[0])

    def pairs(t, count):
        for k in range(count):
            scores(t + 2 * k + 1, bufs[1])
            consume(t + 2 * k, bufs[0])
            scores(t + 2 * k + 2, bufs[0])
            consume(t + 2 * k + 1, bufs[1])

    n_pairs = _div_pow2(n_full, 2)
    pairs_done = 0
    for size in TRIP_PAIRS:
        n_trips = _div_pow2(n_pairs - pairs_done, size)

        def trip(p, carry, size=size, first=pairs_done):
            pairs(2 * (first + p * size), size)
            return carry

        lax.fori_loop(0, n_trips, trip, 0)
        pairs_done = pairs_done + n_trips * size

    t0 = 2 * n_pairs
    two_left = n_full > t0

    @pl.when(two_left)
    def _():
        scores(n_full, bufs[1])
        consume(t0, bufs[0])
        consume(n_full, bufs[1])

    @pl.when(jnp.logical_not(two_left))
    def _():
        consume(t0, bufs[0])

    lam =_lambda_value(lq1_ref[...], lk1_ref[...], lq2_ref[...], lk2_ref[...], lambda_init)
    acc0 = acc_ref[0]
    acc1 = acc_ref[1]
    l0 = acc0[V_DIM:V_DIM + 1, :]
    l1 = acc1[V_DIM:V_DIM + 1, :]
    o = acc0[:V_DIM] / l0 - lam * (acc1[:V_DIM] / l1)
    ms = jnp.mean(o * o, axis=0, keepdims=True)
    on = o * lax.rsqrt(ms + LN_EPS)
    on = on.T * g_ref[...] * (1.0 - lambda_init)
    o_ref[...] = on.astype(_BF16)


def _attention_prompt(qa, qb, ka, kb, vt, lam_vecs, subln_g, lambda_init):
    s = ka.shape[1]
    assert K_TILE % Q_TILE == 0 and s % K_TILE == 0 and Q_TILE % CHUNK == 0
    nq = s // Q_TILE
    vec = pl.BlockSpec((1, QK_DIM), lambda h, i: (0, 0))
    q_tile = pl.BlockSpec((1, 2, Q_TILE, HEAD_W), lambda h, i: (h, 0, i, 0))
    return pl.pallas_call(
        partial(_attn_prompt_kernel, lambda_init=lambda_init),
        grid=(N_HEADS, nq),
        in_specs=[
            q_tile, q_tile,
            pl.BlockSpec((1, s, HEAD_W), lambda h, i: (h, 0, 0)),
            pl.BlockSpec((1, s, HEAD_W), lambda h, i: (h, 0, 0)),
            pl.BlockSpec((1, s // K_TILE, V_EXT, K_TILE), lambda h, i: (h, 0, 0, 0)),
            vec, vec, vec, vec,
            pl.BlockSpec((1, V_DIM), lambda h, i: (0, 0)),
        ],
        out_specs=pl.BlockSpec((Q_TILE, V_DIM), lambda h, i: (i, h)),
        out_shape=jax.ShapeDtypeStruct((s, ATTN_DIM), _BF16),
        scratch_shapes=[
            pltpu.VMEM((2, K_TILE, Q_TILE), _F32), pltpu.VMEM((2, K_TILE, Q_TILE), _F32),
            pltpu.VMEM((2, 1, Q_TILE), _F32), pltpu.VMEM((2, 1, Q_TILE), _F32),
            pltpu.VMEM((2, 1, Q_TILE), _F32), pltpu.VMEM((2, V_EXT, Q_TILE), _F32),
        ],
        compiler_params=pltpu.CompilerParams(
            dimension_semantics=("arbitrary", "arbitrary"), vmem_limit_bytes=_vmem_limit(48 << 20)),
        name="attention_prompt",
    )(qa, qb, ka, kb, vt, *lam_vecs, subln_g)


def _attn_sample_kernel(q_ref, kn_ref, vn_ref, ck_ref, cv_ref, lq1_ref, lk1_ref, lq2_ref, lk2_ref,
                        g_ref, o_ref, *, lambda_init, past_len, n_new):
    group = 2 * N_HEADS
    width = group * n_new
    assert width == HEAD_W
    qb = q_ref[...].astype(_BF16)
    sel_r = lax.broadcasted_iota(jnp.int32, (n_new, width), 0)
    sel_c = lax.broadcasted_iota(jnp.int32, (n_new, width), 1)
    spread = jnp.where(_mod_pow2(sel_c, n_new) == sel_r, 1.0, 0.0).astype(_BF16)
    q_all = _dot_tn(qb, spread)
    blk_r = _div_pow2(lax.broadcasted_iota(jnp.int32, (ATTN_DIM, width), 0), QK_DIM)
    blk_c = _div_pow2(lax.broadcasted_iota(jnp.int32, (ATTN_DIM, width), 1), n_new)
    q_bd = jnp.where(blk_r == blk_c, q_all, 0.0).astype(_BF16)

    half = past_len // 2

    def heads_on_lanes(ref, part):
        return jnp.concatenate(
            [ref[0, pl.ds(part * half * N_HEADS + h, half, stride=N_HEADS), :].astype(_BF16)
             for h in range(N_HEADS)], axis=-1)

    s_old = jnp.concatenate([_dot(heads_on_lanes(ck_ref, part), q_bd) for part in range(2)],
                            axis=0)
    s_new = _dot(kn_ref[...].astype(_BF16), q_bd)

    def chunk_mask(shape, k_off):
        k_pos = k_off + lax.broadcasted_iota(jnp.int32, shape, 0)
        q_pos = past_len + _mod_pow2(lax.broadcasted_iota(jnp.int32, shape, 1), n_new)
        return _div_pow2(k_pos, CHUNK) <= _div_pow2(q_pos, CHUNK)

    s_old = jnp.where(chunk_mask(s_old.shape, 0), s_old, -jnp.inf)
    s_new = jnp.where(chunk_mask(s_new.shape, past_len), s_new, -jnp.inf)
    m = jnp.maximum(jnp.max(s_old, axis=0, keepdims=True), jnp.max(s_new, axis=0, keepdims=True))
    p_old = jnp.exp(s_old - m)
    p_new = jnp.exp(s_new - m)
    inv_l = 1.0 / (jnp.sum(p_old, axis=0, keepdims=True) + jnp.sum(p_new, axis=0, keepdims=True))
    a_old = (p_old * inv_l).astype(_BF16)
    a_new = (p_new * inv_l).astype(_BF16)
    pv = (_dot_tn(a_old[:half], heads_on_lanes(cv_ref, 0)) + _dot_tn(a_old[half:], heads_on_lanes(cv_ref, 1))
          + _dot_tn(a_new, vn_ref[...].astype(_BF16)))

    lam = _lambda_value(lq1_ref[...], lk1_ref[...], lq2_ref[...], lk2_ref[...], lambda_init)
    g = g_ref[...]
    for h in range(N_HEADS):
        cols = slice(h * V_DIM, (h + 1) * V_DIM)
        r0 = h * 2 * n_new
        o = pv[r0:r0 + n_new, cols] - lam * pv[r0 + n_new:r0 + 2 * n_new, cols]
        ms = jnp.mean(o * o, axis=-1, keepdims=True)
        o_ref[:, cols] = (o * lax.rsqrt(ms + LN_EPS) * g * (1.0 - lambda_init)).astype(_BF16)


def _attention_sample(q, k_new, v_new, cache_k, cache_v, lam_vecs, subln_g, lambda_init, n_new):
    nb, rows_per_stream = cache_k.shape[:2]
    past_len = rows_per_stream // N_HEADS
    cache_block = pl.BlockSpec((1, rows_per_stream, HEAD_W), lambda b: (b, 0, 0))
    rows = lambda b: (b, 0)
    vec = pl.BlockSpec((1, QK_DIM), lambda b: (0, 0))
    return pl.pallas_call(
        partial(_attn_sample_kernel, lambda_init=lambda_init, past_len=past_len, n_new=n_new),
        grid=(nb,),
        in_specs=[
            pl.BlockSpec((n_new, ATTN_DIM), rows),
            pl.BlockSpec((n_new, ATTN_DIM), rows),
            pl.BlockSpec((n_new, ATTN_DIM), rows),
            cache_block, cache_block,
            vec, vec, vec, vec,
            pl.BlockSpec((1, V_DIM), lambda b: (0, 0)),
        ],
        out_specs=pl.BlockSpec((n_new, ATTN_DIM), rows),
        out_shape=jax.ShapeDtypeStruct((nb * n_new, ATTN_DIM), _BF16),
        compiler_params=pltpu.CompilerParams(
            dimension_semantics=("arbitrary",), vmem_limit_bytes=_vmem_limit(40 << 20)),
        name="attention_sample",
    )(q, k_new, v_new, cache_k, cache_v, *lam_vecs, subln_g)


def _finish_kernel(x_ref, yconv_ref, attn_ref, wo_ref, g1_ref, b1_ref, w1_ref, w2_ref, g2_ref, b2_ref,
                   y_ref):
    tile = x_ref.shape[0]
    group = min(tile, FINISH_GROUP)
    groups = [slice(r0, r0 + group) for r0 in range(0, tile, group)]
    pre = [ALPHA * x_ref[rows, :]
           + _dot(jnp.concatenate([yconv_ref[rows, :], attn_ref[rows, :]], axis=-1), wo_ref[...])
           for rows in groups]
    for rows, z in zip(groups, pre):
        x1 = _layer_norm(z, g1_ref[...], b1_ref[...])
        x1b = x1.astype(_BF16)
        ff = jnp.zeros_like(x1)
        for c in range(D_FF // FF_CHUNK):
            cols = slice(c * FF_CHUNK, (c + 1) * FF_CHUNK)
            hdn = jnp.square(jnp.maximum(_dot(x1b, w1_ref[:, cols]), 0.0))
            ff = ff + _dot(hdn.astype(_BF16), w2_ref[cols, :])
        y_ref[rows, :] = _layer_norm(ALPHA * x1 + ff, g2_ref[...], b2_ref[...])


def _finish(x, yconv, attn, w_out, g1, b1, w_ff1, w_ff2, g2, b2):
    r = x.shape[0]
    tile = min(ROW_TILE, r)
    assert r % tile == 0
    row = lambda i: (i, 0)
    const = lambda shape: pl.BlockSpec(shape, lambda i: (0, 0))
    return pl.pallas_call(
        _finish_kernel,
        grid=(r // tile,),
        in_specs=[
            pl.BlockSpec((tile, D_MODEL), row),
            pl.BlockSpec((tile, CONV_DIM), row),
            pl.BlockSpec((tile, ATTN_DIM), row),
            const((D_MODEL, D_MODEL)), const((1, D_MODEL)), const((1, D_MODEL)),
            const((D_MODEL, D_FF)), const((D_FF, D_MODEL)), const((1, D_MODEL)), const((1, D_MODEL)),
        ],
        out_specs=pl.BlockSpec((tile, D_MODEL), row),
        out_shape=jax.ShapeDtypeStruct((r, D_MODEL), _F32),
        compiler_params=pltpu.CompilerParams(
            dimension_semantics=("arbitrary",), vmem_limit_bytes=_vmem_limit(56 << 20)),
        name="finish_layer",
    )(x, yconv, attn, w_out, g1, b1, w_ff1, w_ff2, g2, b2)


def kernel(x_prompt, x_sample, cache_k, cache_v, state_conv, w_in, conv_w, lambda_q1, lambda_k1,
           lambda_q2, lambda_k2, subln_g, w_out, ln1_g, ln1_b, w_ff1, w_ff2, ln2_g, ln2_b):
    bp, sp, _ = x_prompt.shape
    bs, ss, _ = x_sample.shape
    depth = w_in.shape[0]
    assert bp == 1 and depth == 1
    past_len = cache_k.shape[2]
    l = 0
    lambda_init = 0.8 - 0.6 * float(np.exp(-0.3 * l))

    w_in_b = w_in[l].astype(_BF16)
    lam_vecs = tuple(v[l].reshape(1, QK_DIM) for v in (lambda_q1, lambda_k1, lambda_q2, lambda_k2))
    g_sub = subln_g[l].reshape(1, V_DIM)
    ln = tuple(v[l].reshape(1, D_MODEL) for v in (ln1_g, ln1_b, ln2_g, ln2_b))

    xp2 = x_prompt.reshape(sp, D_MODEL)
    (yconv_p, k_p, v_p, qa, qb, ka, kb, vt, conv_p, w_out_b, w_ff1_b, w_ff2_b) = _project_prompt(
        xp2, w_in_b, conv_w[l], (w_out[l], w_ff1[l], w_ff2[l]))

    def finish(x2d, yconv, attn):
        return _finish(x2d, yconv, attn, w_out_b, ln[0], ln[1], w_ff1_b, w_ff2_b, ln[2], ln[3])

    attn_p = _attention_prompt(qa, qb, ka, kb, vt, lam_vecs, g_sub, lambda_init)
    y_prompt = finish(xp2, yconv_p, attn_p).reshape(bp, sp, D_MODEL)

    xs2 = x_sample.reshape(bs * ss, D_MODEL)
    st = state_conv[l].astype(_F32)
    pad = lambda a: jnp.pad(a, ((0, 0), (0, ss - a.shape[1]), (0, 0))).reshape(bs * ss, CONV_DIM)
    hist2 = pad(st)
    hist1 = pad(st[:, 1:])
    yconv_s, q_s, k_s, v_s, u_s = _project_sample(xs2, w_in_b, conv_w[l], hist1, hist2, ss)
    ck = cache_k.reshape(depth * bs, past_len * N_HEADS, HEAD_W)
    cv = cache_v.reshape(depth * bs, past_len * N_HEADS, V_DIM)
    attn_s = _attention_sample(q_s, k_s, v_s, ck, cv, lam_vecs, g_sub, lambda_init, ss)
    y_sample = finish(xs2, yconv_s, attn_s).reshape(bs, ss, D_MODEL)

    k_prompt = k_p.reshape(depth, bp, sp, N_HEADS, HEAD_W)
    v_prompt = v_p.reshape(depth, bp, sp, N_HEADS, V_DIM)
    conv_prompt = conv_p.reshape(depth, bp, CONV_W - 1, CONV_DIM)
    k_sample = k_s.reshape(depth, bs, ss, N_HEADS, HEAD_W)
    v_sample = v_s.reshape(depth, bs, ss, N_HEADS, V_DIM)
    conv_sample = u_s.reshape(bs, ss, CONV_DIM)[:, ss - (CONV_W - 1):].reshape(
        depth, bs, CONV_W - 1, CONV_DIM)
    return (y_prompt, y_sample, k_prompt, v_prompt, conv_prompt, k_sample, v_sample, conv_sample)
```

```python
from functools import partial

import jax
import jax.numpy as jnp
import numpy as np
from jax import lax
from jax.experimental import pallas as pl
from jax.experimental.pallas import tpu as pltpu

D_MODEL = 1024
CHUNK = 64
CONV_DIM = 512
ATTN_DIM = 512
QK_DIM = 64
N_HEADS = 4
HEAD_W = 2 * QK_DIM
V_DIM = 128
F32_SUBLANES = 8
BF16_SUBLANES = 16
V_EXT = V_DIM + BF16_SUBLANES
LOG2E = 1.4426950408889634
CONV_W = 3
D_FF = 4096
LN_EPS = 1e-5
DEPTH = 1
ALPHA = (2 * DEPTH) ** 0.25
IN_WIDTH = 3 * CONV_DIM + 3 * ATTN_DIM

V7X_MXU_DIM = 256
V7X_VMEM_BYTES = 64 * 1024 * 1024
PIPELINE_BUFFERS = 2

ROW_TILE = 1024
FINISH_GROUP = 256
Q_TILE = 2 * V7X_MXU_DIM
K_TILE = 2 * V7X_MXU_DIM
PROJ_TILE = 1024
TRIP_PAIRS = (4, 2, 1)
FF_CHUNK = 1024

_F32 = jnp.float32
_BF16 = jnp.bfloat16


def _nbytes(shape, dtype):
    return int(np.prod([1 if d is None else d for d in shape])) * jnp.dtype(dtype).itemsize


def _pallas(body, name, grid, in_specs, operands, out_specs, out_shape, scratch=(), resident=(),
            temporaries=0):
    out_specs_seq = out_specs if isinstance(out_specs, (tuple, list)) else (out_specs,)
    out_shape_seq = out_shape if isinstance(out_shape, (tuple, list)) else (out_shape,)
    limit = temporaries + sum(_nbytes(shape, dtype) for shape, dtype in scratch)
    for pos, (spec, arr) in enumerate(zip(in_specs, operands)):
        limit += (1 if pos in resident else PIPELINE_BUFFERS) * _nbytes(spec.block_shape, arr.dtype)
    for spec, arr in zip(out_specs_seq, out_shape_seq):
        limit += PIPELINE_BUFFERS * _nbytes(spec.block_shape, arr.dtype)
    assert limit <= V7X_VMEM_BYTES, (name, limit)
    return pl.pallas_call(
        body,
        grid=grid,
        in_specs=list(in_specs),
        out_specs=out_specs,
        out_shape=out_shape,
        scratch_shapes=[pltpu.VMEM(shape, dtype) for shape, dtype in scratch],
        compiler_params=pltpu.CompilerParams(
            dimension_semantics=("arbitrary",) * len(grid), vmem_limit_bytes=limit),
        name=name,
    )(*operands)


def _dot(a, b):
    return jnp.dot(a, b, preferred_element_type=_F32)


def _dot_nt(a, b):
    return lax.dot_general(a, b, (((1,), (1,)), ((), ())), preferred_element_type=_F32)


def _dot_tn(a, b):
    return lax.dot_general(a, b, (((0,), (0,)), ((), ())), preferred_element_type=_F32)


def _div_pow2(x, n):
    assert n & (n - 1) == 0
    return lax.shift_right_logical(x, n.bit_length() - 1)


def _mod_pow2(x, n):
    assert n & (n - 1) == 0
    return lax.bitwise_and(x, n - 1)


def _layer_norm(x, g, b):
    mu = jnp.mean(x, axis=-1, keepdims=True)
    xc = x - mu
    var = jnp.mean(xc * xc, axis=-1, keepdims=True)
    return xc * lax.rsqrt(var + LN_EPS) * g + b


def _lambda_value(lq1, lk1, lq2, lk2, lambda_init):
    s1 = jnp.sum(lq1 * lk1, axis=-1, keepdims=True)
    s2 = jnp.sum(lq2 * lk2, axis=-1, keepdims=True)
    return jnp.exp(s1) - jnp.exp(s2) + lambda_init


def _project_columns(x_ref, w_ref):
    xb = x_ref[...].astype(_BF16)
    return [_dot(xb, w_ref[:, g * CONV_DIM:(g + 1) * CONV_DIM]) for g in range(IN_WIDTH // CONV_DIM)]


def _conv_from_taps(u, u1, u2, cw_ref):
    return cw_ref[0:1, :] * u2 + cw_ref[1:2, :] * u1 + cw_ref[2:3, :] * u


def _proj_prompt_kernel(x_ref, w_ref, cw_ref, wo_ref, w1_ref, w2_ref,
                        yconv_ref, kf_ref, vf_ref, qa_ref, qb_ref, ka_ref, kb_ref, vt_ref, cstate_ref,
                        wo_b_ref, w1_b_ref, w2_b_ref, carry_ref):
    i = pl.program_id(0)

    wo_b_ref[...] = wo_ref[...].astype(_BF16)
    w1_b_ref[...] = w1_ref[...].astype(_BF16)
    w2_b_ref[...] = w2_ref[...].astype(_BF16)

    @pl.when(i == 0)
    def _():
        carry_ref[...] = jnp.zeros_like(carry_ref)

    gate, cc, hh, q, k, v = _project_columns(x_ref, w_ref)
    u = cc * hh
    rows = lax.broadcasted_iota(jnp.int32, u.shape, 0)
    prev2 = carry_ref[0:1, :]
    prev1 = carry_ref[1:2, :]
    u1 = jnp.where(rows == 0, prev1, pltpu.roll(u, 1, 0))
    u2 = jnp.where(rows == 0, prev2, jnp.where(rows == 1, prev1, pltpu.roll(u, 2, 0)))
    yconv_ref[...] = (gate * _conv_from_taps(u, u1, u2, cw_ref)).astype(_BF16)
    tail = u[PROJ_TILE - (CONV_W - 1):, :]
    carry_ref[0:CONV_W - 1, :] = tail
    cstate_ref[...] = tail

    for h in range(N_HEADS):
        sl = slice(h * HEAD_W, (h + 1) * HEAD_W)
        kf_ref[pl.ds(h, PROJ_TILE, stride=N_HEADS), :] = k[:, sl]
        vf_ref[pl.ds(h, PROJ_TILE, stride=N_HEADS), :] = v[:, sl]
    qs = q * (QK_DIM ** -0.5 * LOG2E)
    lane = lax.broadcasted_iota(jnp.int32, (PROJ_TILE, HEAD_W), 1)
    first = lane < QK_DIM
    ones_row = jnp.where(lax.broadcasted_iota(jnp.int32, (BF16_SUBLANES, K_TILE), 0) == 0,
                         1.0, 0.0).astype(_BF16)
    chunks_per_ktile = K_TILE // CHUNK
    assert QK_DIM // chunks_per_ktile >= 2
    row = lax.broadcasted_iota(jnp.int32, (PROJ_TILE, HEAD_W), 0)
    key_chunk = _div_pow2(_mod_pow2(row, K_TILE), CHUNK)
    lane_chunk = _mod_pow2(lane, chunks_per_ktile)
    chunk_onehot = jnp.where(lane_chunk == key_chunk, 1.0, 0.0)
    q_tile = _div_pow2(i * PROJ_TILE + row, Q_TILE)
    tile_offset = _mod_pow2(q_tile, K_TILE // Q_TILE) * (Q_TILE // CHUNK)
    query_chunk = _div_pow2(_mod_pow2(row, Q_TILE), CHUNK) + tile_offset
    mask_weight = jnp.where(lane_chunk > query_chunk, float(jnp.finfo(_BF16).min), 0.0)
    for h in range(N_HEADS):
        sl = slice(h * HEAD_W, (h + 1) * HEAD_W)
        qh = qs[:, sl]
        kh = k[:, sl]
        qa_ref[h, 0] = jnp.where(first, qh, 0.0).astype(_BF16)
        qb_ref[h, 0] = jnp.where(first, 0.0, qh).astype(_BF16)
        qa_ref[h, 1] = jnp.where(first, qh, mask_weight).astype(_BF16)
        qb_ref[h, 1] = jnp.where(first, mask_weight, qh).astype(_BF16)
        ka_ref[h] = jnp.where(first, kh, chunk_onehot).astype(_BF16)
        kb_ref[h] = jnp.where(first, chunk_onehot, kh).astype(_BF16)
        for s in range(PROJ_TILE // K_TILE):
            vt_ref[h, s, :V_DIM, :] = v[s * K_TILE:(s + 1) * K_TILE, sl].T.astype(_BF16)
            vt_ref[h, s, V_DIM:, :] = ones_row


def _proj_sample_kernel(x_ref, w_ref, cw_ref, h1_ref, h2_ref,
                        yconv_ref, q_ref, k_ref, v_ref, u_ref, *, period):
    gate, cc, hh, q, k, v = _project_columns(x_ref, w_ref)
    u = cc * hh
    t = _mod_pow2(lax.broadcasted_iota(jnp.int32, u.shape, 0), period)
    u1 = jnp.where(t < 1, h1_ref[...], pltpu.roll(u, 1, 0))
    u2 = jnp.where(t < 2, h2_ref[...], pltpu.roll(u, 2, 0))
    yconv_ref[...] = (gate * _conv_from_taps(u, u1, u2, cw_ref)).astype(_BF16)
    q_ref[...] = q * (QK_DIM ** -0.5)
    k_ref[...] = k
    v_ref[...] = v
    u_ref[...] = u


def _project_prompt(x, w_in, conv_w, later_weights):
    s = x.shape[0]
    assert s % PROJ_TILE == 0 and PROJ_TILE % K_TILE == 0
    n = s // PROJ_TILE
    row = lambda i: (i, 0)
    head_row = lambda i: (0, i, 0)
    for w in later_weights:
        assert w.shape[0] % (n * BF16_SUBLANES) == 0
    weight_specs = [pl.BlockSpec((w.shape[0] // n, w.shape[1]), row) for w in later_weights]
    out_shape = (
        jax.ShapeDtypeStruct((s, CONV_DIM), _BF16),
        jax.ShapeDtypeStruct((s * N_HEADS, HEAD_W), _F32),
        jax.ShapeDtypeStruct((s * N_HEADS, V_DIM), _F32),
        jax.ShapeDtypeStruct((N_HEADS, 2, s, HEAD_W), _BF16),
        jax.ShapeDtypeStruct((N_HEADS, 2, s, HEAD_W), _BF16),
        jax.ShapeDtypeStruct((N_HEADS, s, HEAD_W), _BF16),
        jax.ShapeDtypeStruct((N_HEADS, s, HEAD_W), _BF16),
        jax.ShapeDtypeStruct((N_HEADS, s // K_TILE, V_EXT, K_TILE), _BF16),
        jax.ShapeDtypeStruct((CONV_W - 1, CONV_DIM), _F32),
    ) + tuple(jax.ShapeDtypeStruct(w.shape, _BF16) for w in later_weights)
    out_specs = (
        pl.BlockSpec((PROJ_TILE, CONV_DIM), row),
        pl.BlockSpec((PROJ_TILE * N_HEADS, HEAD_W), row),
        pl.BlockSpec((PROJ_TILE * N_HEADS, V_DIM), row),
        pl.BlockSpec((N_HEADS, 2, PROJ_TILE, HEAD_W), lambda i: (0, 0, i, 0)),
        pl.BlockSpec((N_HEADS, 2, PROJ_TILE, HEAD_W), lambda i: (0, 0, i, 0)),
        pl.BlockSpec((N_HEADS, PROJ_TILE, HEAD_W), head_row),
        pl.BlockSpec((N_HEADS, PROJ_TILE, HEAD_W), head_row),
        pl.BlockSpec((N_HEADS, PROJ_TILE // K_TILE, V_EXT, K_TILE), lambda i: (0, i, 0, 0)),
        pl.BlockSpec((CONV_W - 1, CONV_DIM), lambda i: (0, 0)),
    ) + tuple(weight_specs)
    in_specs = [
        pl.BlockSpec((PROJ_TILE, D_MODEL), row),
        pl.BlockSpec((D_MODEL, IN_WIDTH), lambda i: (0, 0)),
        pl.BlockSpec((CONV_W, CONV_DIM), lambda i: (0, 0)),
    ] + weight_specs
    return _pallas(
        _proj_prompt_kernel, "project_prompt", (n,), in_specs, (x, w_in, conv_w) + tuple(later_weights),
        out_specs, out_shape, scratch=[((F32_SUBLANES, CONV_DIM), _F32)], resident=(1, 2),
        temporaries=3 * _nbytes((PROJ_TILE, CONV_DIM), _F32))


def _project_sample(x, w_in, conv_w, hist1, hist2, period):
    r = x.shape[0]
    full = lambda shape: pl.BlockSpec(shape, lambda i: (0,) * len(shape))
    wide = (r, CONV_DIM)
    in_specs = [full((r, D_MODEL)), full((D_MODEL, IN_WIDTH)), full((CONV_W, CONV_DIM)),
                full(wide), full(wide)]
    out_shape = (jax.ShapeDtypeStruct(wide, _BF16),) + (jax.ShapeDtypeStruct(wide, _F32),) * 4
    return _pallas(
        partial(_proj_sample_kernel, period=period), "project_sample", (1,), in_specs,
        (x, w_in, conv_w, hist1, hist2), (full(wide),) * 5, out_shape,
        resident=tuple(range(len(in_specs))),
        temporaries=(IN_WIDTH // CONV_DIM) * _nbytes(wide, _F32))


def _attn_prompt_kernel(qa_ref, qb_ref, ka_ref, kb_ref, vt_ref, lq1_ref, lk1_ref, lq2_ref, lk2_ref,
                        g_ref, o_ref, s_even, s_odd, cmax_even, cmax_odd, m_ref, acc_ref,
                        *, lambda_init):
    i = pl.program_id(1)
    q_refs = (qa_ref, qb_ref)
    k_refs = (ka_ref, kb_ref)
    bufs = ((s_even, cmax_even), (s_odd, cmax_odd))
    n_full = _div_pow2(i, K_TILE // Q_TILE)

    def col_max(st):
        part = jnp.max(st.reshape(K_TILE // F32_SUBLANES, F32_SUBLANES, Q_TILE), axis=0)
        return jnp.max(part, axis=0, keepdims=True)

    def scores(t, buf):
        s_ref, cmax_ref = buf
        rows = pl.ds(pl.multiple_of(t * K_TILE, K_TILE), K_TILE)
        variant = (t == n_full).astype(jnp.int32)
        for c in range(2):
            st = _dot_nt(k_refs[c][0, rows, :], q_refs[c][0, variant])
            s_ref[c] = st
            cmax_ref[c] = col_max(st)

    def consume(t, buf):
        s_ref, cmax_ref = buf
        vt = vt_ref[0, t]
        for c in range(2):
            m = m_ref[c]
            m_new = jnp.maximum(m, cmax_ref[c])
            alpha = jnp.exp2(m - m_new)
            p = jnp.exp2(s_ref[c] - m_new).astype(_BF16)
            acc_ref[c] = alpha * acc_ref[c] + _dot(vt, p)
            m_ref[c] = m_new

    m_ref[...] = jnp.full(m_ref.shape, -jnp.inf, _F32)
    acc_ref[...] = jnp.zeros(acc_ref.shape, _F32)
    scores(0, bufs[0])

    def pairs(t, count):
        for k in range(count):
            scores(t + 2 * k + 1, bufs[1])
            consume(t + 2 * k, bufs[0])
            scores(t + 2 * k + 2, bufs[0])
            consume(t + 2 * k + 1, bufs[1])

    n_pairs = _div_pow2(n_full, 2)
    pairs_done = 0
    for size in TRIP_PAIRS:
        n_trips = _div_pow2(n_pairs - pairs_done, size)

        def trip(p, carry, size=size, first=pairs_done):
            pairs(2 * (first + p * size), size)
            return carry

        lax.fori_loop(0, n_trips, trip, 0)
        pairs_done = pairs_done + n_trips * size

    t0 = 2 * n_pairs
    two_left = n_full > t0

    @pl.when(two_left)
    def _():
        scores(n_full, bufs[1])
        consume(t0, bufs[0])
        consume(n_full, bufs[1])

    @pl.when(jnp.logical_not(two_left))
    def _():
        consume(t0, bufs[0])

    lam =_lambda_value(lq1_ref[...], lk1_ref[...], lq2_ref[...], lk2_ref[...], lambda_init)
    acc0 = acc_ref[0]
    acc1 = acc_ref[1]
    l0 = acc0[V_DIM:V_DIM + 1, :]
    l1 = acc1[V_DIM:V_DIM + 1, :]
    o = acc0[:V_DIM] / l0 - lam * (acc1[:V_DIM] / l1)
    ms = jnp.mean(o * o, axis=0, keepdims=True)
    on = o * lax.rsqrt(ms + LN_EPS)
    on = on.T * g_ref[...] * (1.0 - lambda_init)
    o_ref[...] = on.astype(_BF16)


def _attention_prompt(qa, qb, ka, kb, vt, lam_vecs, subln_g, lambda_init):
    s = ka.shape[1]
    assert K_TILE % Q_TILE == 0 and s % K_TILE == 0 and Q_TILE % CHUNK == 0
    nq = s // Q_TILE
    vec = pl.BlockSpec((1, QK_DIM), lambda h, i: (0, 0))
    q_tile = pl.BlockSpec((1, 2, Q_TILE, HEAD_W), lambda h, i: (h, 0, i, 0))
    in_specs = [
        q_tile, q_tile,
        pl.BlockSpec((1, s, HEAD_W), lambda h, i: (h, 0, 0)),
        pl.BlockSpec((1, s, HEAD_W), lambda h, i: (h, 0, 0)),
        pl.BlockSpec((1, s // K_TILE, V_EXT, K_TILE), lambda h, i: (h, 0, 0, 0)),
        vec, vec, vec, vec,
        pl.BlockSpec((1, V_DIM), lambda h, i: (0, 0)),
    ]
    logits = ((2, K_TILE, Q_TILE), _F32)
    column = ((2, F32_SUBLANES, Q_TILE), _F32)
    scratch = [logits, logits, ((2, 1, Q_TILE), _F32), ((2, 1, Q_TILE), _F32), ((2, 1, Q_TILE), _F32),
               ((2, V_EXT, Q_TILE), _F32)]
    return _pallas(
        partial(_attn_prompt_kernel, lambda_init=lambda_init), "attention_prompt", (N_HEADS, nq),
        in_specs, (qa, qb, ka, kb, vt) + tuple(lam_vecs) + (subln_g,),
        pl.BlockSpec((Q_TILE, V_DIM), lambda h, i: (i, h)), jax.ShapeDtypeStruct((s, ATTN_DIM), _BF16),
        scratch=scratch, resident=(5, 6, 7, 8, 9),
        temporaries=2 * _nbytes(*logits) + 3 * _nbytes(*column))


def _attn_sample_kernel(q_ref, kn_ref, vn_ref, ck_ref, cv_ref, lq1_ref, lk1_ref, lq2_ref, lk2_ref,
                        g_ref, o_ref, *, lambda_init, past_len, n_new):
    group = 2 * N_HEADS
    width = group * n_new
    assert width == HEAD_W
    qb = q_ref[...].astype(_BF16)
    sel_r = lax.broadcasted_iota(jnp.int32, (n_new, width), 0)
    sel_c = lax.broadcasted_iota(jnp.int32, (n_new, width), 1)
    spread = jnp.where(_mod_pow2(sel_c, n_new) == sel_r, 1.0, 0.0).astype(_BF16)
    q_all = _dot_tn(qb, spread)
    blk_r = _div_pow2(lax.broadcasted_iota(jnp.int32, (ATTN_DIM, width), 0), QK_DIM)
    blk_c = _div_pow2(lax.broadcasted_iota(jnp.int32, (ATTN_DIM, width), 1), n_new)
    q_bd = jnp.where(blk_r == blk_c, q_all, 0.0).astype(_BF16)

    half = past_len // 2

    def heads_on_lanes(ref, part):
        return jnp.concatenate(
            [ref[0, pl.ds(part * half * N_HEADS + h, half, stride=N_HEADS), :].astype(_BF16)
             for h in range(N_HEADS)], axis=-1)

    s_old = jnp.concatenate([_dot(heads_on_lanes(ck_ref, part), q_bd) for part in range(2)],
                            axis=0)
    s_new = _dot(kn_ref[...].astype(_BF16), q_bd)

    def chunk_mask(shape, k_off):
        k_pos = k_off + lax.broadcasted_iota(jnp.int32, shape, 0)
        q_pos = past_len + _mod_pow2(lax.broadcasted_iota(jnp.int32, shape, 1), n_new)
        return _div_pow2(k_pos, CHUNK) <= _div_pow2(q_pos, CHUNK)

    s_old = jnp.where(chunk_mask(s_old.shape, 0), s_old, -jnp.inf)
    s_new = jnp.where(chunk_mask(s_new.shape, past_len), s_new, -jnp.inf)
    m = jnp.maximum(jnp.max(s_old, axis=0, keepdims=True), jnp.max(s_new, axis=0, keepdims=True))
    p_old = jnp.exp(s_old - m)
    p_new = jnp.exp(s_new - m)
    inv_l = 1.0 / (jnp.sum(p_old, axis=0, keepdims=True) + jnp.sum(p_new, axis=0, keepdims=True))
    a_old = (p_old * inv_l).astype(_BF16)
    a_new = (p_new * inv_l).astype(_BF16)
    pv = (_dot_tn(a_old[:half], heads_on_lanes(cv_ref, 0)) + _dot_tn(a_old[half:], heads_on_lanes(cv_ref, 1))
          + _dot_tn(a_new, vn_ref[...].astype(_BF16)))

    lam = _lambda_value(lq1_ref[...], lk1_ref[...], lq2_ref[...], lk2_ref[...], lambda_init)
    g = g_ref[...]
    for h in range(N_HEADS):
        cols = slice(h * V_DIM, (h + 1) * V_DIM)
        r0 = h * 2 * n_new
        o = pv[r0:r0 + n_new, cols] - lam * pv[r0 + n_new:r0 + 2 * n_new, cols]
        ms = jnp.mean(o * o, axis=-1, keepdims=True)
        o_ref[:, cols] = (o * lax.rsqrt(ms + LN_EPS) * g * (1.0 - lambda_init)).astype(_BF16)


def _attention_sample(q, k_new, v_new, cache_k, cache_v, lam_vecs, subln_g, lambda_init, n_new):
    nb, rows_per_stream = cache_k.shape[:2]
    past_len = rows_per_stream // N_HEADS
    cache_block = pl.BlockSpec((1, rows_per_stream, HEAD_W), lambda b: (b, 0, 0))
    rows = lambda b: (b, 0)
    vec = pl.BlockSpec((1, QK_DIM), lambda b: (0, 0))
    in_specs = [
        pl.BlockSpec((n_new, ATTN_DIM), rows),
        pl.BlockSpec((n_new, ATTN_DIM), rows),
        pl.BlockSpec((n_new, ATTN_DIM), rows),
        cache_block, cache_block,
        vec, vec, vec, vec,
        pl.BlockSpec((1, V_DIM), lambda b: (0, 0)),
    ]
    return _pallas(
        partial(_attn_sample_kernel, lambda_init=lambda_init, past_len=past_len, n_new=n_new),
        "attention_sample", (nb,), in_specs,
        (q, k_new, v_new, cache_k, cache_v) + tuple(lam_vecs) + (subln_g,),
        pl.BlockSpec((n_new, ATTN_DIM), rows), jax.ShapeDtypeStruct((nb * n_new, ATTN_DIM), _BF16),
        resident=(5, 6, 7, 8, 9),
        temporaries=2 * _nbytes((past_len, ATTN_DIM), _BF16) + 2 * _nbytes((past_len, HEAD_W), _F32))


def _finish_kernel(x_ref, yconv_ref, attn_ref, wo_ref, g1_ref, b1_ref, w1_ref, w2_ref, g2_ref, b2_ref,
                   y_ref):
    tile = x_ref.shape[0]
    group = min(tile, FINISH_GROUP)
    groups = [slice(r0, r0 + group) for r0 in range(0, tile, group)]
    pre = [ALPHA * x_ref[rows, :]
           + _dot(jnp.concatenate([yconv_ref[rows, :], attn_ref[rows, :]], axis=-1), wo_ref[...])
           for rows in groups]
    for rows, z in zip(groups, pre):
        x1 = _layer_norm(z, g1_ref[...], b1_ref[...])
        x1b = x1.astype(_BF16)
        ff = jnp.zeros_like(x1)
        for c in range(D_FF // FF_CHUNK):
            cols = slice(c * FF_CHUNK, (c + 1) * FF_CHUNK)
            hdn = jnp.square(jnp.maximum(_dot(x1b, w1_ref[:, cols]), 0.0))
            ff = ff + _dot(hdn.astype(_BF16), w2_ref[cols, :])
        y_ref[rows, :] = _layer_norm(ALPHA * x1 + ff, g2_ref[...], b2_ref[...])


def _finish(x, yconv, attn, w_out, g1, b1, w_ff1, w_ff2, g2, b2):
    r = x.shape[0]
    tile = min(ROW_TILE, r)
    assert r % tile == 0
    row = lambda i: (i, 0)
    const = lambda shape: pl.BlockSpec(shape, lambda i: (0, 0))
    in_specs = [
        pl.BlockSpec((tile, D_MODEL), row),
        pl.BlockSpec((tile, CONV_DIM), row),
        pl.BlockSpec((tile, ATTN_DIM), row),
        const((D_MODEL, D_MODEL)), const((1, D_MODEL)), const((1, D_MODEL)),
        const((D_MODEL, D_FF)), const((D_FF, D_MODEL)), const((1, D_MODEL)), const((1, D_MODEL)),
    ]
    return _pallas(
        _finish_kernel, "finish_layer", (r // tile,), in_specs,
        (x, yconv, attn, w_out, g1, b1, w_ff1, w_ff2, g2, b2),
        pl.BlockSpec((tile, D_MODEL), row), jax.ShapeDtypeStruct((r, D_MODEL), _F32),
        resident=tuple(range(3, len(in_specs))),
        temporaries=3 * _nbytes((tile, D_MODEL), _F32) + _nbytes((tile, FF_CHUNK), _F32))


def kernel(x_prompt, x_sample, cache_k, cache_v, state_conv, w_in, conv_w, lambda_q1, lambda_k1,
           lambda_q2, lambda_k2, subln_g, w_out, ln1_g, ln1_b, w_ff1, w_ff2, ln2_g, ln2_b):
    bp, sp, _ = x_prompt.shape
    bs, ss, _ = x_sample.shape
    depth = w_in.shape[0]
    assert bp == 1 and depth == 1
    past_len = cache_k.shape[2]
    l = 0
    lambda_init = 0.8 - 0.6 * float(np.exp(-0.3 * l))

    w_in_b = w_in[l].astype(_BF16)
    lam_vecs = tuple(v[l].reshape(1, QK_DIM) for v in (lambda_q1, lambda_k1, lambda_q2, lambda_k2))
    g_sub = subln_g[l].reshape(1, V_DIM)
    ln = tuple(v[l].reshape(1, D_MODEL) for v in (ln1_g, ln1_b, ln2_g, ln2_b))

    xp2 = x_prompt.reshape(sp, D_MODEL)
    (yconv_p, k_p, v_p, qa, qb, ka, kb, vt, conv_p, w_out_b, w_ff1_b, w_ff2_b) = _project_prompt(
        xp2, w_in_b, conv_w[l], (w_out[l], w_ff1[l], w_ff2[l]))

    def finish(x2d, yconv, attn):
        return _finish(x2d, yconv, attn, w_out_b, ln[0], ln[1], w_ff1_b, w_ff2_b, ln[2], ln[3])

    attn_p = _attention_prompt(qa, qb, ka, kb, vt, lam_vecs, g_sub, lambda_init)
    y_prompt = finish(xp2, yconv_p, attn_p).reshape(bp, sp, D_MODEL)

    xs2 = x_sample.reshape(bs * ss, D_MODEL)
    st = state_conv[l].astype(_F32)
    pad = lambda a: jnp.pad(a, ((0, 0), (0, ss - a.shape[1]), (0, 0))).reshape(bs * ss, CONV_DIM)
    hist2 = pad(st)
    hist1 = pad(st[:, 1:])
    yconv_s, q_s, k_s, v_s, u_s = _project_sample(xs2, w_in_b, conv_w[l], hist1, hist2, ss)
    ck = cache_k.reshape(depth * bs, past_len * N_HEADS, HEAD_W)
    cv = cache_v.reshape(depth * bs, past_len * N_HEADS, V_DIM)
    attn_s = _attention_sample(q_s, k_s, v_s, ck, cv, lam_vecs, g_sub, lambda_init, ss)
    y_sample = finish(xs2, yconv_s, attn_s).reshape(bs, ss, D_MODEL)

    k_prompt = k_p.reshape(depth, bp, sp, N_HEADS, HEAD_W)
    v_prompt = v_p.reshape(depth, bp, sp, N_HEADS, V_DIM)
    conv_prompt = conv_p.reshape(depth, bp, CONV_W - 1, CONV_DIM)
    k_sample = k_s.reshape(depth, bs, ss, N_HEADS, HEAD_W)
    v_sample = v_s.reshape(depth, bs, ss, N_HEADS, V_DIM)
    conv_sample = u_s.reshape(bs, ss, CONV_DIM)[:, ss - (CONV_W - 1):].reshape(
        depth, bs, CONV_W - 1, CONV_DIM)
    return (y_prompt, y_sample, k_prompt, v_prompt, conv_prompt, k_sample, v_sample, conv_sample)
```

```python
from functools import partial

import jax
import jax.numpy as jnp
import numpy as np
from jax import lax
from jax.experimental import pallas as pl
from jax.experimental.pallas import tpu as pltpu

D_MODEL = 1024
CHUNK = 64
CONV_DIM = 512
ATTN_DIM = 512
QK_DIM = 64
N_HEADS = 4
HEAD_W = 2 * QK_DIM
V_DIM = 128
F32_SUBLANES = 8
BF16_SUBLANES = 16
V_EXT = V_DIM + BF16_SUBLANES
LOG2E = 1.4426950408889634
CONV_W = 3
D_FF = 4096
LN_EPS = 1e-5
DEPTH = 1
ALPHA = (2 * DEPTH) ** 0.25
IN_WIDTH = 3 * CONV_DIM + 3 * ATTN_DIM

V7X_MXU_DIM = 256
V7X_VMEM_BYTES = 64 * 1024 * 1024
PIPELINE_BUFFERS = 2
ATTENTION_VMEM_REQUEST = 3 * V7X_VMEM_BYTES // 4

ROW_TILE = 1024
FINISH_GROUP = 256
Q_TILE = 2 * V7X_MXU_DIM
K_TILE = 2 * V7X_MXU_DIM
PROJ_TILE = 1024
TRIP_PAIRS = (4, 2, 1)
FF_CHUNK = 1024

_F32 = jnp.float32
_BF16 = jnp.bfloat16


def _nbytes(shape, dtype):
    return int(np.prod([1 if d is None else d for d in shape])) * jnp.dtype(dtype).itemsize


def _pallas(body, name, grid, in_specs, operands, out_specs, out_shape, scratch=(), resident=(),
            temporaries=0, exclusive=0):
    out_specs_seq = out_specs if isinstance(out_specs, (tuple, list)) else (out_specs,)
    out_shape_seq = out_shape if isinstance(out_shape, (tuple, list)) else (out_shape,)
    limit = temporaries + sum(_nbytes(shape, dtype) for shape, dtype in scratch)
    for pos, (spec, arr) in enumerate(zip(in_specs, operands)):
        limit += (1 if pos in resident else PIPELINE_BUFFERS) * _nbytes(spec.block_shape, arr.dtype)
    for spec, arr in zip(out_specs_seq, out_shape_seq):
        limit += PIPELINE_BUFFERS * _nbytes(spec.block_shape, arr.dtype)
    limit = max(limit, exclusive)
    assert limit <= V7X_VMEM_BYTES, (name, limit)
    return pl.pallas_call(
        body,
        grid=grid,
        in_specs=list(in_specs),
        out_specs=out_specs,
        out_shape=out_shape,
        scratch_shapes=[pltpu.VMEM(shape, dtype) for shape, dtype in scratch],
        compiler_params=pltpu.CompilerParams(
            dimension_semantics=("arbitrary",) * len(grid), vmem_limit_bytes=limit),
        name=name,
    )(*operands)


def _dot(a, b):
    return jnp.dot(a, b, preferred_element_type=_F32)


def _dot_nt(a, b):
    return lax.dot_general(a, b, (((1,), (1,)), ((), ())), preferred_element_type=_F32)


def _dot_tn(a, b):
    return lax.dot_general(a, b, (((0,), (0,)), ((), ())), preferred_element_type=_F32)


def _div_pow2(x, n):
    assert n & (n - 1) == 0
    return lax.shift_right_logical(x, n.bit_length() - 1)


def _mod_pow2(x, n):
    assert n & (n - 1) == 0
    return lax.bitwise_and(x, n - 1)


def _layer_norm(x, g, b):
    mu = jnp.mean(x, axis=-1, keepdims=True)
    xc = x - mu
    var = jnp.mean(xc * xc, axis=-1, keepdims=True)
    return xc * lax.rsqrt(var + LN_EPS) * g + b


def _lambda_value(lq1, lk1, lq2, lk2, lambda_init):
    s1 = jnp.sum(lq1 * lk1, axis=-1, keepdims=True)
    s2 = jnp.sum(lq2 * lk2, axis=-1, keepdims=True)
    return jnp.exp(s1) - jnp.exp(s2) + lambda_init


def _project_columns(x_ref, w_ref):
    xb = x_ref[...].astype(_BF16)
    return [_dot(xb, w_ref[:, g * CONV_DIM:(g + 1) * CONV_DIM]) for g in range(IN_WIDTH // CONV_DIM)]


def _conv_from_taps(u, u1, u2, cw_ref):
    return cw_ref[0:1, :] * u2 + cw_ref[1:2, :] * u1 + cw_ref[2:3, :] * u


def _proj_prompt_kernel(x_ref, w_ref, cw_ref, wo_ref, w1_ref, w2_ref,
                        yconv_ref, kf_ref, vf_ref, qa_ref, qb_ref, ka_ref, kb_ref, vt_ref, cstate_ref,
                        wo_b_ref, w1_b_ref, w2_b_ref, carry_ref):
    i = pl.program_id(0)

    wo_b_ref[...] = wo_ref[...].astype(_BF16)
    w1_b_ref[...] = w1_ref[...].astype(_BF16)
    w2_b_ref[...] = w2_ref[...].astype(_BF16)

    @pl.when(i == 0)
    def _():
        carry_ref[...] = jnp.zeros_like(carry_ref)

    gate, cc, hh, q, k, v = _project_columns(x_ref, w_ref)
    u = cc * hh
    rows = lax.broadcasted_iota(jnp.int32, u.shape, 0)
    prev2 = carry_ref[0:1, :]
    prev1 = carry_ref[1:2, :]
    u1 = jnp.where(rows == 0, prev1, pltpu.roll(u, 1, 0))
    u2 = jnp.where(rows == 0, prev2, jnp.where(rows == 1, prev1, pltpu.roll(u, 2, 0)))
    yconv_ref[...] = (gate * _conv_from_taps(u, u1, u2, cw_ref)).astype(_BF16)
    tail = u[PROJ_TILE - (CONV_W - 1):, :]
    carry_ref[0:CONV_W - 1, :] = tail
    cstate_ref[...] = tail

    for h in range(N_HEADS):
        sl = slice(h * HEAD_W, (h + 1) * HEAD_W)
        kf_ref[pl.ds(h, PROJ_TILE, stride=N_HEADS), :] = k[:, sl]
        vf_ref[pl.ds(h, PROJ_TILE, stride=N_HEADS), :] = v[:, sl]
    qs = q * (QK_DIM ** -0.5 * LOG2E)
    lane = lax.broadcasted_iota(jnp.int32, (PROJ_TILE, HEAD_W), 1)
    first = lane < QK_DIM
    ones_row = jnp.where(lax.broadcasted_iota(jnp.int32, (BF16_SUBLANES, K_TILE), 0) == 0,
                         1.0, 0.0).astype(_BF16)
    chunks_per_ktile = K_TILE // CHUNK
    assert QK_DIM // chunks_per_ktile >= 2
    row = lax.broadcasted_iota(jnp.int32, (PROJ_TILE, HEAD_W), 0)
    key_chunk = _div_pow2(_mod_pow2(row, K_TILE), CHUNK)
    lane_chunk = _mod_pow2(lane, chunks_per_ktile)
    chunk_onehot = jnp.where(lane_chunk == key_chunk, 1.0, 0.0)
    q_tile = _div_pow2(i * PROJ_TILE + row, Q_TILE)
    tile_offset = _mod_pow2(q_tile, K_TILE // Q_TILE) * (Q_TILE // CHUNK)
    query_chunk = _div_pow2(_mod_pow2(row, Q_TILE), CHUNK) + tile_offset
    mask_weight = jnp.where(lane_chunk > query_chunk, float(jnp.finfo(_BF16).min), 0.0)
    for h in range(N_HEADS):
        sl = slice(h * HEAD_W, (h + 1) * HEAD_W)
        qh = qs[:, sl]
        kh = k[:, sl]
        qa_ref[h, 0] = jnp.where(first, qh, 0.0).astype(_BF16)
        qb_ref[h, 0] = jnp.where(first, 0.0, qh).astype(_BF16)
        qa_ref[h, 1] = jnp.where(first, qh, mask_weight).astype(_BF16)
        qb_ref[h, 1] = jnp.where(first, mask_weight, qh).astype(_BF16)
        ka_ref[h] = jnp.where(first, kh, chunk_onehot).astype(_BF16)
        kb_ref[h] = jnp.where(first, chunk_onehot, kh).astype(_BF16)
        for s in range(PROJ_TILE // K_TILE):
            vt_ref[h, s, :V_DIM, :] = v[s * K_TILE:(s + 1) * K_TILE, sl].T.astype(_BF16)
            vt_ref[h, s, V_DIM:, :] = ones_row


def _proj_sample_kernel(x_ref, w_ref, cw_ref, h1_ref, h2_ref,
                        yconv_ref, q_ref, k_ref, v_ref, u_ref, *, period):
    gate, cc, hh, q, k, v = _project_columns(x_ref, w_ref)
    u = cc * hh
    t = _mod_pow2(lax.broadcasted_iota(jnp.int32, u.shape, 0), period)
    u1 = jnp.where(t < 1, h1_ref[...], pltpu.roll(u, 1, 0))
    u2 = jnp.where(t < 2, h2_ref[...], pltpu.roll(u, 2, 0))
    yconv_ref[...] = (gate * _conv_from_taps(u, u1, u2, cw_ref)).astype(_BF16)
    q_ref[...] = q * (QK_DIM ** -0.5)
    k_ref[...] = k
    v_ref[...] = v
    u_ref[...] = u


def _project_prompt(x, w_in, conv_w, later_weights):
    s = x.shape[0]
    assert s % PROJ_TILE == 0 and PROJ_TILE % K_TILE == 0
    n = s // PROJ_TILE
    row = lambda i: (i, 0)
    head_row = lambda i: (0, i, 0)
    for w in later_weights:
        assert w.shape[0] % (n * BF16_SUBLANES) == 0
    weight_specs = [pl.BlockSpec((w.shape[0] // n, w.shape[1]), row) for w in later_weights]
    out_shape = (
        jax.ShapeDtypeStruct((s, CONV_DIM), _BF16),
        jax.ShapeDtypeStruct((s * N_HEADS, HEAD_W), _F32),
        jax.ShapeDtypeStruct((s * N_HEADS, V_DIM), _F32),
        jax.ShapeDtypeStruct((N_HEADS, 2, s, HEAD_W), _BF16),
        jax.ShapeDtypeStruct((N_HEADS, 2, s, HEAD_W), _BF16),
        jax.ShapeDtypeStruct((N_HEADS, s, HEAD_W), _BF16),
        jax.ShapeDtypeStruct((N_HEADS, s, HEAD_W), _BF16),
        jax.ShapeDtypeStruct((N_HEADS, s // K_TILE, V_EXT, K_TILE), _BF16),
        jax.ShapeDtypeStruct((CONV_W - 1, CONV_DIM), _F32),
    ) + tuple(jax.ShapeDtypeStruct(w.shape, _BF16) for w in later_weights)
    out_specs = (
        pl.BlockSpec((PROJ_TILE, CONV_DIM), row),
        pl.BlockSpec((PROJ_TILE * N_HEADS, HEAD_W), row),
        pl.BlockSpec((PROJ_TILE * N_HEADS, V_DIM), row),
        pl.BlockSpec((N_HEADS, 2, PROJ_TILE, HEAD_W), lambda i: (0, 0, i, 0)),
        pl.BlockSpec((N_HEADS, 2, PROJ_TILE, HEAD_W), lambda i: (0, 0, i, 0)),
        pl.BlockSpec((N_HEADS, PROJ_TILE, HEAD_W), head_row),
        pl.BlockSpec((N_HEADS, PROJ_TILE, HEAD_W), head_row),
        pl.BlockSpec((N_HEADS, PROJ_TILE // K_TILE, V_EXT, K_TILE), lambda i: (0, i, 0, 0)),
        pl.BlockSpec((CONV_W - 1, CONV_DIM), lambda i: (0, 0)),
    ) + tuple(weight_specs)
    in_specs = [
        pl.BlockSpec((PROJ_TILE, D_MODEL), row),
        pl.BlockSpec((D_MODEL, IN_WIDTH), lambda i: (0, 0)),
        pl.BlockSpec((CONV_W, CONV_DIM), lambda i: (0, 0)),
    ] + weight_specs
    return _pallas(
        _proj_prompt_kernel, "project_prompt", (n,), in_specs, (x, w_in, conv_w) + tuple(later_weights),
        out_specs, out_shape, scratch=[((F32_SUBLANES, CONV_DIM), _F32)], resident=(1, 2),
        temporaries=3 * _nbytes((PROJ_TILE, CONV_DIM), _F32))


def _project_sample(x, w_in, conv_w, hist1, hist2, period):
    r = x.shape[0]
    full = lambda shape: pl.BlockSpec(shape, lambda i: (0,) * len(shape))
    wide = (r, CONV_DIM)
    in_specs = [full((r, D_MODEL)), full((D_MODEL, IN_WIDTH)), full((CONV_W, CONV_DIM)),
                full(wide), full(wide)]
    out_shape = (jax.ShapeDtypeStruct(wide, _BF16),) + (jax.ShapeDtypeStruct(wide, _F32),) * 4
    return _pallas(
        partial(_proj_sample_kernel, period=period), "project_sample", (1,), in_specs,
        (x, w_in, conv_w, hist1, hist2), (full(wide),) * 5, out_shape,
        resident=tuple(range(len(in_specs))),
        temporaries=(IN_WIDTH // CONV_DIM) * _nbytes(wide, _F32))


def _attn_prompt_kernel(qa_ref, qb_ref, ka_ref, kb_ref, vt_ref, lq1_ref, lk1_ref, lq2_ref, lk2_ref,
                        g_ref, o_ref, s_even, s_odd, cmax_even, cmax_odd, m_ref, acc_ref,
                        *, lambda_init):
    i = pl.program_id(1)
    q_refs = (qa_ref, qb_ref)
    k_refs = (ka_ref, kb_ref)
    bufs = ((s_even, cmax_even), (s_odd, cmax_odd))
    n_full = _div_pow2(i, K_TILE // Q_TILE)

    def col_max(st):
        part = jnp.max(st.reshape(K_TILE // F32_SUBLANES, F32_SUBLANES, Q_TILE), axis=0)
        return jnp.max(part, axis=0, keepdims=True)

    def scores(t, buf):
        s_ref, cmax_ref = buf
        rows = pl.ds(pl.multiple_of(t * K_TILE, K_TILE), K_TILE)
        variant = (t == n_full).astype(jnp.int32)
        for c in range(2):
            st = _dot_nt(k_refs[c][0, rows, :], q_refs[c][0, variant])
            s_ref[c] = st
            cmax_ref[c] = col_max(st)

    def consume(t, buf):
        s_ref, cmax_ref = buf
        vt = vt_ref[0, t]
        for c in range(2):
            m = m_ref[c]
            m_new = jnp.maximum(m, cmax_ref[c])
            alpha = jnp.exp2(m - m_new)
            p = jnp.exp2(s_ref[c] - m_new).astype(_BF16)
            acc_ref[c] = alpha * acc_ref[c] + _dot(vt, p)
            m_ref[c] = m_new

    m_ref[...] = jnp.full(m_ref.shape, -jnp.inf, _F32)
    acc_ref[...] = jnp.zeros(acc_ref.shape, _F32)
    scores(0, bufs[0])

    def pairs(t, count):
        for k in range(count):
            scores(t + 2 * k + 1, bufs[1])
            consume(t + 2 * k, bufs[0])
            scores(t + 2 * k + 2, bufs[0])
            consume(t + 2 * k + 1, bufs[1])

    n_pairs = _div_pow2(n_full, 2)
    pairs_done = 0
    for size in TRIP_PAIRS:
        n_trips = _div_pow2(n_pairs - pairs_done, size)

        def trip(p, carry, size=size, first=pairs_done):
            pairs(2 * (first + p * size), size)
            return carry

        lax.fori_loop(0, n_trips, trip, 0)
        pairs_done = pairs_done + n_trips * size

    t0 = 2 * n_pairs
    two_left = n_full > t0

    @pl.when(two_left)
    def _():
        scores(n_full, bufs[1])
        consume(t0, bufs[0])
        consume(n_full, bufs[1])

    @pl.when(jnp.logical_not(two_left))
    def _():
        consume(t0, bufs[0])

    lam =_lambda_value(lq1_ref[...], lk1_ref[...], lq2_ref[...], lk2_ref[...], lambda_init)
    acc0 = acc_ref[0]
    acc1 = acc_ref[1]
    l0 = acc0[V_DIM:V_DIM + 1, :]
    l1 = acc1[V_DIM:V_DIM + 1, :]
    o = acc0[:V_DIM] / l0 - lam * (acc1[:V_DIM] / l1)
    ms = jnp.mean(o * o, axis=0, keepdims=True)
    on = o * lax.rsqrt(ms + LN_EPS)
    on = on.T * g_ref[...] * (1.0 - lambda_init)
    o_ref[...] = on.astype(_BF16)


def _attention_prompt(qa, qb, ka, kb, vt, lam_vecs, subln_g, lambda_init):
    s = ka.shape[1]
    assert K_TILE % Q_TILE == 0 and s % K_TILE == 0 and Q_TILE % CHUNK == 0
    nq = s // Q_TILE
    vec = pl.BlockSpec((1, QK_DIM), lambda h, i: (0, 0))
    q_tile = pl.BlockSpec((1, 2, Q_TILE, HEAD_W), lambda h, i: (h, 0, i, 0))
    in_specs = [
        q_tile, q_tile,
        pl.BlockSpec((1, s, HEAD_W), lambda h, i: (h, 0, 0)),
        pl.BlockSpec((1, s, HEAD_W), lambda h, i: (h, 0, 0)),
        pl.BlockSpec((1, s // K_TILE, V_EXT, K_TILE), lambda h, i: (h, 0, 0, 0)),
        vec, vec, vec, vec,
        pl.BlockSpec((1, V_DIM), lambda h, i: (0, 0)),
    ]
    logits = ((2, K_TILE, Q_TILE), _F32)
    column = ((2, F32_SUBLANES, Q_TILE), _F32)
    scratch = [logits, logits, ((2, 1, Q_TILE), _F32), ((2, 1, Q_TILE), _F32), ((2, 1, Q_TILE), _F32),
               ((2, V_EXT, Q_TILE), _F32)]
    return _pallas(
        partial(_attn_prompt_kernel, lambda_init=lambda_init), "attention_prompt", (N_HEADS, nq),
        in_specs, (qa, qb, ka, kb, vt) + tuple(lam_vecs) + (subln_g,),
        pl.BlockSpec((Q_TILE, V_DIM), lambda h, i: (i, h)), jax.ShapeDtypeStruct((s, ATTN_DIM), _BF16),
        scratch=scratch, resident=(5, 6, 7, 8, 9),
        temporaries=2 * _nbytes(*logits) + 3 * _nbytes(*column), exclusive=ATTENTION_VMEM_REQUEST)


def _attn_sample_kernel(q_ref, kn_ref, vn_ref, ck_ref, cv_ref, lq1_ref, lk1_ref, lq2_ref, lk2_ref,
                        g_ref, o_ref, *, lambda_init, past_len, n_new):
    group = 2 * N_HEADS
    width = group * n_new
    assert width == HEAD_W
    qb = q_ref[...].astype(_BF16)
    sel_r = lax.broadcasted_iota(jnp.int32, (n_new, width), 0)
    sel_c = lax.broadcasted_iota(jnp.int32, (n_new, width), 1)
    spread = jnp.where(_mod_pow2(sel_c, n_new) == sel_r, 1.0, 0.0).astype(_BF16)
    q_all = _dot_tn(qb, spread)
    blk_r = _div_pow2(lax.broadcasted_iota(jnp.int32, (ATTN_DIM, width), 0), QK_DIM)
    blk_c = _div_pow2(lax.broadcasted_iota(jnp.int32, (ATTN_DIM, width), 1), n_new)
    q_bd = jnp.where(blk_r == blk_c, q_all, 0.0).astype(_BF16)

    half = past_len // 2

    def heads_on_lanes(ref, part):
        return jnp.concatenate(
            [ref[0, pl.ds(part * half * N_HEADS + h, half, stride=N_HEADS), :].astype(_BF16)
             for h in range(N_HEADS)], axis=-1)

    s_old = jnp.concatenate([_dot(heads_on_lanes(ck_ref, part), q_bd) for part in range(2)],
                            axis=0)
    s_new = _dot(kn_ref[...].astype(_BF16), q_bd)

    def chunk_mask(shape, k_off):
        k_pos = k_off + lax.broadcasted_iota(jnp.int32, shape, 0)
        q_pos = past_len + _mod_pow2(lax.broadcasted_iota(jnp.int32, shape, 1), n_new)
        return _div_pow2(k_pos, CHUNK) <= _div_pow2(q_pos, CHUNK)

    s_old = jnp.where(chunk_mask(s_old.shape, 0), s_old, -jnp.inf)
    s_new = jnp.where(chunk_mask(s_new.shape, past_len), s_new, -jnp.inf)
    m = jnp.maximum(jnp.max(s_old, axis=0, keepdims=True), jnp.max(s_new, axis=0, keepdims=True))
    p_old = jnp.exp(s_old - m)
    p_new = jnp.exp(s_new - m)
    inv_l = 1.0 / (jnp.sum(p_old, axis=0, keepdims=True) + jnp.sum(p_new, axis=0, keepdims=True))
    a_old = (p_old * inv_l).astype(_BF16)
    a_new = (p_new * inv_l).astype(_BF16)
    pv = (_dot_tn(a_old[:half], heads_on_lanes(cv_ref, 0)) + _dot_tn(a_old[half:], heads_on_lanes(cv_ref, 1))
          + _dot_tn(a_new, vn_ref[...].astype(_BF16)))

    lam = _lambda_value(lq1_ref[...], lk1_ref[...], lq2_ref[...], lk2_ref[...], lambda_init)
    g = g_ref[...]
    for h in range(N_HEADS):
        cols = slice(h * V_DIM, (h + 1) * V_DIM)
        r0 = h * 2 * n_new
        o = pv[r0:r0 + n_new, cols] - lam * pv[r0 + n_new:r0 + 2 * n_new, cols]
        ms = jnp.mean(o * o, axis=-1, keepdims=True)
        o_ref[:, cols] = (o * lax.rsqrt(ms + LN_EPS) * g * (1.0 - lambda_init)).astype(_BF16)


def _attention_sample(q, k_new, v_new, cache_k, cache_v, lam_vecs, subln_g, lambda_init, n_new):
    nb, rows_per_stream = cache_k.shape[:2]
    past_len = rows_per_stream // N_HEADS
    cache_block = pl.BlockSpec((1, rows_per_stream, HEAD_W), lambda b: (b, 0, 0))
    rows = lambda b: (b, 0)
    vec = pl.BlockSpec((1, QK_DIM), lambda b: (0, 0))
    in_specs = [
        pl.BlockSpec((n_new, ATTN_DIM), rows),
        pl.BlockSpec((n_new, ATTN_DIM), rows),
        pl.BlockSpec((n_new, ATTN_DIM), rows),
        cache_block, cache_block,
        vec, vec, vec, vec,
        pl.BlockSpec((1, V_DIM), lambda b: (0, 0)),
    ]
    return _pallas(
        partial(_attn_sample_kernel, lambda_init=lambda_init, past_len=past_len, n_new=n_new),
        "attention_sample", (nb,), in_specs,
        (q, k_new, v_new, cache_k, cache_v) + tuple(lam_vecs) + (subln_g,),
        pl.BlockSpec((n_new, ATTN_DIM), rows), jax.ShapeDtypeStruct((nb * n_new, ATTN_DIM), _BF16),
        resident=(5, 6, 7, 8, 9),
        temporaries=2 * _nbytes((past_len, ATTN_DIM), _BF16) + 2 * _nbytes((past_len, HEAD_W), _F32))


def _finish_kernel(x_ref, yconv_ref, attn_ref, wo_ref, g1_ref, b1_ref, w1_ref, w2_ref, g2_ref, b2_ref,
                   y_ref):
    tile = x_ref.shape[0]
    group = min(tile, FINISH_GROUP)
    groups = [slice(r0, r0 + group) for r0 in range(0, tile, group)]
    pre = [ALPHA * x_ref[rows, :]
           + _dot(jnp.concatenate([yconv_ref[rows, :], attn_ref[rows, :]], axis=-1), wo_ref[...])
           for rows in groups]
    for rows, z in zip(groups, pre):
        x1 = _layer_norm(z, g1_ref[...], b1_ref[...])
        x1b = x1.astype(_BF16)
        ff = jnp.zeros_like(x1)
        for c in range(D_FF // FF_CHUNK):
            cols = slice(c * FF_CHUNK, (c + 1) * FF_CHUNK)
            hdn = jnp.square(jnp.maximum(_dot(x1b, w1_ref[:, cols]), 0.0))
            ff = ff + _dot(hdn.astype(_BF16), w2_ref[cols, :])
        y_ref[rows, :] = _layer_norm(ALPHA * x1 + ff, g2_ref[...], b2_ref[...])


def _finish(x, yconv, attn, w_out, g1, b1, w_ff1, w_ff2, g2, b2):
    r = x.shape[0]
    tile = min(ROW_TILE, r)
    assert r % tile == 0
    row = lambda i: (i, 0)
    const = lambda shape: pl.BlockSpec(shape, lambda i: (0, 0))
    in_specs = [
        pl.BlockSpec((tile, D_MODEL), row),
        pl.BlockSpec((tile, CONV_DIM), row),
        pl.BlockSpec((tile, ATTN_DIM), row),
        const((D_MODEL, D_MODEL)), const((1, D_MODEL)), const((1, D_MODEL)),
        const((D_MODEL, D_FF)), const((D_FF, D_MODEL)), const((1, D_MODEL)), const((1, D_MODEL)),
    ]
    return _pallas(
        _finish_kernel, "finish_layer", (r // tile,), in_specs,
        (x, yconv, attn, w_out, g1, b1, w_ff1, w_ff2, g2, b2),
        pl.BlockSpec((tile, D_MODEL), row), jax.ShapeDtypeStruct((r, D_MODEL), _F32),
        resident=tuple(range(3, len(in_specs))),
        temporaries=3 * _nbytes((tile, D_MODEL), _F32) + _nbytes((tile, FF_CHUNK), _F32))


def kernel(x_prompt, x_sample, cache_k, cache_v, state_conv, w_in, conv_w, lambda_q1, lambda_k1,
           lambda_q2, lambda_k2, subln_g, w_out, ln1_g, ln1_b, w_ff1, w_ff2, ln2_g, ln2_b):
    bp, sp, _ = x_prompt.shape
    bs, ss, _ = x_sample.shape
    depth = w_in.shape[0]
    assert bp == 1 and depth == 1
    past_len = cache_k.shape[2]
    l = 0
    lambda_init = 0.8 - 0.6 * float(np.exp(-0.3 * l))

    w_in_b = w_in[l].astype(_BF16)
    lam_vecs = tuple(v[l].reshape(1, QK_DIM) for v in (lambda_q1, lambda_k1, lambda_q2, lambda_k2))
    g_sub = subln_g[l].reshape(1, V_DIM)
    ln = tuple(v[l].reshape(1, D_MODEL) for v in (ln1_g, ln1_b, ln2_g, ln2_b))

    xp2 = x_prompt.reshape(sp, D_MODEL)
    (yconv_p, k_p, v_p, qa, qb, ka, kb, vt, conv_p, w_out_b, w_ff1_b, w_ff2_b) = _project_prompt(
        xp2, w_in_b, conv_w[l], (w_out[l], w_ff1[l], w_ff2[l]))

    def finish(x2d, yconv, attn):
        return _finish(x2d, yconv, attn, w_out_b, ln[0], ln[1], w_ff1_b, w_ff2_b, ln[2], ln[3])

    attn_p = _attention_prompt(qa, qb, ka, kb, vt, lam_vecs, g_sub, lambda_init)
    y_prompt = finish(xp2, yconv_p, attn_p).reshape(bp, sp, D_MODEL)

    xs2 = x_sample.reshape(bs * ss, D_MODEL)
    st = state_conv[l].astype(_F32)
    pad = lambda a: jnp.pad(a, ((0, 0), (0, ss - a.shape[1]), (0, 0))).reshape(bs * ss, CONV_DIM)
    hist2 = pad(st)
    hist1 = pad(st[:, 1:])
    yconv_s, q_s, k_s, v_s, u_s = _project_sample(xs2, w_in_b, conv_w[l], hist1, hist2, ss)
    ck = cache_k.reshape(depth * bs, past_len * N_HEADS, HEAD_W)
    cv = cache_v.reshape(depth * bs, past_len * N_HEADS, V_DIM)
    attn_s = _attention_sample(q_s, k_s, v_s, ck, cv, lam_vecs, g_sub, lambda_init, ss)
    y_sample = finish(xs2, yconv_s, attn_s).reshape(bs, ss, D_MODEL)

    k_prompt = k_p.reshape(depth, bp, sp, N_HEADS, HEAD_W)
    v_prompt = v_p.reshape(depth, bp, sp, N_HEADS, V_DIM)
    conv_prompt = conv_p.reshape(depth, bp, CONV_W - 1, CONV_DIM)
    k_sample = k_s.reshape(depth, bs, ss, N_HEADS, HEAD_W)
    v_sample = v_s.reshape(depth, bs, ss, N_HEADS, V_DIM)
    conv_sample = u_s.reshape(bs, ss, CONV_DIM)[:, ss - (CONV_W - 1):].reshape(
        depth, bs, CONV_W - 1, CONV_DIM)
    return (y_prompt, y_sample, k_prompt, v_prompt, conv_prompt, k_sample, v_sample, conv_sample)
```

```python
from functools import partial

import jax
import jax.numpy as jnp
import numpy as np
from jax import lax
from jax.experimental import pallas as pl
from jax.experimental.pallas import tpu as pltpu

D_MODEL = 1024
CHUNK = 64
CONV_DIM = 512
ATTN_DIM = 512
QK_DIM = 64
N_HEADS = 4
HEAD_W = 2 * QK_DIM
V_DIM = 128
F32_SUBLANES = 8
BF16_SUBLANES = 16
V_EXT = V_DIM + BF16_SUBLANES
LOG2E = 1.4426950408889634
CONV_W = 3
D_FF = 4096
LN_EPS = 1e-5
DEPTH = 1
ALPHA = (2 * DEPTH) ** 0.25
IN_WIDTH = 3 * CONV_DIM + 3 * ATTN_DIM

V7X_MXU_DIM = 256
V7X_VMEM_BYTES = 64 * 1024 * 1024
PIPELINE_BUFFERS = 2
VMEM_REQUEST_LARGE = 7 * V7X_VMEM_BYTES // 8
VMEM_REQUEST_MEDIUM = 6 * V7X_VMEM_BYTES // 8
VMEM_REQUEST_SMALL = 5 * V7X_VMEM_BYTES // 8

ROW_TILE = 1024
FINISH_GROUP = 256
Q_TILE = 2 * V7X_MXU_DIM
K_TILE = 2 * V7X_MXU_DIM
PROJ_TILE = 1024
TRIP_PAIRS = (4, 2, 1)
FF_CHUNK = 1024

_F32 = jnp.float32
_BF16 = jnp.bfloat16


def _nbytes(shape, dtype):
    return int(np.prod([1 if d is None else d for d in shape])) * jnp.dtype(dtype).itemsize


def _pallas(body, name, grid, in_specs, operands, out_specs, out_shape, scratch=(), resident=(),
            temporaries=0, exclusive=0):
    out_specs_seq = out_specs if isinstance(out_specs, (tuple, list)) else (out_specs,)
    out_shape_seq = out_shape if isinstance(out_shape, (tuple, list)) else (out_shape,)
    limit = temporaries + sum(_nbytes(shape, dtype) for shape, dtype in scratch)
    for pos, (spec, arr) in enumerate(zip(in_specs, operands)):
        limit += (1 if pos in resident else PIPELINE_BUFFERS) * _nbytes(spec.block_shape, arr.dtype)
    for spec, arr in zip(out_specs_seq, out_shape_seq):
        limit += PIPELINE_BUFFERS * _nbytes(spec.block_shape, arr.dtype)
    limit = max(limit, exclusive)
    assert limit <= V7X_VMEM_BYTES, (name, limit)
    return pl.pallas_call(
        body,
        grid=grid,
        in_specs=list(in_specs),
        out_specs=out_specs,
        out_shape=out_shape,
        scratch_shapes=[pltpu.VMEM(shape, dtype) for shape, dtype in scratch],
        compiler_params=pltpu.CompilerParams(
            dimension_semantics=("arbitrary",) * len(grid), vmem_limit_bytes=limit),
        name=name,
    )(*operands)


def _dot(a, b):
    return jnp.dot(a, b, preferred_element_type=_F32)


def _dot_nt(a, b):
    return lax.dot_general(a, b, (((1,), (1,)), ((), ())), preferred_element_type=_F32)


def _dot_tn(a, b):
    return lax.dot_general(a, b, (((0,), (0,)), ((), ())), preferred_element_type=_F32)


def _div_pow2(x, n):
    assert n & (n - 1) == 0
    return lax.shift_right_logical(x, n.bit_length() - 1)


def _mod_pow2(x, n):
    assert n & (n - 1) == 0
    return lax.bitwise_and(x, n - 1)


def _layer_norm(x, g, b):
    mu = jnp.mean(x, axis=-1, keepdims=True)
    xc = x - mu
    var = jnp.mean(xc * xc, axis=-1, keepdims=True)
    return xc * lax.rsqrt(var + LN_EPS) * g + b


def _lambda_value(lq1, lk1, lq2, lk2, lambda_init):
    s1 = jnp.sum(lq1 * lk1, axis=-1, keepdims=True)
    s2 = jnp.sum(lq2 * lk2, axis=-1, keepdims=True)
    return jnp.exp(s1) - jnp.exp(s2) + lambda_init


def _project_columns(x_ref, w_ref):
    xb = x_ref[...].astype(_BF16)
    return [_dot(xb, w_ref[:, g * CONV_DIM:(g + 1) * CONV_DIM]) for g in range(IN_WIDTH // CONV_DIM)]


def _conv_from_taps(u, u1, u2, cw_ref):
    return cw_ref[0:1, :] * u2 + cw_ref[1:2, :] * u1 + cw_ref[2:3, :] * u


def _proj_prompt_kernel(x_ref, w_ref, cw_ref, wo_ref, w1_ref, w2_ref,
                        yconv_ref, kf_ref, vf_ref, qa_ref, qb_ref, ka_ref, kb_ref, vt_ref, cstate_ref,
                        wo_b_ref, w1_b_ref, w2_b_ref, carry_ref):
    i = pl.program_id(0)

    wo_b_ref[...] = wo_ref[...].astype(_BF16)
    w1_b_ref[...] = w1_ref[...].astype(_BF16)
    w2_b_ref[...] = w2_ref[...].astype(_BF16)

    @pl.when(i == 0)
    def _():
        carry_ref[...] = jnp.zeros_like(carry_ref)

    gate, cc, hh, q, k, v = _project_columns(x_ref, w_ref)
    u = cc * hh
    rows = lax.broadcasted_iota(jnp.int32, u.shape, 0)
    prev2 = carry_ref[0:1, :]
    prev1 = carry_ref[1:2, :]
    u1 = jnp.where(rows == 0, prev1, pltpu.roll(u, 1, 0))
    u2 = jnp.where(rows == 0, prev2, jnp.where(rows == 1, prev1, pltpu.roll(u, 2, 0)))
    yconv_ref[...] = (gate * _conv_from_taps(u, u1, u2, cw_ref)).astype(_BF16)
    tail = u[PROJ_TILE - (CONV_W - 1):, :]
    carry_ref[0:CONV_W - 1, :] = tail
    cstate_ref[...] = tail

    for h in range(N_HEADS):
        sl = slice(h * HEAD_W, (h + 1) * HEAD_W)
        kf_ref[pl.ds(h, PROJ_TILE, stride=N_HEADS), :] = k[:, sl]
        vf_ref[pl.ds(h, PROJ_TILE, stride=N_HEADS), :] = v[:, sl]
    qs = q * (QK_DIM ** -0.5 * LOG2E)
    lane = lax.broadcasted_iota(jnp.int32, (PROJ_TILE, HEAD_W), 1)
    first = lane < QK_DIM
    ones_row = jnp.where(lax.broadcasted_iota(jnp.int32, (BF16_SUBLANES, K_TILE), 0) == 0,
                         1.0, 0.0).astype(_BF16)
    chunks_per_ktile = K_TILE // CHUNK
    assert QK_DIM // chunks_per_ktile >= 2
    row = lax.broadcasted_iota(jnp.int32, (PROJ_TILE, HEAD_W), 0)
    key_chunk = _div_pow2(_mod_pow2(row, K_TILE), CHUNK)
    lane_chunk = _mod_pow2(lane, chunks_per_ktile)
    chunk_onehot = jnp.where(lane_chunk == key_chunk, 1.0, 0.0)
    q_tile = _div_pow2(i * PROJ_TILE + row, Q_TILE)
    tile_offset = _mod_pow2(q_tile, K_TILE // Q_TILE) * (Q_TILE // CHUNK)
    query_chunk = _div_pow2(_mod_pow2(row, Q_TILE), CHUNK) + tile_offset
    mask_weight = jnp.where(lane_chunk > query_chunk, float(jnp.finfo(_BF16).min), 0.0)
    for h in range(N_HEADS):
        sl = slice(h * HEAD_W, (h + 1) * HEAD_W)
        qh = qs[:, sl]
        kh = k[:, sl]
        qa_ref[h, 0] = jnp.where(first, qh, 0.0).astype(_BF16)
        qb_ref[h, 0] = jnp.where(first, 0.0, qh).astype(_BF16)
        qa_ref[h, 1] = jnp.where(first, qh, mask_weight).astype(_BF16)
        qb_ref[h, 1] = jnp.where(first, mask_weight, qh).astype(_BF16)
        ka_ref[h] = jnp.where(first, kh, chunk_onehot).astype(_BF16)
        kb_ref[h] = jnp.where(first, chunk_onehot, kh).astype(_BF16)
        for s in range(PROJ_TILE // K_TILE):
            vt_ref[h, s, :V_DIM, :] = v[s * K_TILE:(s + 1) * K_TILE, sl].T.astype(_BF16)
            vt_ref[h, s, V_DIM:, :] = ones_row


def _proj_sample_kernel(x_ref, w_ref, cw_ref, h1_ref, h2_ref,
                        yconv_ref, q_ref, k_ref, v_ref, u_ref, *, period):
    gate, cc, hh, q, k, v = _project_columns(x_ref, w_ref)
    u = cc * hh
    t = _mod_pow2(lax.broadcasted_iota(jnp.int32, u.shape, 0), period)
    u1 = jnp.where(t < 1, h1_ref[...], pltpu.roll(u, 1, 0))
    u2 = jnp.where(t < 2, h2_ref[...], pltpu.roll(u, 2, 0))
    yconv_ref[...] = (gate * _conv_from_taps(u, u1, u2, cw_ref)).astype(_BF16)
    q_ref[...] = q * (QK_DIM ** -0.5)
    k_ref[...] = k
    v_ref[...] = v
    u_ref[...] = u


def _project_prompt(x, w_in, conv_w, later_weights):
    s = x.shape[0]
    assert s % PROJ_TILE == 0 and PROJ_TILE % K_TILE == 0
    n = s // PROJ_TILE
    row = lambda i: (i, 0)
    head_row = lambda i: (0, i, 0)
    for w in later_weights:
        assert w.shape[0] % (n * BF16_SUBLANES) == 0
    weight_specs = [pl.BlockSpec((w.shape[0] // n, w.shape[1]), row) for w in later_weights]
    out_shape = (
        jax.ShapeDtypeStruct((s, CONV_DIM), _BF16),
        jax.ShapeDtypeStruct((s * N_HEADS, HEAD_W), _F32),
        jax.ShapeDtypeStruct((s * N_HEADS, V_DIM), _F32),
        jax.ShapeDtypeStruct((N_HEADS, 2, s, HEAD_W), _BF16),
        jax.ShapeDtypeStruct((N_HEADS, 2, s, HEAD_W), _BF16),
        jax.ShapeDtypeStruct((N_HEADS, s, HEAD_W), _BF16),
        jax.ShapeDtypeStruct((N_HEADS, s, HEAD_W), _BF16),
        jax.ShapeDtypeStruct((N_HEADS, s // K_TILE, V_EXT, K_TILE), _BF16),
        jax.ShapeDtypeStruct((CONV_W - 1, CONV_DIM), _F32),
    ) + tuple(jax.ShapeDtypeStruct(w.shape, _BF16) for w in later_weights)
    out_specs = (
        pl.BlockSpec((PROJ_TILE, CONV_DIM), row),
        pl.BlockSpec((PROJ_TILE * N_HEADS, HEAD_W), row),
        pl.BlockSpec((PROJ_TILE * N_HEADS, V_DIM), row),
        pl.BlockSpec((N_HEADS, 2, PROJ_TILE, HEAD_W), lambda i: (0, 0, i, 0)),
        pl.BlockSpec((N_HEADS, 2, PROJ_TILE, HEAD_W), lambda i: (0, 0, i, 0)),
        pl.BlockSpec((N_HEADS, PROJ_TILE, HEAD_W), head_row),
        pl.BlockSpec((N_HEADS, PROJ_TILE, HEAD_W), head_row),
        pl.BlockSpec((N_HEADS, PROJ_TILE // K_TILE, V_EXT, K_TILE), lambda i: (0, i, 0, 0)),
        pl.BlockSpec((CONV_W - 1, CONV_DIM), lambda i: (0, 0)),
    ) + tuple(weight_specs)
    in_specs = [
        pl.BlockSpec((PROJ_TILE, D_MODEL), row),
        pl.BlockSpec((D_MODEL, IN_WIDTH), lambda i: (0, 0)),
        pl.BlockSpec((CONV_W, CONV_DIM), lambda i: (0, 0)),
    ] + weight_specs
    return _pallas(
        _proj_prompt_kernel, "project_prompt", (n,), in_specs, (x, w_in, conv_w) + tuple(later_weights),
        out_specs, out_shape, scratch=[((F32_SUBLANES, CONV_DIM), _F32)], resident=(1, 2),
        temporaries=3 * _nbytes((PROJ_TILE, CONV_DIM), _F32), exclusive=VMEM_REQUEST_LARGE)


def _project_sample(x, w_in, conv_w, hist1, hist2, period):
    r = x.shape[0]
    full = lambda shape: pl.BlockSpec(shape, lambda i: (0,) * len(shape))
    wide = (r, CONV_DIM)
    in_specs = [full((r, D_MODEL)), full((D_MODEL, IN_WIDTH)), full((CONV_W, CONV_DIM)),
                full(wide), full(wide)]
    out_shape = (jax.ShapeDtypeStruct(wide, _BF16),) + (jax.ShapeDtypeStruct(wide, _F32),) * 4
    return _pallas(
        partial(_proj_sample_kernel, period=period), "project_sample", (1,), in_specs,
        (x, w_in, conv_w, hist1, hist2), (full(wide),) * 5, out_shape,
        resident=tuple(range(len(in_specs))),
        temporaries=(IN_WIDTH // CONV_DIM) * _nbytes(wide, _F32), exclusive=VMEM_REQUEST_MEDIUM)


def _attn_prompt_kernel(qa_ref, qb_ref, ka_ref, kb_ref, vt_ref, lq1_ref, lk1_ref, lq2_ref, lk2_ref,
                        g_ref, o_ref, s_even, s_odd, cmax_even, cmax_odd, m_ref, acc_ref,
                        *, lambda_init):
    i = pl.program_id(1)
    q_refs = (qa_ref, qb_ref)
    k_refs = (ka_ref, kb_ref)
    bufs = ((s_even, cmax_even), (s_odd, cmax_odd))
    n_full = _div_pow2(i, K_TILE // Q_TILE)

    def col_max(st):
        part = jnp.max(st.reshape(K_TILE // F32_SUBLANES, F32_SUBLANES, Q_TILE), axis=0)
        return jnp.max(part, axis=0, keepdims=True)

    def scores(t, buf):
        s_ref, cmax_ref = buf
        rows = pl.ds(pl.multiple_of(t * K_TILE, K_TILE), K_TILE)
        variant = (t == n_full).astype(jnp.int32)
        for c in range(2):
            st = _dot_nt(k_refs[c][0, rows, :], q_refs[c][0, variant])
            s_ref[c] = st
            cmax_ref[c] = col_max(st)

    def consume(t, buf):
        s_ref, cmax_ref = buf
        vt = vt_ref[0, t]
        for c in range(2):
            m = m_ref[c]
            m_new = jnp.maximum(m, cmax_ref[c])
            alpha = jnp.exp2(m - m_new)
            p = jnp.exp2(s_ref[c] - m_new).astype(_BF16)
            acc_ref[c] = alpha * acc_ref[c] + _dot(vt, p)
            m_ref[c] = m_new

    m_ref[...] = jnp.full(m_ref.shape, -jnp.inf, _F32)
    acc_ref[...] = jnp.zeros(acc_ref.shape, _F32)
    scores(0, bufs[0])

    def pairs(t, count):
        for k in range(count):
            scores(t + 2 * k + 1, bufs[1])
            consume(t + 2 * k, bufs[0])
            scores(t + 2 * k + 2, bufs[0])
            consume(t + 2 * k + 1, bufs[1])

    n_pairs = _div_pow2(n_full, 2)
    pairs_done = 0
    for size in TRIP_PAIRS:
        n_trips = _div_pow2(n_pairs - pairs_done, size)

        def trip(p, carry, size=size, first=pairs_done):
            pairs(2 * (first + p * size), size)
            return carry

        lax.fori_loop(0, n_trips, trip, 0)
        pairs_done = pairs_done + n_trips * size

    t0 = 2 * n_pairs
    two_left = n_full > t0

    @pl.when(two_left)
    def _():
        scores(n_full, bufs[1])
        consume(t0, bufs[0])
        consume(n_full, bufs[1])

    @pl.when(jnp.logical_not(two_left))
    def _():
        consume(t0, bufs[0])

    lam =_lambda_value(lq1_ref[...], lk1_ref[...], lq2_ref[...], lk2_ref[...], lambda_init)
    acc0 = acc_ref[0]
    acc1 = acc_ref[1]
    l0 = acc0[V_DIM:V_DIM + 1, :]
    l1 = acc1[V_DIM:V_DIM + 1, :]
    o = acc0[:V_DIM] / l0 - lam * (acc1[:V_DIM] / l1)
    ms = jnp.mean(o * o, axis=0, keepdims=True)
    on = o * lax.rsqrt(ms + LN_EPS)
    on = on.T * g_ref[...] * (1.0 - lambda_init)
    o_ref[...] = on.astype(_BF16)


def _attention_prompt(qa, qb, ka, kb, vt, lam_vecs, subln_g, lambda_init):
    s = ka.shape[1]
    assert K_TILE % Q_TILE == 0 and s % K_TILE == 0 and Q_TILE % CHUNK == 0
    nq = s // Q_TILE
    vec = pl.BlockSpec((1, QK_DIM), lambda h, i: (0, 0))
    q_tile = pl.BlockSpec((1, 2, Q_TILE, HEAD_W), lambda h, i: (h, 0, i, 0))
    in_specs = [
        q_tile, q_tile,
        pl.BlockSpec((1, s, HEAD_W), lambda h, i: (h, 0, 0)),
        pl.BlockSpec((1, s, HEAD_W), lambda h, i: (h, 0, 0)),
        pl.BlockSpec((1, s // K_TILE, V_EXT, K_TILE), lambda h, i: (h, 0, 0, 0)),
        vec, vec, vec, vec,
        pl.BlockSpec((1, V_DIM), lambda h, i: (0, 0)),
    ]
    logits = ((2, K_TILE, Q_TILE), _F32)
    column = ((2, F32_SUBLANES, Q_TILE), _F32)
    scratch = [logits, logits, ((2, 1, Q_TILE), _F32), ((2, 1, Q_TILE), _F32), ((2, 1, Q_TILE), _F32),
               ((2, V_EXT, Q_TILE), _F32)]
    return _pallas(
        partial(_attn_prompt_kernel, lambda_init=lambda_init), "attention_prompt", (N_HEADS, nq),
        in_specs, (qa, qb, ka, kb, vt) + tuple(lam_vecs) + (subln_g,),
        pl.BlockSpec((Q_TILE, V_DIM), lambda h, i: (i, h)), jax.ShapeDtypeStruct((s, ATTN_DIM), _BF16),
        scratch=scratch, resident=(5, 6, 7, 8, 9),
        temporaries=2 * _nbytes(*logits) + 3 * _nbytes(*column), exclusive=VMEM_REQUEST_MEDIUM)


def _attn_sample_kernel(q_ref, kn_ref, vn_ref, ck_ref, cv_ref, lq1_ref, lk1_ref, lq2_ref, lk2_ref,
                        g_ref, o_ref, *, lambda_init, past_len, n_new):
    group = 2 * N_HEADS
    width = group * n_new
    assert width == HEAD_W
    qb = q_ref[...].astype(_BF16)
    sel_r = lax.broadcasted_iota(jnp.int32, (n_new, width), 0)
    sel_c = lax.broadcasted_iota(jnp.int32, (n_new, width), 1)
    spread = jnp.where(_mod_pow2(sel_c, n_new) == sel_r, 1.0, 0.0).astype(_BF16)
    q_all = _dot_tn(qb, spread)
    blk_r = _div_pow2(lax.broadcasted_iota(jnp.int32, (ATTN_DIM, width), 0), QK_DIM)
    blk_c = _div_pow2(lax.broadcasted_iota(jnp.int32, (ATTN_DIM, width), 1), n_new)
    q_bd = jnp.where(blk_r == blk_c, q_all, 0.0).astype(_BF16)

    half = past_len // 2

    def heads_on_lanes(ref, part):
        return jnp.concatenate(
            [ref[0, pl.ds(part * half * N_HEADS + h, half, stride=N_HEADS), :].astype(_BF16)
             for h in range(N_HEADS)], axis=-1)

    s_old = jnp.concatenate([_dot(heads_on_lanes(ck_ref, part), q_bd) for part in range(2)],
                            axis=0)
    s_new = _dot(kn_ref[...].astype(_BF16), q_bd)

    def chunk_mask(shape, k_off):
        k_pos = k_off + lax.broadcasted_iota(jnp.int32, shape, 0)
        q_pos = past_len + _mod_pow2(lax.broadcasted_iota(jnp.int32, shape, 1), n_new)
        return _div_pow2(k_pos, CHUNK) <= _div_pow2(q_pos, CHUNK)

    s_old = jnp.where(chunk_mask(s_old.shape, 0), s_old, -jnp.inf)
    s_new = jnp.where(chunk_mask(s_new.shape, past_len), s_new, -jnp.inf)
    m = jnp.maximum(jnp.max(s_old, axis=0, keepdims=True), jnp.max(s_new, axis=0, keepdims=True))
    p_old = jnp.exp(s_old - m)
    p_new = jnp.exp(s_new - m)
    inv_l = 1.0 / (jnp.sum(p_old, axis=0, keepdims=True) + jnp.sum(p_new, axis=0, keepdims=True))
    a_old = (p_old * inv_l).astype(_BF16)
    a_new = (p_new * inv_l).astype(_BF16)
    pv = (_dot_tn(a_old[:half], heads_on_lanes(cv_ref, 0)) + _dot_tn(a_old[half:], heads_on_lanes(cv_ref, 1))
          + _dot_tn(a_new, vn_ref[...].astype(_BF16)))

    lam = _lambda_value(lq1_ref[...], lk1_ref[...], lq2_ref[...], lk2_ref[...], lambda_init)
    g = g_ref[...]
    for h in range(N_HEADS):
        cols = slice(h * V_DIM, (h + 1) * V_DIM)
        r0 = h * 2 * n_new
        o = pv[r0:r0 + n_new, cols] - lam * pv[r0 + n_new:r0 + 2 * n_new, cols]
        ms = jnp.mean(o * o, axis=-1, keepdims=True)
        o_ref[:, cols] = (o * lax.rsqrt(ms + LN_EPS) * g * (1.0 - lambda_init)).astype(_BF16)


def _attention_sample(q, k_new, v_new, cache_k, cache_v, lam_vecs, subln_g, lambda_init, n_new):
    nb, rows_per_stream = cache_k.shape[:2]
    past_len = rows_per_stream // N_HEADS
    cache_block = pl.BlockSpec((1, rows_per_stream, HEAD_W), lambda b: (b, 0, 0))
    rows = lambda b: (b, 0)
    vec = pl.BlockSpec((1, QK_DIM), lambda b: (0, 0))
    in_specs = [
        pl.BlockSpec((n_new, ATTN_DIM), rows),
        pl.BlockSpec((n_new, ATTN_DIM), rows),
        pl.BlockSpec((n_new, ATTN_DIM), rows),
        cache_block, cache_block,
        vec, vec, vec, vec,
        pl.BlockSpec((1, V_DIM), lambda b: (0, 0)),
    ]
    return _pallas(
        partial(_attn_sample_kernel, lambda_init=lambda_init, past_len=past_len, n_new=n_new),
        "attention_sample", (nb,), in_specs,
        (q, k_new, v_new, cache_k, cache_v) + tuple(lam_vecs) + (subln_g,),
        pl.BlockSpec((n_new, ATTN_DIM), rows), jax.ShapeDtypeStruct((nb * n_new, ATTN_DIM), _BF16),
        resident=(5, 6, 7, 8, 9),
        temporaries=2 * _nbytes((past_len, ATTN_DIM), _BF16) + 2 * _nbytes((past_len, HEAD_W), _F32),
        exclusive=VMEM_REQUEST_SMALL)


def _finish_kernel(x_ref, yconv_ref, attn_ref, wo_ref, g1_ref, b1_ref, w1_ref, w2_ref, g2_ref, b2_ref,
                   y_ref):
    tile = x_ref.shape[0]
    group = min(tile, FINISH_GROUP)
    groups = [slice(r0, r0 + group) for r0 in range(0, tile, group)]
    pre = [ALPHA * x_ref[rows, :]
           + _dot(jnp.concatenate([yconv_ref[rows, :], attn_ref[rows, :]], axis=-1), wo_ref[...])
           for rows in groups]
    for rows, z in zip(groups, pre):
        x1 = _layer_norm(z, g1_ref[...], b1_ref[...])
        x1b = x1.astype(_BF16)
        ff = jnp.zeros_like(x1)
        for c in range(D_FF // FF_CHUNK):
            cols = slice(c * FF_CHUNK, (c + 1) * FF_CHUNK)
            hdn = jnp.square(jnp.maximum(_dot(x1b, w1_ref[:, cols]), 0.0))
            ff = ff + _dot(hdn.astype(_BF16), w2_ref[cols, :])
        y_ref[rows, :] = _layer_norm(ALPHA * x1 + ff, g2_ref[...], b2_ref[...])


def _finish(x, yconv, attn, w_out, g1, b1, w_ff1, w_ff2, g2, b2):
    r = x.shape[0]
    tile = min(ROW_TILE, r)
    assert r % tile == 0
    row = lambda i: (i, 0)
    const = lambda shape: pl.BlockSpec(shape, lambda i: (0, 0))
    in_specs = [
        pl.BlockSpec((tile, D_MODEL), row),
        pl.BlockSpec((tile, CONV_DIM), row),
        pl.BlockSpec((tile, ATTN_DIM), row),
        const((D_MODEL, D_MODEL)), const((1, D_MODEL)), const((1, D_MODEL)),
        const((D_MODEL, D_FF)), const((D_FF, D_MODEL)), const((1, D_MODEL)), const((1, D_MODEL)),
    ]
    return _pallas(
        _finish_kernel, "finish_layer", (r // tile,), in_specs,
        (x, yconv, attn, w_out, g1, b1, w_ff1, w_ff2, g2, b2),
        pl.BlockSpec((tile, D_MODEL), row), jax.ShapeDtypeStruct((r, D_MODEL), _F32),
        resident=tuple(range(3, len(in_specs))),
        temporaries=3 * _nbytes((tile, D_MODEL), _F32) + _nbytes((tile, FF_CHUNK), _F32),
        exclusive=VMEM_REQUEST_LARGE)


def kernel(x_prompt, x_sample, cache_k, cache_v, state_conv, w_in, conv_w, lambda_q1, lambda_k1,
           lambda_q2, lambda_k2, subln_g, w_out, ln1_g, ln1_b, w_ff1, w_ff2, ln2_g, ln2_b):
    bp, sp, _ = x_prompt.shape
    bs, ss, _ = x_sample.shape
    depth = w_in.shape[0]
    assert bp == 1 and depth == 1
    past_len = cache_k.shape[2]
    l = 0
    lambda_init = 0.8 - 0.6 * float(np.exp(-0.3 * l))

    w_in_b = w_in[l].astype(_BF16)
    lam_vecs = tuple(v[l].reshape(1, QK_DIM) for v in (lambda_q1, lambda_k1, lambda_q2, lambda_k2))
    g_sub = subln_g[l].reshape(1, V_DIM)
    ln = tuple(v[l].reshape(1, D_MODEL) for v in (ln1_g, ln1_b, ln2_g, ln2_b))

    xp2 = x_prompt.reshape(sp, D_MODEL)
    (yconv_p, k_p, v_p, qa, qb, ka, kb, vt, conv_p, w_out_b, w_ff1_b, w_ff2_b) = _project_prompt(
        xp2, w_in_b, conv_w[l], (w_out[l], w_ff1[l], w_ff2[l]))

    def finish(x2d, yconv, attn):
        return _finish(x2d, yconv, attn, w_out_b, ln[0], ln[1], w_ff1_b, w_ff2_b, ln[2], ln[3])

    attn_p = _attention_prompt(qa, qb, ka, kb, vt, lam_vecs, g_sub, lambda_init)
    y_prompt = finish(xp2, yconv_p, attn_p).reshape(bp, sp, D_MODEL)

    xs2 = x_sample.reshape(bs * ss, D_MODEL)
    st = state_conv[l].astype(_F32)
    pad = lambda a: jnp.pad(a, ((0, 0), (0, ss - a.shape[1]), (0, 0))).reshape(bs * ss, CONV_DIM)
    hist2 = pad(st)
    hist1 = pad(st[:, 1:])
    yconv_s, q_s, k_s, v_s, u_s = _project_sample(xs2, w_in_b, conv_w[l], hist1, hist2, ss)
    ck = cache_k.reshape(depth * bs, past_len * N_HEADS, HEAD_W)
    cv = cache_v.reshape(depth * bs, past_len * N_HEADS, V_DIM)
    attn_s = _attention_sample(q_s, k_s, v_s, ck, cv, lam_vecs, g_sub, lambda_init, ss)
    y_sample = finish(xs2, yconv_s, attn_s).reshape(bs, ss, D_MODEL)

    k_prompt = k_p.reshape(depth, bp, sp, N_HEADS, HEAD_W)
    v_prompt = v_p.reshape(depth, bp, sp, N_HEADS, V_DIM)
    conv_prompt = conv_p.reshape(depth, bp, CONV_W - 1, CONV_DIM)
    k_sample = k_s.reshape(depth, bs, ss, N_HEADS, HEAD_W)
    v_sample = v_s.reshape(depth, bs, ss, N_HEADS, V_DIM)
    conv_sample = u_s.reshape(bs, ss, CONV_DIM)[:, ss - (CONV_W - 1):].reshape(
        depth, bs, CONV_W - 1, CONV_DIM)
    return (y_prompt, y_sample, k_prompt, v_prompt, conv_prompt, k_sample, v_sample, conv_sample)
```

```python
from functools import partial

import jax
import jax.numpy as jnp
import numpy as np
from jax import lax
from jax.experimental import pallas as pl
from jax.experimental.pallas import tpu as pltpu

D_MODEL = 1024
CHUNK = 64
CONV_DIM = 512
ATTN_DIM = 512
QK_DIM = 64
N_HEADS = 4
HEAD_W = 2 * QK_DIM
V_DIM = 128
F32_SUBLANES = 8
BF16_SUBLANES = 16
V_EXT = V_DIM + BF16_SUBLANES
LOG2E = 1.4426950408889634
CONV_W = 3
D_FF = 4096
LN_EPS = 1e-5
DEPTH = 1
ALPHA = (2 * DEPTH) ** 0.25
IN_WIDTH = 3 * CONV_DIM + 3 * ATTN_DIM

V7X_MXU_DIM = 256
V7X_VMEM_BYTES = 64 * 1024 * 1024

ROW_TILE = 1024
FINISH_GROUP = 256
Q_TILE = 2 * V7X_MXU_DIM
K_TILE = 2 * V7X_MXU_DIM
PROJ_TILE = 1024
TRIP_PAIRS = (4, 2, 1)
FF_CHUNK = 1024

_F32 = jnp.float32
_BF16 = jnp.bfloat16


def _vmem_limit(nbytes):
    assert nbytes <= V7X_VMEM_BYTES - (4 << 20)
    return int(nbytes)


def _dot(a, b):
    return jnp.dot(a, b, preferred_element_type=_F32)


def _dot_nt(a, b):
    return lax.dot_general(a, b, (((1,), (1,)), ((), ())), preferred_element_type=_F32)


def _dot_tn(a, b):
    return lax.dot_general(a, b, (((0,), (0,)), ((), ())), preferred_element_type=_F32)


def _div_pow2(x, n):
    assert n & (n - 1) == 0
    return lax.shift_right_logical(x, n.bit_length() - 1)


def _mod_pow2(x, n):
    assert n & (n - 1) == 0
    return lax.bitwise_and(x, n - 1)


def _layer_norm(x, g, b):
    mu = jnp.mean(x, axis=-1, keepdims=True)
    xc = x - mu
    var = jnp.mean(xc * xc, axis=-1, keepdims=True)
    return xc * lax.rsqrt(var + LN_EPS) * g + b


def _lambda_value(lq1, lk1, lq2, lk2, lambda_init):
    s1 = jnp.sum(lq1 * lk1, axis=-1, keepdims=True)
    s2 = jnp.sum(lq2 * lk2, axis=-1, keepdims=True)
    return jnp.exp(s1) - jnp.exp(s2) + lambda_init


def _project_columns(x_ref, w_ref):
    xb = x_ref[...].astype(_BF16)
    return [_dot(xb, w_ref[:, g * CONV_DIM:(g + 1) * CONV_DIM]) for g in range(IN_WIDTH // CONV_DIM)]


def _conv_from_taps(u, u1, u2, cw_ref):
    return cw_ref[0:1, :] * u2 + cw_ref[1:2, :] * u1 + cw_ref[2:3, :] * u


def _proj_prompt_kernel(x_ref, w_ref, cw_ref, wo_ref, w1_ref, w2_ref,
                        yconv_ref, kf_ref, vf_ref, qa_ref, qb_ref, ka_ref, kb_ref, vt_ref, cstate_ref,
                        wo_b_ref, w1_b_ref, w2_b_ref, carry_ref):
    i = pl.program_id(0)

    wo_b_ref[...] = wo_ref[...].astype(_BF16)
    w1_b_ref[...] = w1_ref[...].astype(_BF16)
    w2_b_ref[...] = w2_ref[...].astype(_BF16)

    @pl.when(i == 0)
    def _():
        carry_ref[...] = jnp.zeros_like(carry_ref)

    gate, cc, hh, q, k, v = _project_columns(x_ref, w_ref)
    u = cc * hh
    rows = lax.broadcasted_iota(jnp.int32, u.shape, 0)
    prev2 = carry_ref[0:1, :]
    prev1 = carry_ref[1:2, :]
    u1 = jnp.where(rows == 0, prev1, pltpu.roll(u, 1, 0))
    u2 = jnp.where(rows == 0, prev2, jnp.where(rows == 1, prev1, pltpu.roll(u, 2, 0)))
    yconv_ref[...] = (gate * _conv_from_taps(u, u1, u2, cw_ref)).astype(_BF16)
    tail = u[PROJ_TILE - (CONV_W - 1):, :]
    carry_ref[0:CONV_W - 1, :] = tail
    cstate_ref[...] = tail

    for h in range(N_HEADS):
        sl = slice(h * HEAD_W, (h + 1) * HEAD_W)
        kf_ref[pl.ds(h, PROJ_TILE, stride=N_HEADS), :] = k[:, sl]
        vf_ref[pl.ds(h, PROJ_TILE, stride=N_HEADS), :] = v[:, sl]
    qs = q * (QK_DIM ** -0.5 * LOG2E)
    lane = lax.broadcasted_iota(jnp.int32, (PROJ_TILE, HEAD_W), 1)
    first = lane < QK_DIM
    ones_row = jnp.where(lax.broadcasted_iota(jnp.int32, (BF16_SUBLANES, K_TILE), 0) == 0,
                         1.0, 0.0).astype(_BF16)
    chunks_per_ktile = K_TILE // CHUNK
    assert QK_DIM // chunks_per_ktile >= 2
    row = lax.broadcasted_iota(jnp.int32, (PROJ_TILE, HEAD_W), 0)
    key_chunk = _div_pow2(_mod_pow2(row, K_TILE), CHUNK)
    lane_chunk = _mod_pow2(lane, chunks_per_ktile)
    chunk_onehot = jnp.where(lane_chunk == key_chunk, 1.0, 0.0)
    q_tile = _div_pow2(i * PROJ_TILE + row, Q_TILE)
    tile_offset = _mod_pow2(q_tile, K_TILE // Q_TILE) * (Q_TILE // CHUNK)
    query_chunk = _div_pow2(_mod_pow2(row, Q_TILE), CHUNK) + tile_offset
    mask_weight = jnp.where(lane_chunk > query_chunk, float(jnp.finfo(_BF16).min), 0.0)
    for h in range(N_HEADS):
        sl = slice(h * HEAD_W, (h + 1) * HEAD_W)
        qh = qs[:, sl]
        kh = k[:, sl]
        qa_ref[h, 0] = jnp.where(first, qh, 0.0).astype(_BF16)
        qb_ref[h, 0] = jnp.where(first, 0.0, qh).astype(_BF16)
        qa_ref[h, 1] = jnp.where(first, qh, mask_weight).astype(_BF16)
        qb_ref[h, 1] = jnp.where(first, mask_weight, qh).astype(_BF16)
        ka_ref[h] = jnp.where(first, kh, chunk_onehot).astype(_BF16)
        kb_ref[h] = jnp.where(first, chunk_onehot, kh).astype(_BF16)
        for s in range(PROJ_TILE // K_TILE):
            vt_ref[h, s, :V_DIM, :] = v[s * K_TILE:(s + 1) * K_TILE, sl].T.astype(_BF16)
            vt_ref[h, s, V_DIM:, :] = ones_row


def _proj_sample_kernel(x_ref, w_ref, cw_ref, h1_ref, h2_ref,
                        yconv_ref, q_ref, k_ref, v_ref, u_ref, *, period):
    gate, cc, hh, q, k, v = _project_columns(x_ref, w_ref)
    u = cc * hh
    t = _mod_pow2(lax.broadcasted_iota(jnp.int32, u.shape, 0), period)
    u1 = jnp.where(t < 1, h1_ref[...], pltpu.roll(u, 1, 0))
    u2 = jnp.where(t < 2, h2_ref[...], pltpu.roll(u, 2, 0))
    yconv_ref[...] = (gate * _conv_from_taps(u, u1, u2, cw_ref)).astype(_BF16)
    q_ref[...] = q * (QK_DIM ** -0.5)
    k_ref[...] = k
    v_ref[...] = v
    u_ref[...] = u


def _project_prompt(x, w_in, conv_w, later_weights):
    s = x.shape[0]
    assert s % PROJ_TILE == 0 and PROJ_TILE % K_TILE == 0
    n = s // PROJ_TILE
    row = lambda i: (i, 0)
    head_row = lambda i: (0, i, 0)
    for w in later_weights:
        assert w.shape[0] % (n * BF16_SUBLANES) == 0
    weight_specs = [pl.BlockSpec((w.shape[0] // n, w.shape[1]), row) for w in later_weights]
    out_shape = (
        jax.ShapeDtypeStruct((s, CONV_DIM), _BF16),
        jax.ShapeDtypeStruct((s * N_HEADS, HEAD_W), _F32),
        jax.ShapeDtypeStruct((s * N_HEADS, V_DIM), _F32),
        jax.ShapeDtypeStruct((N_HEADS, 2, s, HEAD_W), _BF16),
        jax.ShapeDtypeStruct((N_HEADS, 2, s, HEAD_W), _BF16),
        jax.ShapeDtypeStruct((N_HEADS, s, HEAD_W), _BF16),
        jax.ShapeDtypeStruct((N_HEADS, s, HEAD_W), _BF16),
        jax.ShapeDtypeStruct((N_HEADS, s // K_TILE, V_EXT, K_TILE), _BF16),
        jax.ShapeDtypeStruct((CONV_W - 1, CONV_DIM), _F32),
    ) + tuple(jax.ShapeDtypeStruct(w.shape, _BF16) for w in later_weights)
    out_specs = (
        pl.BlockSpec((PROJ_TILE, CONV_DIM), row),
        pl.BlockSpec((PROJ_TILE * N_HEADS, HEAD_W), row),
        pl.BlockSpec((PROJ_TILE * N_HEADS, V_DIM), row),
        pl.BlockSpec((N_HEADS, 2, PROJ_TILE, HEAD_W), lambda i: (0, 0, i, 0)),
        pl.BlockSpec((N_HEADS, 2, PROJ_TILE, HEAD_W), lambda i: (0, 0, i, 0)),
        pl.BlockSpec((N_HEADS, PROJ_TILE, HEAD_W), head_row),
        pl.BlockSpec((N_HEADS, PROJ_TILE, HEAD_W), head_row),
        pl.BlockSpec((N_HEADS, PROJ_TILE // K_TILE, V_EXT, K_TILE), lambda i: (0, i, 0, 0)),
        pl.BlockSpec((CONV_W - 1, CONV_DIM), lambda i: (0, 0)),
    ) + tuple(weight_specs)
    return pl.pallas_call(
        _proj_prompt_kernel,
        grid=(n,),
        in_specs=[
            pl.BlockSpec((PROJ_TILE, D_MODEL), row),
            pl.BlockSpec((D_MODEL, IN_WIDTH), lambda i: (0, 0)),
            pl.BlockSpec((CONV_W, CONV_DIM), lambda i: (0, 0)),
        ] + weight_specs,
        out_specs=out_specs,
        out_shape=out_shape,
        scratch_shapes=[pltpu.VMEM((8, CONV_DIM), _F32)],
        compiler_params=pltpu.CompilerParams(
            dimension_semantics=("arbitrary",), vmem_limit_bytes=_vmem_limit(56 << 20)),
        name="project_prompt",
    )(x, w_in, conv_w, *later_weights)


def _project_sample(x, w_in, conv_w, hist1, hist2, period):
    r = x.shape[0]
    full = lambda shape: pl.BlockSpec(shape, lambda i: (0,) * len(shape))
    wide = (r, CONV_DIM)
    return pl.pallas_call(
        partial(_proj_sample_kernel, period=period),
        grid=(1,),
        in_specs=[full((r, D_MODEL)), full((D_MODEL, IN_WIDTH)), full((CONV_W, CONV_DIM)),
                  full(wide), full(wide)],
        out_specs=(full(wide),) * 5,
        out_shape=(jax.ShapeDtypeStruct(wide, _BF16),) + (jax.ShapeDtypeStruct(wide, _F32),) * 4,
        compiler_params=pltpu.CompilerParams(
            dimension_semantics=("arbitrary",), vmem_limit_bytes=_vmem_limit(48 << 20)),
        name="project_sample",
    )(x, w_in, conv_w, hist1, hist2)


def _attn_prompt_kernel(qa_ref, qb_ref, qa_next_ref, qb_next_ref, ka_ref, kb_ref, vt_ref,
                        lq1_ref, lk1_ref, lq2_ref, lk2_ref, g_ref, o_ref,
                        s_first, s_a, s_b, cmax_first, cmax_a, cmax_b, m_ref, acc_ref, *, lambda_init):
    i = pl.program_id(1)
    ratio = K_TILE // Q_TILE
    k_refs = (ka_ref, kb_ref)
    first_buf = (s_first, cmax_first)
    bufs = ((s_a, cmax_a), (s_b, cmax_b))
    n_full = _div_pow2(i, ratio)
    this_q = ((qa_ref, qb_ref), n_full)
    next_tile = jnp.minimum(i + 1, pl.num_programs(1) - 1)
    next_q = ((qa_next_ref, qb_next_ref), _div_pow2(next_tile, ratio))

    def col_max(st):
        part = jnp.max(st.reshape(K_TILE // F32_SUBLANES, F32_SUBLANES, Q_TILE), axis=0)
        return jnp.max(part, axis=0, keepdims=True)

    def scores(t, buf, query=this_q):
        q_refs, last_tile = query
        s_ref, cmax_ref = buf
        rows = pl.ds(pl.multiple_of(t * K_TILE, K_TILE), K_TILE)
        variant = (t == last_tile).astype(jnp.int32)
        for c in range(2):
            st = _dot_nt(k_refs[c][0, rows, :], q_refs[c][0, variant])
            s_ref[c] = st
            cmax_ref[c] = col_max(st)

    def consume(t, buf):
        s_ref, cmax_ref = buf
        vt = vt_ref[0, t]
        for c in range(2):
            m = m_ref[c]
            m_new = jnp.maximum(m, cmax_ref[c])
            alpha = jnp.exp2(m - m_new)
            p = jnp.exp2(s_ref[c] - m_new).astype(_BF16)
            acc_ref[c] = alpha * acc_ref[c] + _dot(vt, p)
            m_ref[c] = m_new

    m_ref[...] = jnp.full(m_ref.shape, -jnp.inf, _F32)
    acc_ref[...] = jnp.zeros(acc_ref.shape, _F32)

    @pl.when(i == 0)
    def _():
        scores(0, first_buf)

    @pl.when(n_full == 0)
    def _():
        consume(0, first_buf)
        scores(0, first_buf, next_q)

    @pl.when(n_full > 0)
    def _():
        scores(1, bufs[0])
        consume(0, first_buf)

    def pairs(t, count):
        for k in range(count):
            scores(t + 2 * k + 1, bufs[1])
            consume(t + 2 * k, bufs[0])
            scores(t + 2 * k + 2, bufs[0])
            consume(t + 2 * k + 1, bufs[1])

    n_pairs = _div_pow2(jnp.maximum(n_full - 1, 0), 2)
    pairs_done = 0
    for size in TRIP_PAIRS:
        n_trips = _div_pow2(n_pairs - pairs_done, size)

        def trip(p, carry, size=size, first=pairs_done):
            pairs(1 + 2 * (first + p * size), size)
            return carry

        lax.fori_loop(0, n_trips, trip, 0)
        pairs_done = pairs_done + n_trips * size

    t0 = 1 + 2 * n_pairs
    two_left = n_full > t0

    @pl.when(two_left)
    def _():
        scores(n_full, bufs[1])
        consume(t0, bufs[0])
        scores(0, first_buf, next_q)
        consume(n_full, bufs[1])

    @pl.when(n_full == t0)
    def _():
        scores(0, first_buf, next_q)
        consume(t0, bufs[0])

    lam =_lambda_value(lq1_ref[...], lk1_ref[...], lq2_ref[...], lk2_ref[...], lambda_init)
    acc0 = acc_ref[0]
    acc1 = acc_ref[1]
    l0 = acc0[V_DIM:V_DIM + 1, :]
    l1 = acc1[V_DIM:V_DIM + 1, :]
    o = acc0[:V_DIM] / l0 - lam * (acc1[:V_DIM] / l1)
    ms = jnp.mean(o * o, axis=0, keepdims=True)
    on = o * lax.rsqrt(ms + LN_EPS)
    on = on.T * g_ref[...] * (1.0 - lambda_init)
    o_ref[...] = on.astype(_BF16)


def _attention_prompt(qa, qb, ka, kb, vt, lam_vecs, subln_g, lambda_init):
    s = ka.shape[1]
    assert K_TILE % Q_TILE == 0 and s % K_TILE == 0 and Q_TILE % CHUNK == 0
    nq = s // Q_TILE
    vec = pl.BlockSpec((1, QK_DIM), lambda h, i: (0, 0))
    q_tile = pl.BlockSpec((1, 2, Q_TILE, HEAD_W), lambda h, i: (h, 0, i, 0))
    q_next = pl.BlockSpec((1, 2, Q_TILE, HEAD_W), lambda h, i: (h, 0, jnp.minimum(i + 1, nq - 1), 0))
    return pl.pallas_call(
        partial(_attn_prompt_kernel, lambda_init=lambda_init),
        grid=(N_HEADS, nq),
        in_specs=[
            q_tile, q_tile, q_next, q_next,
            pl.BlockSpec((1, s, HEAD_W), lambda h, i: (h, 0, 0)),
            pl.BlockSpec((1, s, HEAD_W), lambda h, i: (h, 0, 0)),
            pl.BlockSpec((1, s // K_TILE, V_EXT, K_TILE), lambda h, i: (h, 0, 0, 0)),
            vec, vec, vec, vec,
            pl.BlockSpec((1, V_DIM), lambda h, i: (0, 0)),
        ],
        out_specs=pl.BlockSpec((Q_TILE, V_DIM), lambda h, i: (i, h)),
        out_shape=jax.ShapeDtypeStruct((s, ATTN_DIM), _BF16),
        scratch_shapes=[
            pltpu.VMEM((2, K_TILE, Q_TILE), _F32), pltpu.VMEM((2, K_TILE, Q_TILE), _F32),
            pltpu.VMEM((2, K_TILE, Q_TILE), _F32),
            pltpu.VMEM((2, 1, Q_TILE), _F32), pltpu.VMEM((2, 1, Q_TILE), _F32),
            pltpu.VMEM((2, 1, Q_TILE), _F32),
            pltpu.VMEM((2, 1, Q_TILE), _F32), pltpu.VMEM((2, V_EXT, Q_TILE), _F32),
        ],
        compiler_params=pltpu.CompilerParams(
            dimension_semantics=("arbitrary", "arbitrary"), vmem_limit_bytes=_vmem_limit(48 << 20)),
        name="attention_prompt",
    )(qa, qb, qa, qb, ka, kb, vt, *lam_vecs, subln_g)


def _attn_sample_kernel(q_ref, kn_ref, vn_ref, ck_ref, cv_ref, lq1_ref, lk1_ref, lq2_ref, lk2_ref,
                        g_ref, o_ref, *, lambda_init, past_len, n_new):
    group = 2 * N_HEADS
    width = group * n_new
    assert width == HEAD_W
    qb = q_ref[...].astype(_BF16)
    sel_r = lax.broadcasted_iota(jnp.int32, (n_new, width), 0)
    sel_c = lax.broadcasted_iota(jnp.int32, (n_new, width), 1)
    spread = jnp.where(_mod_pow2(sel_c, n_new) == sel_r, 1.0, 0.0).astype(_BF16)
    q_all = _dot_tn(qb, spread)
    blk_r = _div_pow2(lax.broadcasted_iota(jnp.int32, (ATTN_DIM, width), 0), QK_DIM)
    blk_c = _div_pow2(lax.broadcasted_iota(jnp.int32, (ATTN_DIM, width), 1), n_new)
    q_bd = jnp.where(blk_r == blk_c, q_all, 0.0).astype(_BF16)

    half = past_len // 2

    def heads_on_lanes(ref, part):
        return jnp.concatenate(
            [ref[0, pl.ds(part * half * N_HEADS + h, half, stride=N_HEADS), :].astype(_BF16)
             for h in range(N_HEADS)], axis=-1)

    s_old = jnp.concatenate([_dot(heads_on_lanes(ck_ref, part), q_bd) for part in range(2)],
                            axis=0)
    s_new = _dot(kn_ref[...].astype(_BF16), q_bd)

    def chunk_mask(shape, k_off):
        k_pos = k_off + lax.broadcasted_iota(jnp.int32, shape, 0)
        q_pos = past_len + _mod_pow2(lax.broadcasted_iota(jnp.int32, shape, 1), n_new)
        return _div_pow2(k_pos, CHUNK) <= _div_pow2(q_pos, CHUNK)

    s_old = jnp.where(chunk_mask(s_old.shape, 0), s_old, -jnp.inf)
    s_new = jnp.where(chunk_mask(s_new.shape, past_len), s_new, -jnp.inf)
    m = jnp.maximum(jnp.max(s_old, axis=0, keepdims=True), jnp.max(s_new, axis=0, keepdims=True))
    p_old = jnp.exp(s_old - m)
    p_new = jnp.exp(s_new - m)
    inv_l = 1.0 / (jnp.sum(p_old, axis=0, keepdims=True) + jnp.sum(p_new, axis=0, keepdims=True))
    a_old = (p_old * inv_l).astype(_BF16)
    a_new = (p_new * inv_l).astype(_BF16)
    pv = (_dot_tn(a_old[:half], heads_on_lanes(cv_ref, 0)) + _dot_tn(a_old[half:], heads_on_lanes(cv_ref, 1))
          + _dot_tn(a_new, vn_ref[...].astype(_BF16)))

    lam = _lambda_value(lq1_ref[...], lk1_ref[...], lq2_ref[...], lk2_ref[...], lambda_init)
    g = g_ref[...]
    for h in range(N_HEADS):
        cols = slice(h * V_DIM, (h + 1) * V_DIM)
        r0 = h * 2 * n_new
        o = pv[r0:r0 + n_new, cols] - lam * pv[r0 + n_new:r0 + 2 * n_new, cols]
        ms = jnp.mean(o * o, axis=-1, keepdims=True)
        o_ref[:, cols] = (o * lax.rsqrt(ms + LN_EPS) * g * (1.0 - lambda_init)).astype(_BF16)


def _attention_sample(q, k_new, v_new, cache_k, cache_v, lam_vecs, subln_g, lambda_init, n_new):
    nb, rows_per_stream = cache_k.shape[:2]
    past_len = rows_per_stream // N_HEADS
    cache_block = pl.BlockSpec((1, rows_per_stream, HEAD_W), lambda b: (b, 0, 0))
    rows = lambda b: (b, 0)
    vec = pl.BlockSpec((1, QK_DIM), lambda b: (0, 0))
    return pl.pallas_call(
        partial(_attn_sample_kernel, lambda_init=lambda_init, past_len=past_len, n_new=n_new),
        grid=(nb,),
        in_specs=[
            pl.BlockSpec((n_new, ATTN_DIM), rows),
            pl.BlockSpec((n_new, ATTN_DIM), rows),
            pl.BlockSpec((n_new, ATTN_DIM), rows),
            cache_block, cache_block,
            vec, vec, vec, vec,
            pl.BlockSpec((1, V_DIM), lambda b: (0, 0)),
        ],
        out_specs=pl.BlockSpec((n_new, ATTN_DIM), rows),
        out_shape=jax.ShapeDtypeStruct((nb * n_new, ATTN_DIM), _BF16),
        compiler_params=pltpu.CompilerParams(
            dimension_semantics=("arbitrary",), vmem_limit_bytes=_vmem_limit(40 << 20)),
        name="attention_sample",
    )(q, k_new, v_new, cache_k, cache_v, *lam_vecs, subln_g)


def _finish_kernel(x_ref, yconv_ref, attn_ref, wo_ref, g1_ref, b1_ref, w1_ref, w2_ref, g2_ref, b2_ref,
                   y_ref):
    tile = x_ref.shape[0]
    group = min(tile, FINISH_GROUP)
    groups = [slice(r0, r0 + group) for r0 in range(0, tile, group)]
    pre = [ALPHA * x_ref[rows, :]
           + _dot(jnp.concatenate([yconv_ref[rows, :], attn_ref[rows, :]], axis=-1), wo_ref[...])
           for rows in groups]
    for rows, z in zip(groups, pre):
        x1 = _layer_norm(z, g1_ref[...], b1_ref[...])
        x1b = x1.astype(_BF16)
        ff = jnp.zeros_like(x1)
        for c in range(D_FF // FF_CHUNK):
            cols = slice(c * FF_CHUNK, (c + 1) * FF_CHUNK)
            hdn = jnp.square(jnp.maximum(_dot(x1b, w1_ref[:, cols]), 0.0))
            ff = ff + _dot(hdn.astype(_BF16), w2_ref[cols, :])
        y_ref[rows, :] = _layer_norm(ALPHA * x1 + ff, g2_ref[...], b2_ref[...])


def _finish(x, yconv, attn, w_out, g1, b1, w_ff1, w_ff2, g2, b2):
    r = x.shape[0]
    tile = min(ROW_TILE, r)
    assert r % tile == 0
    row = lambda i: (i, 0)
    const = lambda shape: pl.BlockSpec(shape, lambda i: (0, 0))
    return pl.pallas_call(
        _finish_kernel,
        grid=(r // tile,),
        in_specs=[
            pl.BlockSpec((tile, D_MODEL), row),
            pl.BlockSpec((tile, CONV_DIM), row),
            pl.BlockSpec((tile, ATTN_DIM), row),
            const((D_MODEL, D_MODEL)), const((1, D_MODEL)), const((1, D_MODEL)),
            const((D_MODEL, D_FF)), const((D_FF, D_MODEL)), const((1, D_MODEL)), const((1, D_MODEL)),
        ],
        out_specs=pl.BlockSpec((tile, D_MODEL), row),
        out_shape=jax.ShapeDtypeStruct((r, D_MODEL), _F32),
        compiler_params=pltpu.CompilerParams(
            dimension_semantics=("arbitrary",), vmem_limit_bytes=_vmem_limit(56 << 20)),
        name="finish_layer",
    )(x, yconv, attn, w_out, g1, b1, w_ff1, w_ff2, g2, b2)


def kernel(x_prompt, x_sample, cache_k, cache_v, state_conv, w_in, conv_w, lambda_q1, lambda_k1,
           lambda_q2, lambda_k2, subln_g, w_out, ln1_g, ln1_b, w_ff1, w_ff2, ln2_g, ln2_b):
    bp, sp, _ = x_prompt.shape
    bs, ss, _ = x_sample.shape
    depth = w_in.shape[0]
    assert bp == 1 and depth == 1
    past_len = cache_k.shape[2]
    l = 0
    lambda_init = 0.8 - 0.6 * float(np.exp(-0.3 * l))

    w_in_b = w_in[l].astype(_BF16)
    lam_vecs = tuple(v[l].reshape(1, QK_DIM) for v in (lambda_q1, lambda_k1, lambda_q2, lambda_k2))
    g_sub = subln_g[l].reshape(1, V_DIM)
    ln = tuple(v[l].reshape(1, D_MODEL) for v in (ln1_g, ln1_b, ln2_g, ln2_b))

    xp2 = x_prompt.reshape(sp, D_MODEL)
    (yconv_p, k_p, v_p, qa, qb, ka, kb, vt, conv_p, w_out_b, w_ff1_b, w_ff2_b) = _project_prompt(
        xp2, w_in_b, conv_w[l], (w_out[l], w_ff1[l], w_ff2[l]))

    def finish(x2d, yconv, attn):
        return _finish(x2d, yconv, attn, w_out_b, ln[0], ln[1], w_ff1_b, w_ff2_b, ln[2], ln[3])

    attn_p = _attention_prompt(qa, qb, ka, kb, vt, lam_vecs, g_sub, lambda_init)
    y_prompt = finish(xp2, yconv_p, attn_p).reshape(bp, sp, D_MODEL)

    xs2 = x_sample.reshape(bs * ss, D_MODEL)
    st = state_conv[l].astype(_F32)
    pad = lambda a: jnp.pad(a, ((0, 0), (0, ss - a.shape[1]), (0, 0))).reshape(bs * ss, CONV_DIM)
    hist2 = pad(st)
    hist1 = pad(st[:, 1:])
    yconv_s, q_s, k_s, v_s, u_s = _project_sample(xs2, w_in_b, conv_w[l], hist1, hist2, ss)
    ck = cache_k.reshape(depth * bs, past_len * N_HEADS, HEAD_W)
    cv = cache_v.reshape(depth * bs, past_len * N_HEADS, V_DIM)
    attn_s = _attention_sample(q_s, k_s, v_s, ck, cv, lam_vecs, g_sub, lambda_init, ss)
    y_sample = finish(xs2, yconv_s, attn_s).reshape(bs, ss, D_MODEL)

    k_prompt = k_p.reshape(depth, bp, sp, N_HEADS, HEAD_W)
    v_prompt = v_p.reshape(depth, bp, sp, N_HEADS, V_DIM)
    conv_prompt = conv_p.reshape(depth, bp, CONV_W - 1, CONV_DIM)
    k_sample = k_s.reshape(depth, bs, ss, N_HEADS, HEAD_W)
    v_sample = v_s.reshape(depth, bs, ss, N_HEADS, V_DIM)
    conv_sample = u_s.reshape(bs, ss, CONV_DIM)[:, ss - (CONV_W - 1):].reshape(
        depth, bs, CONV_W - 1, CONV_DIM)
    return (y_prompt, y_sample, k_prompt, v_prompt, conv_prompt, k_sample, v_sample, conv_sample)
```

```python
from functools import partial

import jax
import jax.numpy as jnp
import numpy as np
from jax import lax
from jax.experimental import pallas as pl
from jax.experimental.pallas import tpu as pltpu

D_MODEL = 1024
CHUNK = 64
CONV_DIM = 512
ATTN_DIM = 512
QK_DIM = 64
N_HEADS = 4
HEAD_W = 2 * QK_DIM
V_DIM = 128
F32_SUBLANES = 8
BF16_SUBLANES = 16
V_EXT = V_DIM + BF16_SUBLANES
LOG2E = 1.4426950408889634
CONV_W = 3
D_FF = 4096
LN_EPS = 1e-5
DEPTH = 1
ALPHA = (2 * DEPTH) ** 0.25
IN_WIDTH = 3 * CONV_DIM + 3 * ATTN_DIM

V7X_MXU_DIM = 256
V7X_VMEM_BYTES = 64 * 1024 * 1024

ROW_TILE = 1024
FINISH_GROUP = 256
Q_TILE = 2 * V7X_MXU_DIM
K_TILE = 2 * V7X_MXU_DIM
PROJ_TILE = 1024
TRIP_PAIRS = (8, 4, 2, 1)
FF_CHUNK = 1024

_F32 = jnp.float32
_BF16 = jnp.bfloat16


def _vmem_limit(nbytes):
    assert nbytes <= V7X_VMEM_BYTES - (4 << 20)
    return int(nbytes)


def _dot(a, b):
    return jnp.dot(a, b, preferred_element_type=_F32)


def _dot_nt(a, b):
    return lax.dot_general(a, b, (((1,), (1,)), ((), ())), preferred_element_type=_F32)


def _dot_tn(a, b):
    return lax.dot_general(a, b, (((0,), (0,)), ((), ())), preferred_element_type=_F32)


def _div_pow2(x, n):
    assert n & (n - 1) == 0
    return lax.shift_right_logical(x, n.bit_length() - 1)


def _mod_pow2(x, n):
    assert n & (n - 1) == 0
    return lax.bitwise_and(x, n - 1)


def _layer_norm(x, g, b):
    mu = jnp.mean(x, axis=-1, keepdims=True)
    xc = x - mu
    var = jnp.mean(xc * xc, axis=-1, keepdims=True)
    return xc * lax.rsqrt(var + LN_EPS) * g + b


def _lambda_value(lq1, lk1, lq2, lk2, lambda_init):
    s1 = jnp.sum(lq1 * lk1, axis=-1, keepdims=True)
    s2 = jnp.sum(lq2 * lk2, axis=-1, keepdims=True)
    return jnp.exp(s1) - jnp.exp(s2) + lambda_init


def _project_columns(x_ref, w_ref):
    xb = x_ref[...].astype(_BF16)
    groups = {g: _dot(xb, w_ref[:, g * CONV_DIM:(g + 1) * CONV_DIM]) for g in (1, 2, 0, 5, 4, 3)}
    return [groups[g] for g in range(IN_WIDTH // CONV_DIM)]


def _conv_from_taps(u, u1, u2, cw_ref):
    return cw_ref[0:1, :] * u2 + cw_ref[1:2, :] * u1 + cw_ref[2:3, :] * u


def _proj_prompt_kernel(x_ref, w_ref, cw_ref, wo_ref, w1_ref, w2_ref,
                        yconv_ref, kf_ref, vf_ref, qa_ref, qb_ref, ka_ref, kb_ref, vt_ref, cstate_ref,
                        wo_b_ref, w1_b_ref, w2_b_ref, carry_ref):
    i = pl.program_id(0)

    wo_b_ref[...] = wo_ref[...].astype(_BF16)
    w1_b_ref[...] = w1_ref[...].astype(_BF16)
    w2_b_ref[...] = w2_ref[...].astype(_BF16)

    @pl.when(i == 0)
    def _():
        carry_ref[...] = jnp.zeros_like(carry_ref)

    gate, cc, hh, q, k, v = _project_columns(x_ref, w_ref)
    u = cc * hh
    rows = lax.broadcasted_iota(jnp.int32, u.shape, 0)
    prev2 = carry_ref[0:1, :]
    prev1 = carry_ref[1:2, :]
    u1 = jnp.where(rows == 0, prev1, pltpu.roll(u, 1, 0))
    u2 = jnp.where(rows == 0, prev2, jnp.where(rows == 1, prev1, pltpu.roll(u, 2, 0)))
    yconv_ref[...] = (gate * _conv_from_taps(u, u1, u2, cw_ref)).astype(_BF16)
    tail = u[PROJ_TILE - (CONV_W - 1):, :]
    carry_ref[0:CONV_W - 1, :] = tail
    cstate_ref[...] = tail

    for h in range(N_HEADS):
        sl = slice(h * HEAD_W, (h + 1) * HEAD_W)
        kf_ref[pl.ds(h, PROJ_TILE, stride=N_HEADS), :] = k[:, sl]
        vf_ref[pl.ds(h, PROJ_TILE, stride=N_HEADS), :] = v[:, sl]
    qs = q * (QK_DIM ** -0.5 * LOG2E)
    lane = lax.broadcasted_iota(jnp.int32, (PROJ_TILE, HEAD_W), 1)
    first = lane < QK_DIM
    ones_row = jnp.where(lax.broadcasted_iota(jnp.int32, (BF16_SUBLANES, K_TILE), 0) == 0,
                         1.0, 0.0).astype(_BF16)
    chunks_per_ktile = K_TILE // CHUNK
    assert QK_DIM // chunks_per_ktile >= 2
    row = lax.broadcasted_iota(jnp.int32, (PROJ_TILE, HEAD_W), 0)
    key_chunk = _div_pow2(_mod_pow2(row, K_TILE), CHUNK)
    lane_chunk = _mod_pow2(lane, chunks_per_ktile)
    chunk_onehot = jnp.where(lane_chunk == key_chunk, 1.0, 0.0)
    q_tile = _div_pow2(i * PROJ_TILE + row, Q_TILE)
    tile_offset = _mod_pow2(q_tile, K_TILE // Q_TILE) * (Q_TILE // CHUNK)
    query_chunk = _div_pow2(_mod_pow2(row, Q_TILE), CHUNK) + tile_offset
    mask_weight = jnp.where(lane_chunk > query_chunk, float(jnp.finfo(_BF16).min), 0.0)
    for h in range(N_HEADS):
        sl = slice(h * HEAD_W, (h + 1) * HEAD_W)
        qh = qs[:, sl]
        kh = k[:, sl]
        qa_ref[h, 0] = jnp.where(first, qh, 0.0).astype(_BF16)
        qb_ref[h, 0] = jnp.where(first, 0.0, qh).astype(_BF16)
        qa_ref[h, 1] = jnp.where(first, qh, mask_weight).astype(_BF16)
        qb_ref[h, 1] = jnp.where(first, mask_weight, qh).astype(_BF16)
        ka_ref[h] = jnp.where(first, kh, chunk_onehot).astype(_BF16)
        kb_ref[h] = jnp.where(first, chunk_onehot, kh).astype(_BF16)
        for s in range(PROJ_TILE // K_TILE):
            vt_ref[h, s, :V_DIM, :] = v[s * K_TILE:(s + 1) * K_TILE, sl].T.astype(_BF16)
            vt_ref[h, s, V_DIM:, :] = ones_row


def _proj_sample_kernel(x_ref, w_ref, cw_ref, h1_ref, h2_ref,
                        yconv_ref, q_ref, k_ref, v_ref, u_ref, *, period):
    gate, cc, hh, q, k, v = _project_columns(x_ref, w_ref)
    u = cc * hh
    t = _mod_pow2(lax.broadcasted_iota(jnp.int32, u.shape, 0), period)
    u1 = jnp.where(t < 1, h1_ref[...], pltpu.roll(u, 1, 0))
    u2 = jnp.where(t < 2, h2_ref[...], pltpu.roll(u, 2, 0))
    yconv_ref[...] = (gate * _conv_from_taps(u, u1, u2, cw_ref)).astype(_BF16)
    q_ref[...] = q * (QK_DIM ** -0.5)
    k_ref[...] = k
    v_ref[...] = v
    u_ref[...] = u


def _project_prompt(x, w_in, conv_w, later_weights):
    s = x.shape[0]
    assert s % PROJ_TILE == 0 and PROJ_TILE % K_TILE == 0
    n = s // PROJ_TILE
    row = lambda i: (i, 0)
    head_row = lambda i: (0, i, 0)
    for w in later_weights:
        assert w.shape[0] % (n * BF16_SUBLANES) == 0
    weight_specs = [pl.BlockSpec((w.shape[0] // n, w.shape[1]), row) for w in later_weights]
    out_shape = (
        jax.ShapeDtypeStruct((s, CONV_DIM), _BF16),
        jax.ShapeDtypeStruct((s * N_HEADS, HEAD_W), _F32),
        jax.ShapeDtypeStruct((s * N_HEADS, V_DIM), _F32),
        jax.ShapeDtypeStruct((N_HEADS, 2, s, HEAD_W), _BF16),
        jax.ShapeDtypeStruct((N_HEADS, 2, s, HEAD_W), _BF16),
        jax.ShapeDtypeStruct((N_HEADS, s, HEAD_W), _BF16),
        jax.ShapeDtypeStruct((N_HEADS, s, HEAD_W), _BF16),
        jax.ShapeDtypeStruct((N_HEADS, s // K_TILE, V_EXT, K_TILE), _BF16),
        jax.ShapeDtypeStruct((CONV_W - 1, CONV_DIM), _F32),
    ) + tuple(jax.ShapeDtypeStruct(w.shape, _BF16) for w in later_weights)
    out_specs = (
        pl.BlockSpec((PROJ_TILE, CONV_DIM), row),
        pl.BlockSpec((PROJ_TILE * N_HEADS, HEAD_W), row),
        pl.BlockSpec((PROJ_TILE * N_HEADS, V_DIM), row),
        pl.BlockSpec((N_HEADS, 2, PROJ_TILE, HEAD_W), lambda i: (0, 0, i, 0)),
        pl.BlockSpec((N_HEADS, 2, PROJ_TILE, HEAD_W), lambda i: (0, 0, i, 0)),
        pl.BlockSpec((N_HEADS, PROJ_TILE, HEAD_W), head_row),
        pl.BlockSpec((N_HEADS, PROJ_TILE, HEAD_W), head_row),
        pl.BlockSpec((N_HEADS, PROJ_TILE // K_TILE, V_EXT, K_TILE), lambda i: (0, i, 0, 0)),
        pl.BlockSpec((CONV_W - 1, CONV_DIM), lambda i: (0, 0)),
    ) + tuple(weight_specs)
    return pl.pallas_call(
        _proj_prompt_kernel,
        grid=(n,),
        in_specs=[
            pl.BlockSpec((PROJ_TILE, D_MODEL), row),
            pl.BlockSpec((D_MODEL, IN_WIDTH), lambda i: (0, 0)),
            pl.BlockSpec((CONV_W, CONV_DIM), lambda i: (0, 0)),
        ] + weight_specs,
        out_specs=out_specs,
        out_shape=out_shape,
        scratch_shapes=[pltpu.VMEM((8, CONV_DIM), _F32)],
        compiler_params=pltpu.CompilerParams(
            dimension_semantics=("arbitrary",), vmem_limit_bytes=_vmem_limit(56 << 20)),
        name="project_prompt",
    )(x, w_in, conv_w, *later_weights)


def _project_sample(x, w_in, conv_w, hist1, hist2, period):
    r = x.shape[0]
    full = lambda shape: pl.BlockSpec(shape, lambda i: (0,) * len(shape))
    wide = (r, CONV_DIM)
    return pl.pallas_call(
        partial(_proj_sample_kernel, period=period),
        grid=(1,),
        in_specs=[full((r, D_MODEL)), full((D_MODEL, IN_WIDTH)), full((CONV_W, CONV_DIM)),
                  full(wide), full(wide)],
        out_specs=(full(wide),) * 5,
        out_shape=(jax.ShapeDtypeStruct(wide, _BF16),) + (jax.ShapeDtypeStruct(wide, _F32),) * 4,
        compiler_params=pltpu.CompilerParams(
            dimension_semantics=("arbitrary",), vmem_limit_bytes=_vmem_limit(48 << 20)),
        name="project_sample",
    )(x, w_in, conv_w, hist1, hist2)


def _attn_prompt_kernel(qa_ref, qb_ref, ka_ref, kb_ref, vt_ref, lq1_ref, lk1_ref, lq2_ref, lk2_ref,
                        g_ref, o_ref, s_even, s_odd, cmax_even, cmax_odd, m_ref, acc_ref,
                        *, lambda_init):
    i = pl.program_id(1)
    q_refs = (qa_ref, qb_ref)
    k_refs = (ka_ref, kb_ref)
    bufs = ((s_even, cmax_even), (s_odd, cmax_odd))
    n_full = _div_pow2(i, K_TILE // Q_TILE)

    def col_max(st):
        part = jnp.max(st.reshape(K_TILE // F32_SUBLANES, F32_SUBLANES, Q_TILE), axis=0)
        return jnp.max(part, axis=0, keepdims=True)

    def scores(t, buf):
        s_ref, cmax_ref = buf
        rows = pl.ds(pl.multiple_of(t * K_TILE, K_TILE), K_TILE)
        variant = (t == n_full).astype(jnp.int32)
        for c in range(2):
            st = _dot_nt(k_refs[c][0, rows, :], q_refs[c][0, variant])
            s_ref[c] = st
            cmax_ref[c] = col_max(st)

    def consume(t, buf):
        s_ref, cmax_ref = buf
        vt = vt_ref[0, t]
        for c in range(2):
            m = m_ref[c]
            m_new = jnp.maximum(m, cmax_ref[c])
            alpha = jnp.exp2(m - m_new)
            p = jnp.exp2(s_ref[c] - m_new).astype(_BF16)
            acc_ref[c] = alpha * acc_ref[c] + _dot(vt, p)
            m_ref[c] = m_new

    m_ref[...] = jnp.full(m_ref.shape, -jnp.inf, _F32)
    acc_ref[...] = jnp.zeros(acc_ref.shape, _F32)
    scores(0, bufs[0])

    def pairs(t, count):
        for k in range(count):
            scores(t + 2 * k + 1, bufs[1])
            consume(t + 2 * k, bufs[0])
            scores(t + 2 * k + 2, bufs[0])
            consume(t + 2 * k + 1, bufs[1])

    n_pairs = _div_pow2(n_full, 2)
    pairs_done = 0
    for size in TRIP_PAIRS:
        n_trips = _div_pow2(n_pairs - pairs_done, size)

        def trip(p, carry, size=size, first=pairs_done):
            pairs(2 * (first + p * size), size)
            return carry

        lax.fori_loop(0, n_trips, trip, 0)
        pairs_done = pairs_done + n_trips * size

    t0 = 2 * n_pairs
    two_left = n_full > t0

    @pl.when(two_left)
    def _():
        scores(n_full, bufs[1])
        consume(t0, bufs[0])
        consume(n_full, bufs[1])

    @pl.when(jnp.logical_not(two_left))
    def _():
        consume(t0, bufs[0])

    lam =_lambda_value(lq1_ref[...], lk1_ref[...], lq2_ref[...], lk2_ref[...], lambda_init)
    acc0 = acc_ref[0]
    acc1 = acc_ref[1]
    l0 = acc0[V_DIM:V_DIM + 1, :]
    l1 = acc1[V_DIM:V_DIM + 1, :]
    o = acc0[:V_DIM] / l0 - lam * (acc1[:V_DIM] / l1)
    ms = jnp.mean(o * o, axis=0, keepdims=True)
    on = o * lax.rsqrt(ms + LN_EPS)
    on = on.T * g_ref[...] * (1.0 - lambda_init)
    o_ref[...] = on.astype(_BF16)


def _attention_prompt(qa, qb, ka, kb, vt, lam_vecs, subln_g, lambda_init):
    s = ka.shape[1]
    assert K_TILE % Q_TILE == 0 and s % K_TILE == 0 and Q_TILE % CHUNK == 0
    nq = s // Q_TILE
    vec = pl.BlockSpec((1, QK_DIM), lambda h, i: (0, 0))
    q_tile = pl.BlockSpec((1, 2, Q_TILE, HEAD_W), lambda h, i: (h, 0, i, 0))
    return pl.pallas_call(
        partial(_attn_prompt_kernel, lambda_init=lambda_init),
        grid=(N_HEADS, nq),
        in_specs=[
            q_tile, q_tile,
            pl.BlockSpec((1, s, HEAD_W), lambda h, i: (h, 0, 0)),
            pl.BlockSpec((1, s, HEAD_W), lambda h, i: (h, 0, 0)),
            pl.BlockSpec((1, s // K_TILE, V_EXT, K_TILE), lambda h, i: (h, 0, 0, 0)),
            vec, vec, vec, vec,
            pl.BlockSpec((1, V_DIM), lambda h, i: (0, 0)),
        ],
        out_specs=pl.BlockSpec((Q_TILE, V_DIM), lambda h, i: (i, h)),
        out_shape=jax.ShapeDtypeStruct((s, ATTN_DIM), _BF16),
        scratch_shapes=[
            pltpu.VMEM((2, K_TILE, Q_TILE), _F32), pltpu.VMEM((2, K_TILE, Q_TILE), _F32),
            pltpu.VMEM((2, 1, Q_TILE), _F32), pltpu.VMEM((2, 1, Q_TILE), _F32),
            pltpu.VMEM((2, 1, Q_TILE), _F32), pltpu.VMEM((2, V_EXT, Q_TILE), _F32),
        ],
        compiler_params=pltpu.CompilerParams(
            dimension_semantics=("arbitrary", "arbitrary"), vmem_limit_bytes=_vmem_limit(48 << 20)),
        name="attention_prompt",
    )(qa, qb, ka, kb, vt, *lam_vecs, subln_g)


def _attn_sample_kernel(q_ref, kn_ref, vn_ref, ck_ref, cv_ref, lq1_ref, lk1_ref, lq2_ref, lk2_ref,
                        g_ref, o_ref, *, lambda_init, past_len, n_new):
    group = 2 * N_HEADS
    width = group * n_new
    assert width == HEAD_W
    qb = q_ref[...].astype(_BF16)
    sel_r = lax.broadcasted_iota(jnp.int32, (n_new, width), 0)
    sel_c = lax.broadcasted_iota(jnp.int32, (n_new, width), 1)
    spread = jnp.where(_mod_pow2(sel_c, n_new) == sel_r, 1.0, 0.0).astype(_BF16)
    q_all = _dot_tn(qb, spread)
    blk_r = _div_pow2(lax.broadcasted_iota(jnp.int32, (ATTN_DIM, width), 0), QK_DIM)
    blk_c = _div_pow2(lax.broadcasted_iota(jnp.int32, (ATTN_DIM, width), 1), n_new)
    q_bd = jnp.where(blk_r == blk_c, q_all, 0.0).astype(_BF16)

    half = past_len // 2

    def heads_on_lanes(ref, part):
        return jnp.concatenate(
            [ref[0, pl.ds(part * half * N_HEADS + h, half, stride=N_HEADS), :].astype(_BF16)
             for h in range(N_HEADS)], axis=-1)

    s_old = jnp.concatenate([_dot(heads_on_lanes(ck_ref, part), q_bd) for part in range(2)],
                            axis=0)
    s_new = _dot(kn_ref[...].astype(_BF16), q_bd)

    def chunk_mask(shape, k_off):
        k_pos = k_off + lax.broadcasted_iota(jnp.int32, shape, 0)
        q_pos = past_len + _mod_pow2(lax.broadcasted_iota(jnp.int32, shape, 1), n_new)
        return _div_pow2(k_pos, CHUNK) <= _div_pow2(q_pos, CHUNK)

    s_old = jnp.where(chunk_mask(s_old.shape, 0), s_old, -jnp.inf)
    s_new = jnp.where(chunk_mask(s_new.shape, past_len), s_new, -jnp.inf)
    m = jnp.maximum(jnp.max(s_old, axis=0, keepdims=True), jnp.max(s_new, axis=0, keepdims=True))
    p_old = jnp.exp(s_old - m)
    p_new = jnp.exp(s_new - m)
    inv_l = 1.0 / (jnp.sum(p_old, axis=0, keepdims=True) + jnp.sum(p_new, axis=0, keepdims=True))
    a_old = (p_old * inv_l).astype(_BF16)
    a_new = (p_new * inv_l).astype(_BF16)
    pv = (_dot_tn(a_old[:half], heads_on_lanes(cv_ref, 0)) + _dot_tn(a_old[half:], heads_on_lanes(cv_ref, 1))
          + _dot_tn(a_new, vn_ref[...].astype(_BF16)))

    lam = _lambda_value(lq1_ref[...], lk1_ref[...], lq2_ref[...], lk2_ref[...], lambda_init)
    g = g_ref[...]
    for h in range(N_HEADS):
        cols = slice(h * V_DIM, (h + 1) * V_DIM)
        r0 = h * 2 * n_new
        o = pv[r0:r0 + n_new, cols] - lam * pv[r0 + n_new:r0 + 2 * n_new, cols]
        ms = jnp.mean(o * o, axis=-1, keepdims=True)
        o_ref[:, cols] = (o * lax.rsqrt(ms + LN_EPS) * g * (1.0 - lambda_init)).astype(_BF16)


def _attention_sample(q, k_new, v_new, cache_k, cache_v, lam_vecs, subln_g, lambda_init, n_new):
    nb, rows_per_stream = cache_k.shape[:2]
    past_len = rows_per_stream // N_HEADS
    cache_block = pl.BlockSpec((1, rows_per_stream, HEAD_W), lambda b: (b, 0, 0))
    rows = lambda b: (b, 0)
    vec = pl.BlockSpec((1, QK_DIM), lambda b: (0, 0))
    return pl.pallas_call(
        partial(_attn_sample_kernel, lambda_init=lambda_init, past_len=past_len, n_new=n_new),
        grid=(nb,),
        in_specs=[
            pl.BlockSpec((n_new, ATTN_DIM), rows),
            pl.BlockSpec((n_new, ATTN_DIM), rows),
            pl.BlockSpec((n_new, ATTN_DIM), rows),
            cache_block, cache_block,
            vec, vec, vec, vec,
            pl.BlockSpec((1, V_DIM), lambda b: (0, 0)),
        ],
        out_specs=pl.BlockSpec((n_new, ATTN_DIM), rows),
        out_shape=jax.ShapeDtypeStruct((nb * n_new, ATTN_DIM), _BF16),
        compiler_params=pltpu.CompilerParams(
            dimension_semantics=("arbitrary",), vmem_limit_bytes=_vmem_limit(40 << 20)),
        name="attention_sample",
    )(q, k_new, v_new, cache_k, cache_v, *lam_vecs, subln_g)


def _finish_kernel(x_ref, yconv_ref, attn_ref, wo_ref, g1_ref, b1_ref, w1_ref, w2_ref, g2_ref, b2_ref,
                   y_ref):
    tile = x_ref.shape[0]
    group = min(tile, FINISH_GROUP)
    groups = [slice(r0, r0 + group) for r0 in range(0, tile, group)]
    pre = [ALPHA * x_ref[rows, :]
           + _dot(jnp.concatenate([yconv_ref[rows, :], attn_ref[rows, :]], axis=-1), wo_ref[...])
           for rows in groups]
    for rows, z in zip(groups, pre):
        x1 = _layer_norm(z, g1_ref[...], b1_ref[...])
        x1b = x1.astype(_BF16)
        ff = jnp.zeros_like(x1)
        for c in range(D_FF // FF_CHUNK):
            cols = slice(c * FF_CHUNK, (c + 1) * FF_CHUNK)
            hdn = jnp.square(jnp.maximum(_dot(x1b, w1_ref[:, cols]), 0.0))
            ff = ff + _dot(hdn.astype(_BF16), w2_ref[cols, :])
        y_ref[rows, :] = _layer_norm(ALPHA * x1 + ff, g2_ref[...], b2_ref[...])


def _finish(x, yconv, attn, w_out, g1, b1, w_ff1, w_ff2, g2, b2):
    r = x.shape[0]
    tile = min(ROW_TILE, r)
    assert r % tile == 0
    row = lambda i: (i, 0)
    const = lambda shape: pl.BlockSpec(shape, lambda i: (0, 0))
    return pl.pallas_call(
        _finish_kernel,
        grid=(r // tile,),
        in_specs=[
            pl.BlockSpec((tile, D_MODEL), row),
            pl.BlockSpec((tile, CONV_DIM), row),
            pl.BlockSpec((tile, ATTN_DIM), row),
            const((D_MODEL, D_MODEL)), const((1, D_MODEL)), const((1, D_MODEL)),
            const((D_MODEL, D_FF)), const((D_FF, D_MODEL)), const((1, D_MODEL)), const((1, D_MODEL)),
        ],
        out_specs=pl.BlockSpec((tile, D_MODEL), row),
        out_shape=jax.ShapeDtypeStruct((r, D_MODEL), _F32),
        compiler_params=pltpu.CompilerParams(
            dimension_semantics=("arbitrary",), vmem_limit_bytes=_vmem_limit(56 << 20)),
        name="finish_layer",
    )(x, yconv, attn, w_out, g1, b1, w_ff1, w_ff2, g2, b2)


def kernel(x_prompt, x_sample, cache_k, cache_v, state_conv, w_in, conv_w, lambda_q1, lambda_k1,
           lambda_q2, lambda_k2, subln_g, w_out, ln1_g, ln1_b, w_ff1, w_ff2, ln2_g, ln2_b):
    bp, sp, _ = x_prompt.shape
    bs, ss, _ = x_sample.shape
    depth = w_in.shape[0]
    assert bp == 1 and depth == 1
    past_len = cache_k.shape[2]
    l = 0
    lambda_init = 0.8 - 0.6 * float(np.exp(-0.3 * l))

    w_in_b = w_in[l].astype(_BF16)
    lam_vecs = tuple(v[l].reshape(1, QK_DIM) for v in (lambda_q1, lambda_k1, lambda_q2, lambda_k2))
    g_sub = subln_g[l].reshape(1, V_DIM)
    ln = tuple(v[l].reshape(1, D_MODEL) for v in (ln1_g, ln1_b, ln2_g, ln2_b))

    xp2 = x_prompt.reshape(sp, D_MODEL)
    (yconv_p, k_p, v_p, qa, qb, ka, kb, vt, conv_p, w_out_b, w_ff1_b, w_ff2_b) = _project_prompt(
        xp2, w_in_b, conv_w[l], (w_out[l], w_ff1[l], w_ff2[l]))

    def finish(x2d, yconv, attn):
        return _finish(x2d, yconv, attn, w_out_b, ln[0], ln[1], w_ff1_b, w_ff2_b, ln[2], ln[3])

    attn_p = _attention_prompt(qa, qb, ka, kb, vt, lam_vecs, g_sub, lambda_init)
    y_prompt = finish(xp2, yconv_p, attn_p).reshape(bp, sp, D_MODEL)

    xs2 = x_sample.reshape(bs * ss, D_MODEL)
    st = state_conv[l].astype(_F32)
    pad = lambda a: jnp.pad(a, ((0, 0), (0, ss - a.shape[1]), (0, 0))).reshape(bs * ss, CONV_DIM)
    hist2 = pad(st)
    hist1 = pad(st[:, 1:])
    yconv_s, q_s, k_s, v_s, u_s = _project_sample(xs2, w_in_b, conv_w[l], hist1, hist2, ss)
    ck = cache_k.reshape(depth * bs, past_len * N_HEADS, HEAD_W)
    cv = cache_v.reshape(depth * bs, past_len * N_HEADS, V_DIM)
    attn_s = _attention_sample(q_s, k_s, v_s, ck, cv, lam_vecs, g_sub, lambda_init, ss)
    y_sample = finish(xs2, yconv_s, attn_s).reshape(bs, ss, D_MODEL)

    k_prompt = k_p.reshape(depth, bp, sp, N_HEADS, HEAD_W)
    v_prompt = v_p.reshape(depth, bp, sp, N_HEADS, V_DIM)
    conv_prompt = conv_p.reshape(depth, bp, CONV_W - 1, CONV_DIM)
    k_sample = k_s.reshape(depth, bs, ss, N_HEADS, HEAD_W)
    v_sample = v_s.reshape(depth, bs, ss, N_HEADS, V_DIM)
    conv_sample = u_s.reshape(bs, ss, CONV_DIM)[:, ss - (CONV_W - 1):].reshape(
        depth, bs, CONV_W - 1, CONV_DIM)
    return (y_prompt, y_sample, k_prompt, v_prompt, conv_prompt, k_sample, v_sample, conv_sample)
```

```python
from functools import partial

import jax
import jax.numpy as jnp
import numpy as np
from jax import lax
from jax.experimental import pallas as pl
from jax.experimental.pallas import tpu as pltpu

D_MODEL = 1024
CHUNK = 64
CONV_DIM = 512
ATTN_DIM = 512
QK_DIM = 64
N_HEADS = 4
HEAD_W = 2 * QK_DIM
V_DIM = 128
F32_SUBLANES = 8
BF16_SUBLANES = 16
V_EXT = V_DIM + BF16_SUBLANES
LOG2E = 1.4426950408889634
CONV_W = 3
D_FF = 4096
LN_EPS = 1e-5
DEPTH = 1
ALPHA = (2 * DEPTH) ** 0.25
IN_WIDTH = 3 * CONV_DIM + 3 * ATTN_DIM

V7X_MXU_DIM = 256
V7X_VMEM_BYTES = 64 * 1024 * 1024

ROW_TILE = 1024
FINISH_GROUP = 256
Q_TILE = 2 * V7X_MXU_DIM
K_TILE = 2 * V7X_MXU_DIM
PROJ_TILE = 1024
TRIP_PAIRS = (8, 4, 2, 1)
FF_CHUNK = 1024
CACHE_RING = 3

_F32 = jnp.float32
_BF16 = jnp.bfloat16


def _vmem_limit(nbytes):
    assert nbytes <= V7X_VMEM_BYTES - (4 << 20)
    return int(nbytes)


def _dot(a, b):
    return jnp.dot(a, b, preferred_element_type=_F32)


def _dot_nt(a, b):
    return lax.dot_general(a, b, (((1,), (1,)), ((), ())), preferred_element_type=_F32)


def _dot_tn(a, b):
    return lax.dot_general(a, b, (((0,), (0,)), ((), ())), preferred_element_type=_F32)


def _div_pow2(x, n):
    assert n & (n - 1) == 0
    return lax.shift_right_logical(x, n.bit_length() - 1)


def _mod_pow2(x, n):
    assert n & (n - 1) == 0
    return lax.bitwise_and(x, n - 1)


def _layer_norm(x, g, b):
    mu = jnp.mean(x, axis=-1, keepdims=True)
    xc = x - mu
    var = jnp.mean(xc * xc, axis=-1, keepdims=True)
    return xc * lax.rsqrt(var + LN_EPS) * g + b


def _lambda_value(lq1, lk1, lq2, lk2, lambda_init):
    s1 = jnp.sum(lq1 * lk1, axis=-1, keepdims=True)
    s2 = jnp.sum(lq2 * lk2, axis=-1, keepdims=True)
    return jnp.exp(s1) - jnp.exp(s2) + lambda_init


def _project_columns(x_ref, w_ref):
    xb = x_ref[...].astype(_BF16)
    groups = {g: _dot(xb, w_ref[:, g * CONV_DIM:(g + 1) * CONV_DIM]) for g in (1, 2, 0, 5, 4, 3)}
    return [groups[g] for g in range(IN_WIDTH // CONV_DIM)]


def _conv_from_taps(u, u1, u2, cw_ref):
    return cw_ref[0:1, :] * u2 + cw_ref[1:2, :] * u1 + cw_ref[2:3, :] * u


def _proj_prompt_kernel(x_ref, w_ref, cw_ref, wo_ref, w1_ref, w2_ref,
                        yconv_ref, kf_ref, vf_ref, qa_ref, qb_ref, ka_ref, kb_ref, vt_ref, cstate_ref,
                        wo_b_ref, w1_b_ref, w2_b_ref, carry_ref):
    i = pl.program_id(0)

    wo_b_ref[...] = wo_ref[...].astype(_BF16)
    w1_b_ref[...] = w1_ref[...].astype(_BF16)
    w2_b_ref[...] = w2_ref[...].astype(_BF16)

    @pl.when(i == 0)
    def _():
        carry_ref[...] = jnp.zeros_like(carry_ref)

    gate, cc, hh, q, k, v = _project_columns(x_ref, w_ref)
    u = cc * hh
    rows = lax.broadcasted_iota(jnp.int32, u.shape, 0)
    prev2 = carry_ref[0:1, :]
    prev1 = carry_ref[1:2, :]
    u1 = jnp.where(rows == 0, prev1, pltpu.roll(u, 1, 0))
    u2 = jnp.where(rows == 0, prev2, jnp.where(rows == 1, prev1, pltpu.roll(u, 2, 0)))
    yconv_ref[...] = (gate * _conv_from_taps(u, u1, u2, cw_ref)).astype(_BF16)
    tail = u[PROJ_TILE - (CONV_W - 1):, :]
    carry_ref[0:CONV_W - 1, :] = tail
    cstate_ref[...] = tail

    for h in range(N_HEADS):
        sl = slice(h * HEAD_W, (h + 1) * HEAD_W)
        kf_ref[pl.ds(h, PROJ_TILE, stride=N_HEADS), :] = k[:, sl]
        vf_ref[pl.ds(h, PROJ_TILE, stride=N_HEADS), :] = v[:, sl]
    qs = q * (QK_DIM ** -0.5 * LOG2E)
    lane = lax.broadcasted_iota(jnp.int32, (PROJ_TILE, HEAD_W), 1)
    first = lane < QK_DIM
    ones_row = jnp.where(lax.broadcasted_iota(jnp.int32, (BF16_SUBLANES, K_TILE), 0) == 0,
                         1.0, 0.0).astype(_BF16)
    chunks_per_ktile = K_TILE // CHUNK
    assert QK_DIM // chunks_per_ktile >= 2
    row = lax.broadcasted_iota(jnp.int32, (PROJ_TILE, HEAD_W), 0)
    key_chunk = _div_pow2(_mod_pow2(row, K_TILE), CHUNK)
    lane_chunk = _mod_pow2(lane, chunks_per_ktile)
    chunk_onehot = jnp.where(lane_chunk == key_chunk, 1.0, 0.0)
    q_tile = _div_pow2(i * PROJ_TILE + row, Q_TILE)
    tile_offset = _mod_pow2(q_tile, K_TILE // Q_TILE) * (Q_TILE // CHUNK)
    query_chunk = _div_pow2(_mod_pow2(row, Q_TILE), CHUNK) + tile_offset
    mask_weight = jnp.where(lane_chunk > query_chunk, float(jnp.finfo(_BF16).min), 0.0)
    for h in range(N_HEADS):
        sl = slice(h * HEAD_W, (h + 1) * HEAD_W)
        qh = qs[:, sl]
        kh = k[:, sl]
        qa_ref[h, 0] = jnp.where(first, qh, 0.0).astype(_BF16)
        qb_ref[h, 0] = jnp.where(first, 0.0, qh).astype(_BF16)
        qa_ref[h, 1] = jnp.where(first, qh, mask_weight).astype(_BF16)
        qb_ref[h, 1] = jnp.where(first, mask_weight, qh).astype(_BF16)
        ka_ref[h] = jnp.where(first, kh, chunk_onehot).astype(_BF16)
        kb_ref[h] = jnp.where(first, chunk_onehot, kh).astype(_BF16)
        for s in range(PROJ_TILE // K_TILE):
            vt_ref[h, s, :V_DIM, :] = v[s * K_TILE:(s + 1) * K_TILE, sl].T.astype(_BF16)
            vt_ref[h, s, V_DIM:, :] = ones_row


def _proj_sample_kernel(x_ref, w_ref, cw_ref, h1_ref, h2_ref,
                        yconv_ref, q_ref, k_ref, v_ref, u_ref, *, period):
    gate, cc, hh, q, k, v = _project_columns(x_ref, w_ref)
    u = cc * hh
    t = _mod_pow2(lax.broadcasted_iota(jnp.int32, u.shape, 0), period)
    u1 = jnp.where(t < 1, h1_ref[...], pltpu.roll(u, 1, 0))
    u2 = jnp.where(t < 2, h2_ref[...], pltpu.roll(u, 2, 0))
    yconv_ref[...] = (gate * _conv_from_taps(u, u1, u2, cw_ref)).astype(_BF16)
    q_ref[...] = q * (QK_DIM ** -0.5)
    k_ref[...] = k
    v_ref[...] = v
    u_ref[...] = u


def _project_prompt(x, w_in, conv_w, later_weights):
    s = x.shape[0]
    assert s % PROJ_TILE == 0 and PROJ_TILE % K_TILE == 0
    n = s // PROJ_TILE
    row = lambda i: (i, 0)
    head_row = lambda i: (0, i, 0)
    for w in later_weights:
        assert w.shape[0] % (n * BF16_SUBLANES) == 0
    weight_specs = [pl.BlockSpec((w.shape[0] // n, w.shape[1]), row) for w in later_weights]
    out_shape = (
        jax.ShapeDtypeStruct((s, CONV_DIM), _BF16),
        jax.ShapeDtypeStruct((s * N_HEADS, HEAD_W), _F32),
        jax.ShapeDtypeStruct((s * N_HEADS, V_DIM), _F32),
        jax.ShapeDtypeStruct((N_HEADS, 2, s, HEAD_W), _BF16),
        jax.ShapeDtypeStruct((N_HEADS, 2, s, HEAD_W), _BF16),
        jax.ShapeDtypeStruct((N_HEADS, s, HEAD_W), _BF16),
        jax.ShapeDtypeStruct((N_HEADS, s, HEAD_W), _BF16),
        jax.ShapeDtypeStruct((N_HEADS, s // K_TILE, V_EXT, K_TILE), _BF16),
        jax.ShapeDtypeStruct((CONV_W - 1, CONV_DIM), _F32),
    ) + tuple(jax.ShapeDtypeStruct(w.shape, _BF16) for w in later_weights)
    out_specs = (
        pl.BlockSpec((PROJ_TILE, CONV_DIM), row),
        pl.BlockSpec((PROJ_TILE * N_HEADS, HEAD_W), row),
        pl.BlockSpec((PROJ_TILE * N_HEADS, V_DIM), row),
        pl.BlockSpec((N_HEADS, 2, PROJ_TILE, HEAD_W), lambda i: (0, 0, i, 0)),
        pl.BlockSpec((N_HEADS, 2, PROJ_TILE, HEAD_W), lambda i: (0, 0, i, 0)),
        pl.BlockSpec((N_HEADS, PROJ_TILE, HEAD_W), head_row),
        pl.BlockSpec((N_HEADS, PROJ_TILE, HEAD_W), head_row),
        pl.BlockSpec((N_HEADS, PROJ_TILE // K_TILE, V_EXT, K_TILE), lambda i: (0, i, 0, 0)),
        pl.BlockSpec((CONV_W - 1, CONV_DIM), lambda i: (0, 0)),
    ) + tuple(weight_specs)
    return pl.pallas_call(
        _proj_prompt_kernel,
        grid=(n,),
        in_specs=[
            pl.BlockSpec((PROJ_TILE, D_MODEL), row),
            pl.BlockSpec((D_MODEL, IN_WIDTH), lambda i: (0, 0)),
            pl.BlockSpec((CONV_W, CONV_DIM), lambda i: (0, 0)),
        ] + weight_specs,
        out_specs=out_specs,
        out_shape=out_shape,
        scratch_shapes=[pltpu.VMEM((8, CONV_DIM), _F32)],
        compiler_params=pltpu.CompilerParams(
            dimension_semantics=("arbitrary",), vmem_limit_bytes=_vmem_limit(56 << 20)),
        name="project_prompt",
    )(x, w_in, conv_w, *later_weights)


def _project_sample(x, w_in, conv_w, hist1, hist2, period):
    r = x.shape[0]
    full = lambda shape: pl.BlockSpec(shape, lambda i: (0,) * len(shape))
    wide = (r, CONV_DIM)
    return pl.pallas_call(
        partial(_proj_sample_kernel, period=period),
        grid=(1,),
        in_specs=[full((r, D_MODEL)), full((D_MODEL, IN_WIDTH)), full((CONV_W, CONV_DIM)),
                  full(wide), full(wide)],
        out_specs=(full(wide),) * 5,
        out_shape=(jax.ShapeDtypeStruct(wide, _BF16),) + (jax.ShapeDtypeStruct(wide, _F32),) * 4,
        compiler_params=pltpu.CompilerParams(
            dimension_semantics=("arbitrary",), vmem_limit_bytes=_vmem_limit(48 << 20)),
        name="project_sample",
    )(x, w_in, conv_w, hist1, hist2)


def _attn_prompt_kernel(qa_ref, qb_ref, ka_ref, kb_ref, vt_ref, lq1_ref, lk1_ref, lq2_ref, lk2_ref,
                        g_ref, o_ref, s_even, s_odd, cmax_even, cmax_odd, m_ref, acc_ref,
                        *, lambda_init):
    i = pl.program_id(1)
    q_refs = (qa_ref, qb_ref)
    k_refs = (ka_ref, kb_ref)
    bufs = ((s_even, cmax_even), (s_odd, cmax_odd))
    n_full = _div_pow2(i, K_TILE // Q_TILE)

    def col_max(st):
        part = jnp.max(st.reshape(K_TILE // F32_SUBLANES, F32_SUBLANES, Q_TILE), axis=0)
        return jnp.max(part, axis=0, keepdims=True)

    def scores(t, buf):
        s_ref, cmax_ref = buf
        rows = pl.ds(pl.multiple_of(t * K_TILE, K_TILE), K_TILE)
        variant = (t == n_full).astype(jnp.int32)
        for c in range(2):
            st = _dot_nt(k_refs[c][0, rows, :], q_refs[c][0, variant])
            s_ref[c] = st
            cmax_ref[c] = col_max(st)

    def consume(t, buf):
        s_ref, cmax_ref = buf
        vt = vt_ref[0, t]
        for c in range(2):
            m = m_ref[c]
            m_new = jnp.maximum(m, cmax_ref[c])
            alpha = jnp.exp2(m - m_new)
            p = jnp.exp2(s_ref[c] - m_new).astype(_BF16)
            acc_ref[c] = alpha * acc_ref[c] + _dot(vt, p)
            m_ref[c] = m_new

    m_ref[...] = jnp.full(m_ref.shape, -jnp.inf, _F32)
    acc_ref[...] = jnp.zeros(acc_ref.shape, _F32)
    scores(0, bufs[0])

    def pairs(t, count):
        for k in range(count):
            scores(t + 2 * k + 1, bufs[1])
            consume(t + 2 * k, bufs[0])
            scores(t + 2 * k + 2, bufs[0])
            consume(t + 2 * k + 1, bufs[1])

    n_pairs = _div_pow2(n_full, 2)
    pairs_done = 0
    for size in TRIP_PAIRS:
        n_trips = _div_pow2(n_pairs - pairs_done, size)

        def trip(p, carry, size=size, first=pairs_done):
            pairs(2 * (first + p * size), size)
            return carry

        lax.fori_loop(0, n_trips, trip, 0)
        pairs_done = pairs_done + n_trips * size

    t0 = 2 * n_pairs
    two_left = n_full > t0

    @pl.when(two_left)
    def _():
        scores(n_full, bufs[1])
        consume(t0, bufs[0])
        consume(n_full, bufs[1])

    @pl.when(jnp.logical_not(two_left))
    def _():
        consume(t0, bufs[0])

    lam =_lambda_value(lq1_ref[...], lk1_ref[...], lq2_ref[...], lk2_ref[...], lambda_init)
    acc0 = acc_ref[0]
    acc1 = acc_ref[1]
    l0 = acc0[V_DIM:V_DIM + 1, :]
    l1 = acc1[V_DIM:V_DIM + 1, :]
    o = acc0[:V_DIM] / l0 - lam * (acc1[:V_DIM] / l1)
    ms = jnp.mean(o * o, axis=0, keepdims=True)
    on = o * lax.rsqrt(ms + LN_EPS)
    on = on.T * g_ref[...] * (1.0 - lambda_init)
    o_ref[...] = on.astype(_BF16)


def _attention_prompt(qa, qb, ka, kb, vt, lam_vecs, subln_g, lambda_init):
    s = ka.shape[1]
    assert K_TILE % Q_TILE == 0 and s % K_TILE == 0 and Q_TILE % CHUNK == 0
    nq = s // Q_TILE
    vec = pl.BlockSpec((1, QK_DIM), lambda h, i: (0, 0))
    q_tile = pl.BlockSpec((1, 2, Q_TILE, HEAD_W), lambda h, i: (h, 0, i, 0))
    return pl.pallas_call(
        partial(_attn_prompt_kernel, lambda_init=lambda_init),
        grid=(N_HEADS, nq),
        in_specs=[
            q_tile, q_tile,
            pl.BlockSpec((1, s, HEAD_W), lambda h, i: (h, 0, 0)),
            pl.BlockSpec((1, s, HEAD_W), lambda h, i: (h, 0, 0)),
            pl.BlockSpec((1, s // K_TILE, V_EXT, K_TILE), lambda h, i: (h, 0, 0, 0)),
            vec, vec, vec, vec,
            pl.BlockSpec((1, V_DIM), lambda h, i: (0, 0)),
        ],
        out_specs=pl.BlockSpec((Q_TILE, V_DIM), lambda h, i: (i, h)),
        out_shape=jax.ShapeDtypeStruct((s, ATTN_DIM), _BF16),
        scratch_shapes=[
            pltpu.VMEM((2, K_TILE, Q_TILE), _F32), pltpu.VMEM((2, K_TILE, Q_TILE), _F32),
            pltpu.VMEM((2, 1, Q_TILE), _F32), pltpu.VMEM((2, 1, Q_TILE), _F32),
            pltpu.VMEM((2, 1, Q_TILE), _F32), pltpu.VMEM((2, V_EXT, Q_TILE), _F32),
        ],
        compiler_params=pltpu.CompilerParams(
            dimension_semantics=("arbitrary", "arbitrary"), vmem_limit_bytes=_vmem_limit(48 << 20)),
        name="attention_prompt",
    )(qa, qb, ka, kb, vt, *lam_vecs, subln_g)


def _attn_sample_kernel(q_ref, kn_ref, vn_ref, ck_hbm, cv_hbm, lq1_ref, lk1_ref, lq2_ref, lk2_ref,
                        g_ref, o_ref, ck_buf, cv_buf, sem, *, lambda_init, past_len, n_new):
    b = pl.program_id(0)
    nb = pl.num_programs(0)

    def stream_copies(stream, slot):
        return (pltpu.make_async_copy(ck_hbm.at[stream], ck_buf.at[slot], sem.at[0, slot]),
                pltpu.make_async_copy(cv_hbm.at[stream], cv_buf.at[slot], sem.at[1, slot]))

    @pl.when(b == 0)
    def _():
        for s in range(CACHE_RING - 1):
            for copy in stream_copies(s, s):
                copy.start()

    ahead = b + (CACHE_RING - 1)

    @pl.when(ahead < nb)
    def _():
        for copy in stream_copies(ahead, lax.rem(ahead, CACHE_RING)):
            copy.start()

    slot = lax.rem(b, CACHE_RING)
    for copy in stream_copies(b, slot):
        copy.wait()
    ck_ref = ck_buf.at[slot]
    cv_ref = cv_buf.at[slot]

    group = 2 * N_HEADS
    width = group * n_new
    assert width == HEAD_W
    qb = q_ref[...].astype(_BF16)
    sel_r = lax.broadcasted_iota(jnp.int32, (n_new, width), 0)
    sel_c = lax.broadcasted_iota(jnp.int32, (n_new, width), 1)
    spread = jnp.where(_mod_pow2(sel_c, n_new) == sel_r, 1.0, 0.0).astype(_BF16)
    q_all = _dot_tn(qb, spread)
    blk_r = _div_pow2(lax.broadcasted_iota(jnp.int32, (ATTN_DIM, width), 0), QK_DIM)
    blk_c = _div_pow2(lax.broadcasted_iota(jnp.int32, (ATTN_DIM, width), 1), n_new)
    q_bd = jnp.where(blk_r == blk_c, q_all, 0.0).astype(_BF16)

    half = past_len // 2

    def heads_on_lanes(ref, part):
        return jnp.concatenate(
            [ref[pl.ds(part * half * N_HEADS + h, half, stride=N_HEADS), :].astype(_BF16)
             for h in range(N_HEADS)], axis=-1)

    s_old = jnp.concatenate([_dot(heads_on_lanes(ck_ref, part), q_bd) for part in range(2)],
                            axis=0)
    s_new = _dot(kn_ref[...].astype(_BF16), q_bd)

    def chunk_mask(shape, k_off):
        k_pos = k_off + lax.broadcasted_iota(jnp.int32, shape, 0)
        q_pos = past_len + _mod_pow2(lax.broadcasted_iota(jnp.int32, shape, 1), n_new)
        return _div_pow2(k_pos, CHUNK) <= _div_pow2(q_pos, CHUNK)

    s_old = jnp.where(chunk_mask(s_old.shape, 0), s_old, -jnp.inf)
    s_new = jnp.where(chunk_mask(s_new.shape, past_len), s_new, -jnp.inf)
    m = jnp.maximum(jnp.max(s_old, axis=0, keepdims=True), jnp.max(s_new, axis=0, keepdims=True))
    p_old = jnp.exp(s_old - m)
    p_new = jnp.exp(s_new - m)
    inv_l = 1.0 / (jnp.sum(p_old, axis=0, keepdims=True) + jnp.sum(p_new, axis=0, keepdims=True))
    a_old = (p_old * inv_l).astype(_BF16)
    a_new = (p_new * inv_l).astype(_BF16)
    pv = (_dot_tn(a_old[:half], heads_on_lanes(cv_ref, 0)) + _dot_tn(a_old[half:], heads_on_lanes(cv_ref, 1))
          + _dot_tn(a_new, vn_ref[...].astype(_BF16)))

    lam = _lambda_value(lq1_ref[...], lk1_ref[...], lq2_ref[...], lk2_ref[...], lambda_init)
    g = g_ref[...]
    for h in range(N_HEADS):
        cols = slice(h * V_DIM, (h + 1) * V_DIM)
        r0 = h * 2 * n_new
        o = pv[r0:r0 + n_new, cols] - lam * pv[r0 + n_new:r0 + 2 * n_new, cols]
        ms = jnp.mean(o * o, axis=-1, keepdims=True)
        o_ref[:, cols] = (o * lax.rsqrt(ms + LN_EPS) * g * (1.0 - lambda_init)).astype(_BF16)


def _attention_sample(q, k_new, v_new, cache_k, cache_v, lam_vecs, subln_g, lambda_init, n_new):
    nb, rows_per_stream = cache_k.shape[:2]
    past_len = rows_per_stream // N_HEADS
    assert nb >= CACHE_RING
    cache_block = pl.BlockSpec(memory_space=pl.ANY)
    rows = lambda b: (b, 0)
    vec = pl.BlockSpec((1, QK_DIM), lambda b: (0, 0))
    return pl.pallas_call(
        partial(_attn_sample_kernel, lambda_init=lambda_init, past_len=past_len, n_new=n_new),
        grid=(nb,),
        in_specs=[
            pl.BlockSpec((n_new, ATTN_DIM), rows),
            pl.BlockSpec((n_new, ATTN_DIM), rows),
            pl.BlockSpec((n_new, ATTN_DIM), rows),
            cache_block, cache_block,
            vec, vec, vec, vec,
            pl.BlockSpec((1, V_DIM), lambda b: (0, 0)),
        ],
        out_specs=pl.BlockSpec((n_new, ATTN_DIM), rows),
        out_shape=jax.ShapeDtypeStruct((nb * n_new, ATTN_DIM), _BF16),
        scratch_shapes=[
            pltpu.VMEM((CACHE_RING, rows_per_stream, HEAD_W), cache_k.dtype),
            pltpu.VMEM((CACHE_RING, rows_per_stream, V_DIM), cache_v.dtype),
            pltpu.SemaphoreType.DMA((2, CACHE_RING)),
        ],
        compiler_params=pltpu.CompilerParams(
            dimension_semantics=("arbitrary",), vmem_limit_bytes=_vmem_limit(40 << 20)),
        name="attention_sample",
    )(q, k_new, v_new, cache_k, cache_v, *lam_vecs, subln_g)


def _finish_kernel(x_ref, yconv_ref, attn_ref, wo_ref, g1_ref, b1_ref, w1_ref, w2_ref, g2_ref, b2_ref,
                   y_ref):
    tile = x_ref.shape[0]
    group = min(tile, FINISH_GROUP)
    groups = [slice(r0, r0 + group) for r0 in range(0, tile, group)]
    pre = [ALPHA * x_ref[rows, :]
           + _dot(jnp.concatenate([yconv_ref[rows, :], attn_ref[rows, :]], axis=-1), wo_ref[...])
           for rows in groups]
    for rows, z in zip(groups, pre):
        x1 = _layer_norm(z, g1_ref[...], b1_ref[...])
        x1b = x1.astype(_BF16)
        ff = jnp.zeros_like(x1)
        for c in range(D_FF // FF_CHUNK):
            cols = slice(c * FF_CHUNK, (c + 1) * FF_CHUNK)
            hdn = jnp.square(jnp.maximum(_dot(x1b, w1_ref[:, cols]), 0.0))
            ff = ff + _dot(hdn.astype(_BF16), w2_ref[cols, :])
        y_ref[rows, :] = _layer_norm(ALPHA * x1 + ff, g2_ref[...], b2_ref[...])


def _finish(x, yconv, attn, w_out, g1, b1, w_ff1, w_ff2, g2, b2):
    r = x.shape[0]
    tile = min(ROW_TILE, r)
    assert r % tile == 0
    row = lambda i: (i, 0)
    const = lambda shape: pl.BlockSpec(shape, lambda i: (0, 0))
    return pl.pallas_call(
        _finish_kernel,
        grid=(r // tile,),
        in_specs=[
            pl.BlockSpec((tile, D_MODEL), row),
            pl.BlockSpec((tile, CONV_DIM), row),
            pl.BlockSpec((tile, ATTN_DIM), row),
            const((D_MODEL, D_MODEL)), const((1, D_MODEL)), const((1, D_MODEL)),
            const((D_MODEL, D_FF)), const((D_FF, D_MODEL)), const((1, D_MODEL)), const((1, D_MODEL)),
        ],
        out_specs=pl.BlockSpec((tile, D_MODEL), row),
        out_shape=jax.ShapeDtypeStruct((r, D_MODEL), _F32),
        compiler_params=pltpu.CompilerParams(
            dimension_semantics=("arbitrary",), vmem_limit_bytes=_vmem_limit(56 << 20)),
        name="finish_layer",
    )(x, yconv, attn, w_out, g1, b1, w_ff1, w_ff2, g2, b2)


def kernel(x_prompt, x_sample, cache_k, cache_v, state_conv, w_in, conv_w, lambda_q1, lambda_k1,
           lambda_q2, lambda_k2, subln_g, w_out, ln1_g, ln1_b, w_ff1, w_ff2, ln2_g, ln2_b):
    bp, sp, _ = x_prompt.shape
    bs, ss, _ = x_sample.shape
    depth = w_in.shape[0]
    assert bp == 1 and depth == 1
    past_len = cache_k.shape[2]
    l = 0
    lambda_init = 0.8 - 0.6 * float(np.exp(-0.3 * l))

    w_in_b = w_in[l].astype(_BF16)
    lam_vecs = tuple(v[l].reshape(1, QK_DIM) for v in (lambda_q1, lambda_k1, lambda_q2, lambda_k2))
    g_sub = subln_g[l].reshape(1, V_DIM)
    ln = tuple(v[l].reshape(1, D_MODEL) for v in (ln1_g, ln1_b, ln2_g, ln2_b))

    xp2 = x_prompt.reshape(sp, D_MODEL)
    (yconv_p, k_p, v_p, qa, qb, ka, kb, vt, conv_p, w_out_b, w_ff1_b, w_ff2_b) = _project_prompt(
        xp2, w_in_b, conv_w[l], (w_out[l], w_ff1[l], w_ff2[l]))

    def finish(x2d, yconv, attn):
        return _finish(x2d, yconv, attn, w_out_b, ln[0], ln[1], w_ff1_b, w_ff2_b, ln[2], ln[3])

    attn_p = _attention_prompt(qa, qb, ka, kb, vt, lam_vecs, g_sub, lambda_init)
    y_prompt = finish(xp2, yconv_p, attn_p).reshape(bp, sp, D_MODEL)

    xs2 = x_sample.reshape(bs * ss, D_MODEL)
    st = state_conv[l].astype(_F32)
    pad = lambda a: jnp.pad(a, ((0, 0), (0, ss - a.shape[1]), (0, 0))).reshape(bs * ss, CONV_DIM)
    hist2 = pad(st)
    hist1 = pad(st[:, 1:])
    yconv_s, q_s, k_s, v_s, u_s = _project_sample(xs2, w_in_b, conv_w[l], hist1, hist2, ss)
    ck = cache_k.reshape(depth * bs, past_len * N_HEADS, HEAD_W)
    cv = cache_v.reshape(depth * bs, past_len * N_HEADS, V_DIM)
    attn_s = _attention_sample(q_s, k_s, v_s, ck, cv, lam_vecs, g_sub, lambda_init, ss)
    y_sample = finish(xs2, yconv_s, attn_s).reshape(bs, ss, D_MODEL)

    k_prompt = k_p.reshape(depth, bp, sp, N_HEADS, HEAD_W)
    v_prompt = v_p.reshape(depth, bp, sp, N_HEADS, V_DIM)
    conv_prompt = conv_p.reshape(depth, bp, CONV_W - 1, CONV_DIM)
    k_sample = k_s.reshape(depth, bs, ss, N_HEADS, HEAD_W)
    v_sample = v_s.reshape(depth, bs, ss, N_HEADS, V_DIM)
    conv_sample = u_s.reshape(bs, ss, CONV_DIM)[:, ss - (CONV_W - 1):].reshape(
        depth, bs, CONV_W - 1, CONV_DIM)
    return (y_prompt, y_sample, k_prompt, v_prompt, conv_prompt, k_sample, v_sample, conv_sample)
```

```python
from functools import partial

import jax
import jax.numpy as jnp
import numpy as np
from jax import lax
from jax.experimental import pallas as pl
from jax.experimental.pallas import tpu as pltpu

D_MODEL = 1024
CHUNK = 64
CONV_DIM = 512
ATTN_DIM = 512
QK_DIM = 64
N_HEADS = 4
HEAD_W = 2 * QK_DIM
V_DIM = 128
F32_SUBLANES = 8
BF16_SUBLANES = 16
V_EXT = V_DIM + BF16_SUBLANES
LOG2E = 1.4426950408889634
CONV_W = 3
D_FF = 4096
LN_EPS = 1e-5
DEPTH = 1
ALPHA = (2 * DEPTH) ** 0.25
IN_WIDTH = 3 * CONV_DIM + 3 * ATTN_DIM

V7X_MXU_DIM = 256
V7X_VMEM_BYTES = 64 * 1024 * 1024

ROW_TILE = 1024
FINISH_GROUP = 256
Q_TILE = 2 * V7X_MXU_DIM
K_TILE = 2 * V7X_MXU_DIM
PROJ_TILE = 1024
TRIP_PAIRS = (8, 4, 2, 1)
FF_CHUNK = 1024
CACHE_RING = 3

_F32 = jnp.float32
_BF16 = jnp.bfloat16


def _vmem_limit(nbytes):
    assert nbytes <= V7X_VMEM_BYTES - (4 << 20)
    return int(nbytes)


def _dot(a, b):
    return jnp.dot(a, b, preferred_element_type=_F32)


def _dot_nt(a, b):
    return lax.dot_general(a, b, (((1,), (1,)), ((), ())), preferred_element_type=_F32)


def _dot_tn(a, b):
    return lax.dot_general(a, b, (((0,), (0,)), ((), ())), preferred_element_type=_F32)


def _div_pow2(x, n):
    assert n & (n - 1) == 0
    return lax.shift_right_logical(x, n.bit_length() - 1)


def _mod_pow2(x, n):
    assert n & (n - 1) == 0
    return lax.bitwise_and(x, n - 1)


def _layer_norm(x, g, b):
    mu = jnp.mean(x, axis=-1, keepdims=True)
    xc = x - mu
    var = jnp.mean(xc * xc, axis=-1, keepdims=True)
    return xc * lax.rsqrt(var + LN_EPS) * g + b


def _lambda_value(lq1, lk1, lq2, lk2, lambda_init):
    s1 = jnp.sum(lq1 * lk1, axis=-1, keepdims=True)
    s2 = jnp.sum(lq2 * lk2, axis=-1, keepdims=True)
    return jnp.exp(s1) - jnp.exp(s2) + lambda_init


def _project_columns(x_ref, w_ref):
    xb = x_ref[...].astype(_BF16)
    groups = {g: _dot(xb, w_ref[:, g * CONV_DIM:(g + 1) * CONV_DIM]) for g in (1, 2, 0, 5, 4, 3)}
    return [groups[g] for g in range(IN_WIDTH // CONV_DIM)]


def _conv_from_taps(u, u1, u2, cw_ref):
    return cw_ref[0:1, :] * u2 + cw_ref[1:2, :] * u1 + cw_ref[2:3, :] * u


def _proj_prompt_kernel(x_ref, w_ref, cw_ref, wo_ref, w1_ref, w2_ref,
                        yconv_ref, kf_ref, vf_ref, qa_ref, qb_ref, ka_ref, kb_ref, vt_ref, cstate_ref,
                        wo_b_ref, w1_b_ref, w2_b_ref, carry_ref):
    i = pl.program_id(0)

    wo_b_ref[...] = wo_ref[...].astype(_BF16)
    w1_b_ref[...] = w1_ref[...].astype(_BF16)
    w2_b_ref[...] = w2_ref[...].astype(_BF16)

    @pl.when(i == 0)
    def _():
        carry_ref[...] = jnp.zeros_like(carry_ref)

    gate, cc, hh, q, k, v = _project_columns(x_ref, w_ref)
    u = cc * hh
    rows = lax.broadcasted_iota(jnp.int32, u.shape, 0)
    prev2 = carry_ref[0:1, :]
    prev1 = carry_ref[1:2, :]
    u1 = jnp.where(rows == 0, prev1, pltpu.roll(u, 1, 0))
    u2 = jnp.where(rows == 0, prev2, jnp.where(rows == 1, prev1, pltpu.roll(u, 2, 0)))
    yconv_ref[...] = (gate * _conv_from_taps(u, u1, u2, cw_ref)).astype(_BF16)
    tail = u[PROJ_TILE - (CONV_W - 1):, :]
    carry_ref[0:CONV_W - 1, :] = tail
    cstate_ref[...] = tail

    for h in range(N_HEADS):
        sl = slice(h * HEAD_W, (h + 1) * HEAD_W)
        kf_ref[pl.ds(h, PROJ_TILE, stride=N_HEADS), :] = k[:, sl]
        vf_ref[pl.ds(h, PROJ_TILE, stride=N_HEADS), :] = v[:, sl]
    qs = q * (QK_DIM ** -0.5 * LOG2E)
    lane = lax.broadcasted_iota(jnp.int32, (PROJ_TILE, HEAD_W), 1)
    first = lane < QK_DIM
    ones_row = jnp.where(lax.broadcasted_iota(jnp.int32, (BF16_SUBLANES, K_TILE), 0) == 0,
                         1.0, 0.0).astype(_BF16)
    chunks_per_ktile = K_TILE // CHUNK
    assert QK_DIM // chunks_per_ktile >= 2
    row = lax.broadcasted_iota(jnp.int32, (PROJ_TILE, HEAD_W), 0)
    key_chunk = _div_pow2(_mod_pow2(row, K_TILE), CHUNK)
    lane_chunk = _mod_pow2(lane, chunks_per_ktile)
    chunk_onehot = jnp.where(lane_chunk == key_chunk, 1.0, 0.0)
    q_tile = _div_pow2(i * PROJ_TILE + row, Q_TILE)
    tile_offset = _mod_pow2(q_tile, K_TILE // Q_TILE) * (Q_TILE // CHUNK)
    query_chunk = _div_pow2(_mod_pow2(row, Q_TILE), CHUNK) + tile_offset
    mask_weight = jnp.where(lane_chunk > query_chunk, float(jnp.finfo(_BF16).min), 0.0)
    for h in range(N_HEADS):
        sl = slice(h * HEAD_W, (h + 1) * HEAD_W)
        qh = qs[:, sl]
        kh = k[:, sl]
        qa_ref[h, 0] = jnp.where(first, qh, 0.0).astype(_BF16)
        qb_ref[h, 0] = jnp.where(first, 0.0, qh).astype(_BF16)
        qa_ref[h, 1] = jnp.where(first, qh, mask_weight).astype(_BF16)
        qb_ref[h, 1] = jnp.where(first, mask_weight, qh).astype(_BF16)
        ka_ref[h] = jnp.where(first, kh, chunk_onehot).astype(_BF16)
        kb_ref[h] = jnp.where(first, chunk_onehot, kh).astype(_BF16)
        for s in range(PROJ_TILE // K_TILE):
            vt_ref[h, s, :V_DIM, :] = v[s * K_TILE:(s + 1) * K_TILE, sl].T.astype(_BF16)
            vt_ref[h, s, V_DIM:, :] = ones_row


def _proj_sample_kernel(x_ref, w_ref, cw_ref, h1_ref, h2_ref,
                        yconv_ref, q_ref, k_ref, v_ref, u_ref, wb_ref, *, period):
    wb_ref[...] = w_ref[...].astype(_BF16)
    gate, cc, hh, q, k, v = _project_columns(x_ref, wb_ref)
    u = cc * hh
    t = _mod_pow2(lax.broadcasted_iota(jnp.int32, u.shape, 0), period)
    u1 = jnp.where(t < 1, h1_ref[...], pltpu.roll(u, 1, 0))
    u2 = jnp.where(t < 2, h2_ref[...], pltpu.roll(u, 2, 0))
    yconv_ref[...] = (gate * _conv_from_taps(u, u1, u2, cw_ref)).astype(_BF16)
    q_ref[...] = q * (QK_DIM ** -0.5)
    k_ref[...] = k
    v_ref[...] = v
    u_ref[...] = u


def _project_prompt(x, w_in, conv_w, later_weights):
    s = x.shape[0]
    assert s % PROJ_TILE == 0 and PROJ_TILE % K_TILE == 0
    n = s // PROJ_TILE
    row = lambda i: (i, 0)
    head_row = lambda i: (0, i, 0)
    for w in later_weights:
        assert w.shape[0] % (n * BF16_SUBLANES) == 0
    weight_specs = [pl.BlockSpec((w.shape[0] // n, w.shape[1]), row) for w in later_weights]
    out_shape = (
        jax.ShapeDtypeStruct((s, CONV_DIM), _BF16),
        jax.ShapeDtypeStruct((s * N_HEADS, HEAD_W), _F32),
        jax.ShapeDtypeStruct((s * N_HEADS, V_DIM), _F32),
        jax.ShapeDtypeStruct((N_HEADS, 2, s, HEAD_W), _BF16),
        jax.ShapeDtypeStruct((N_HEADS, 2, s, HEAD_W), _BF16),
        jax.ShapeDtypeStruct((N_HEADS, s, HEAD_W), _BF16),
        jax.ShapeDtypeStruct((N_HEADS, s, HEAD_W), _BF16),
        jax.ShapeDtypeStruct((N_HEADS, s // K_TILE, V_EXT, K_TILE), _BF16),
        jax.ShapeDtypeStruct((CONV_W - 1, CONV_DIM), _F32),
    ) + tuple(jax.ShapeDtypeStruct(w.shape, _BF16) for w in later_weights)
    out_specs = (
        pl.BlockSpec((PROJ_TILE, CONV_DIM), row),
        pl.BlockSpec((PROJ_TILE * N_HEADS, HEAD_W), row),
        pl.BlockSpec((PROJ_TILE * N_HEADS, V_DIM), row),
        pl.BlockSpec((N_HEADS, 2, PROJ_TILE, HEAD_W), lambda i: (0, 0, i, 0)),
        pl.BlockSpec((N_HEADS, 2, PROJ_TILE, HEAD_W), lambda i: (0, 0, i, 0)),
        pl.BlockSpec((N_HEADS, PROJ_TILE, HEAD_W), head_row),
        pl.BlockSpec((N_HEADS, PROJ_TILE, HEAD_W), head_row),
        pl.BlockSpec((N_HEADS, PROJ_TILE // K_TILE, V_EXT, K_TILE), lambda i: (0, i, 0, 0)),
        pl.BlockSpec((CONV_W - 1, CONV_DIM), lambda i: (0, 0)),
    ) + tuple(weight_specs)
    return pl.pallas_call(
        _proj_prompt_kernel,
        grid=(n,),
        in_specs=[
            pl.BlockSpec((PROJ_TILE, D_MODEL), row),
            pl.BlockSpec((D_MODEL, IN_WIDTH), lambda i: (0, 0)),
            pl.BlockSpec((CONV_W, CONV_DIM), lambda i: (0, 0)),
        ] + weight_specs,
        out_specs=out_specs,
        out_shape=out_shape,
        scratch_shapes=[pltpu.VMEM((8, CONV_DIM), _F32)],
        compiler_params=pltpu.CompilerParams(
            dimension_semantics=("arbitrary",), vmem_limit_bytes=_vmem_limit(56 << 20)),
        name="project_prompt",
    )(x, w_in, conv_w, *later_weights)


def _project_sample(x, w_in, conv_w, hist1, hist2, period):
    r = x.shape[0]
    full = lambda shape: pl.BlockSpec(shape, lambda i: (0,) * len(shape))
    wide = (r, CONV_DIM)
    return pl.pallas_call(
        partial(_proj_sample_kernel, period=period),
        grid=(1,),
        in_specs=[full((r, D_MODEL)), full((D_MODEL, IN_WIDTH)), full((CONV_W, CONV_DIM)),
                  full(wide), full(wide)],
        out_specs=(full(wide),) * 5 + (full((D_MODEL, IN_WIDTH)),),
        out_shape=(jax.ShapeDtypeStruct(wide, _BF16),) + (jax.ShapeDtypeStruct(wide, _F32),) * 4
        + (jax.ShapeDtypeStruct((D_MODEL, IN_WIDTH), _BF16),),
        compiler_params=pltpu.CompilerParams(
            dimension_semantics=("arbitrary",), vmem_limit_bytes=_vmem_limit(48 << 20)),
        name="project_sample",
    )(x, w_in, conv_w, hist1, hist2)


def _attn_prompt_kernel(qa_ref, qb_ref, ka_ref, kb_ref, vt_ref, lq1_ref, lk1_ref, lq2_ref, lk2_ref,
                        g_ref, o_ref, s_even, s_odd, cmax_even, cmax_odd, m_ref, acc_ref,
                        *, lambda_init):
    i = pl.program_id(1)
    q_refs = (qa_ref, qb_ref)
    k_refs = (ka_ref, kb_ref)
    bufs = ((s_even, cmax_even), (s_odd, cmax_odd))
    n_full = _div_pow2(i, K_TILE // Q_TILE)

    def col_max(st):
        part = jnp.max(st.reshape(K_TILE // F32_SUBLANES, F32_SUBLANES, Q_TILE), axis=0)
        return jnp.max(part, axis=0, keepdims=True)

    def scores(t, buf):
        s_ref, cmax_ref = buf
        rows = pl.ds(pl.multiple_of(t * K_TILE, K_TILE), K_TILE)
        variant = (t == n_full).astype(jnp.int32)
        for c in range(2):
            st = _dot_nt(k_refs[c][0, rows, :], q_refs[c][0, variant])
            s_ref[c] = st
            cmax_ref[c] = col_max(st)

    def consume(t, buf):
        s_ref, cmax_ref = buf
        vt = vt_ref[0, t]
        for c in range(2):
            m = m_ref[c]
            m_new = jnp.maximum(m, cmax_ref[c])
            alpha = jnp.exp2(m - m_new)
            p = jnp.exp2(s_ref[c] - m_new).astype(_BF16)
            acc_ref[c] = alpha * acc_ref[c] + _dot(vt, p)
            m_ref[c] = m_new

    m_ref[...] = jnp.full(m_ref.shape, -jnp.inf, _F32)
    acc_ref[...] = jnp.zeros(acc_ref.shape, _F32)
    scores(0, bufs[0])

    def pairs(t, count):
        for k in range(count):
            scores(t + 2 * k + 1, bufs[1])
            consume(t + 2 * k, bufs[0])
            scores(t + 2 * k + 2, bufs[0])
            consume(t + 2 * k + 1, bufs[1])

    n_pairs = _div_pow2(n_full, 2)
    pairs_done = 0
    for size in TRIP_PAIRS:
        n_trips = _div_pow2(n_pairs - pairs_done, size)

        def trip(p, carry, size=size, first=pairs_done):
            pairs(2 * (first + p * size), size)
            return carry

        lax.fori_loop(0, n_trips, trip, 0)
        pairs_done = pairs_done + n_trips * size

    t0 = 2 * n_pairs
    two_left = n_full > t0

    @pl.when(two_left)
    def _():
        scores(n_full, bufs[1])
        consume(t0, bufs[0])
        consume(n_full, bufs[1])

    @pl.when(jnp.logical_not(two_left))
    def _():
        consume(t0, bufs[0])

    lam =_lambda_value(lq1_ref[...], lk1_ref[...], lq2_ref[...], lk2_ref[...], lambda_init)
    acc0 = acc_ref[0]
    acc1 = acc_ref[1]
    l0 = acc0[V_DIM:V_DIM + 1, :]
    l1 = acc1[V_DIM:V_DIM + 1, :]
    o = acc0[:V_DIM] / l0 - lam * (acc1[:V_DIM] / l1)
    ms = jnp.mean(o * o, axis=0, keepdims=True)
    on = o * lax.rsqrt(ms + LN_EPS)
    on = on.T * g_ref[...] * (1.0 - lambda_init)
    o_ref[...] = on.astype(_BF16)


def _attention_prompt(qa, qb, ka, kb, vt, lam_vecs, subln_g, lambda_init):
    s = ka.shape[1]
    assert K_TILE % Q_TILE == 0 and s % K_TILE == 0 and Q_TILE % CHUNK == 0
    nq = s // Q_TILE
    vec = pl.BlockSpec((1, QK_DIM), lambda h, i: (0, 0))
    q_tile = pl.BlockSpec((1, 2, Q_TILE, HEAD_W), lambda h, i: (h, 0, i, 0))
    return pl.pallas_call(
        partial(_attn_prompt_kernel, lambda_init=lambda_init),
        grid=(N_HEADS, nq),
        in_specs=[
            q_tile, q_tile,
            pl.BlockSpec((1, s, HEAD_W), lambda h, i: (h, 0, 0)),
            pl.BlockSpec((1, s, HEAD_W), lambda h, i: (h, 0, 0)),
            pl.BlockSpec((1, s // K_TILE, V_EXT, K_TILE), lambda h, i: (h, 0, 0, 0)),
            vec, vec, vec, vec,
            pl.BlockSpec((1, V_DIM), lambda h, i: (0, 0)),
        ],
        out_specs=pl.BlockSpec((Q_TILE, V_DIM), lambda h, i: (i, h)),
        out_shape=jax.ShapeDtypeStruct((s, ATTN_DIM), _BF16),
        scratch_shapes=[
            pltpu.VMEM((2, K_TILE, Q_TILE), _F32), pltpu.VMEM((2, K_TILE, Q_TILE), _F32),
            pltpu.VMEM((2, 1, Q_TILE), _F32), pltpu.VMEM((2, 1, Q_TILE), _F32),
            pltpu.VMEM((2, 1, Q_TILE), _F32), pltpu.VMEM((2, V_EXT, Q_TILE), _F32),
        ],
        compiler_params=pltpu.CompilerParams(
            dimension_semantics=("arbitrary", "arbitrary"), vmem_limit_bytes=_vmem_limit(48 << 20)),
        name="attention_prompt",
    )(qa, qb, ka, kb, vt, *lam_vecs, subln_g)


def _attn_sample_kernel(q_ref, kn_ref, vn_ref, ck_hbm, cv_hbm, lq1_ref, lk1_ref, lq2_ref, lk2_ref,
                        g_ref, o_ref, ck_buf, cv_buf, sem, *, lambda_init, past_len, n_new):
    b = pl.program_id(0)
    nb = pl.num_programs(0)

    def stream_copies(stream, slot):
        return (pltpu.make_async_copy(ck_hbm.at[stream], ck_buf.at[slot], sem.at[0, slot]),
                pltpu.make_async_copy(cv_hbm.at[stream], cv_buf.at[slot], sem.at[1, slot]))

    @pl.when(b == 0)
    def _():
        for s in range(CACHE_RING - 1):
            for copy in stream_copies(s, s):
                copy.start()

    ahead = b + (CACHE_RING - 1)

    @pl.when(ahead < nb)
    def _():
        for copy in stream_copies(ahead, lax.rem(ahead, CACHE_RING)):
            copy.start()

    slot = lax.rem(b, CACHE_RING)
    for copy in stream_copies(b, slot):
        copy.wait()
    ck_ref = ck_buf.at[slot]
    cv_ref = cv_buf.at[slot]

    group = 2 * N_HEADS
    width = group * n_new
    assert width == HEAD_W
    qb = q_ref[...].astype(_BF16)
    sel_r = lax.broadcasted_iota(jnp.int32, (n_new, width), 0)
    sel_c = lax.broadcasted_iota(jnp.int32, (n_new, width), 1)
    spread = jnp.where(_mod_pow2(sel_c, n_new) == sel_r, 1.0, 0.0).astype(_BF16)
    q_all = _dot_tn(qb, spread)
    blk_r = _div_pow2(lax.broadcasted_iota(jnp.int32, (ATTN_DIM, width), 0), QK_DIM)
    blk_c = _div_pow2(lax.broadcasted_iota(jnp.int32, (ATTN_DIM, width), 1), n_new)
    q_bd = jnp.where(blk_r == blk_c, q_all, 0.0).astype(_BF16)

    half = past_len // 2

    def heads_on_lanes(ref, part):
        return jnp.concatenate(
            [ref[pl.ds(part * half * N_HEADS + h, half, stride=N_HEADS), :].astype(_BF16)
             for h in range(N_HEADS)], axis=-1)

    s_old = jnp.concatenate([_dot(heads_on_lanes(ck_ref, part), q_bd) for part in range(2)],
                            axis=0)
    s_new = _dot(kn_ref[...].astype(_BF16), q_bd)

    def chunk_mask(shape, k_off):
        k_pos = k_off + lax.broadcasted_iota(jnp.int32, shape, 0)
        q_pos = past_len + _mod_pow2(lax.broadcasted_iota(jnp.int32, shape, 1), n_new)
        return _div_pow2(k_pos, CHUNK) <= _div_pow2(q_pos, CHUNK)

    s_old = jnp.where(chunk_mask(s_old.shape, 0), s_old, -jnp.inf)
    s_new = jnp.where(chunk_mask(s_new.shape, past_len), s_new, -jnp.inf)
    m = jnp.maximum(jnp.max(s_old, axis=0, keepdims=True), jnp.max(s_new, axis=0, keepdims=True))
    p_old = jnp.exp(s_old - m)
    p_new = jnp.exp(s_new - m)
    inv_l = 1.0 / (jnp.sum(p_old, axis=0, keepdims=True) + jnp.sum(p_new, axis=0, keepdims=True))
    a_old = (p_old * inv_l).astype(_BF16)
    a_new = (p_new * inv_l).astype(_BF16)
    pv = (_dot_tn(a_old[:half], heads_on_lanes(cv_ref, 0)) + _dot_tn(a_old[half:], heads_on_lanes(cv_ref, 1))
          + _dot_tn(a_new, vn_ref[...].astype(_BF16)))

    lam = _lambda_value(lq1_ref[...], lk1_ref[...], lq2_ref[...], lk2_ref[...], lambda_init)
    g = g_ref[...]
    for h in range(N_HEADS):
        cols = slice(h * V_DIM, (h + 1) * V_DIM)
        r0 = h * 2 * n_new
        o = pv[r0:r0 + n_new, cols] - lam * pv[r0 + n_new:r0 + 2 * n_new, cols]
        ms = jnp.mean(o * o, axis=-1, keepdims=True)
        o_ref[:, cols] = (o * lax.rsqrt(ms + LN_EPS) * g * (1.0 - lambda_init)).astype(_BF16)


def _attention_sample(q, k_new, v_new, cache_k, cache_v, lam_vecs, subln_g, lambda_init, n_new):
    nb, rows_per_stream = cache_k.shape[:2]
    past_len = rows_per_stream // N_HEADS
    assert nb >= CACHE_RING
    cache_block = pl.BlockSpec(memory_space=pl.ANY)
    rows = lambda b: (b, 0)
    vec = pl.BlockSpec((1, QK_DIM), lambda b: (0, 0))
    return pl.pallas_call(
        partial(_attn_sample_kernel, lambda_init=lambda_init, past_len=past_len, n_new=n_new),
        grid=(nb,),
        in_specs=[
            pl.BlockSpec((n_new, ATTN_DIM), rows),
            pl.BlockSpec((n_new, ATTN_DIM), rows),
            pl.BlockSpec((n_new, ATTN_DIM), rows),
            cache_block, cache_block,
            vec, vec, vec, vec,
            pl.BlockSpec((1, V_DIM), lambda b: (0, 0)),
        ],
        out_specs=pl.BlockSpec((n_new, ATTN_DIM), rows),
        out_shape=jax.ShapeDtypeStruct((nb * n_new, ATTN_DIM), _BF16),
        scratch_shapes=[
            pltpu.VMEM((CACHE_RING, rows_per_stream, HEAD_W), cache_k.dtype),
            pltpu.VMEM((CACHE_RING, rows_per_stream, V_DIM), cache_v.dtype),
            pltpu.SemaphoreType.DMA((2, CACHE_RING)),
        ],
        compiler_params=pltpu.CompilerParams(
            dimension_semantics=("arbitrary",), vmem_limit_bytes=_vmem_limit(40 << 20)),
        name="attention_sample",
    )(q, k_new, v_new, cache_k, cache_v, *lam_vecs, subln_g)


def _finish_kernel(x_ref, yconv_ref, attn_ref, wo_ref, g1_ref, b1_ref, w1_ref, w2_ref, g2_ref, b2_ref,
                   y_ref):
    tile = x_ref.shape[0]
    group = min(tile, FINISH_GROUP)
    groups = [slice(r0, r0 + group) for r0 in range(0, tile, group)]
    pre = [ALPHA * x_ref[rows, :]
           + _dot(jnp.concatenate([yconv_ref[rows, :], attn_ref[rows, :]], axis=-1), wo_ref[...])
           for rows in groups]
    for rows, z in zip(groups, pre):
        x1 = _layer_norm(z, g1_ref[...], b1_ref[...])
        x1b = x1.astype(_BF16)
        ff = jnp.zeros_like(x1)
        for c in range(D_FF // FF_CHUNK):
            cols = slice(c * FF_CHUNK, (c + 1) * FF_CHUNK)
            hdn = jnp.square(jnp.maximum(_dot(x1b, w1_ref[:, cols]), 0.0))
            ff = ff + _dot(hdn.astype(_BF16), w2_ref[cols, :])
        y_ref[rows, :] = _layer_norm(ALPHA * x1 + ff, g2_ref[...], b2_ref[...])


def _finish(x, yconv, attn, w_out, g1, b1, w_ff1, w_ff2, g2, b2):
    r = x.shape[0]
    tile = min(ROW_TILE, r)
    assert r % tile == 0
    row = lambda i: (i, 0)
    const = lambda shape: pl.BlockSpec(shape, lambda i: (0, 0))
    return pl.pallas_call(
        _finish_kernel,
        grid=(r // tile,),
        in_specs=[
            pl.BlockSpec((tile, D_MODEL), row),
            pl.BlockSpec((tile, CONV_DIM), row),
            pl.BlockSpec((tile, ATTN_DIM), row),
            const((D_MODEL, D_MODEL)), const((1, D_MODEL)), const((1, D_MODEL)),
            const((D_MODEL, D_FF)), const((D_FF, D_MODEL)), const((1, D_MODEL)), const((1, D_MODEL)),
        ],
        out_specs=pl.BlockSpec((tile, D_MODEL), row),
        out_shape=jax.ShapeDtypeStruct((r, D_MODEL), _F32),
        compiler_params=pltpu.CompilerParams(
            dimension_semantics=("arbitrary",), vmem_limit_bytes=_vmem_limit(56 << 20)),
        name="finish_layer",
    )(x, yconv, attn, w_out, g1, b1, w_ff1, w_ff2, g2, b2)


def kernel(x_prompt, x_sample, cache_k, cache_v, state_conv, w_in, conv_w, lambda_q1, lambda_k1,
           lambda_q2, lambda_k2, subln_g, w_out, ln1_g, ln1_b, w_ff1, w_ff2, ln2_g, ln2_b):
    bp, sp, _ = x_prompt.shape
    bs, ss, _ = x_sample.shape
    depth = w_in.shape[0]
    assert bp == 1 and depth == 1
    past_len = cache_k.shape[2]
    l = 0
    lambda_init = 0.8 - 0.6 * float(np.exp(-0.3 * l))

    lam_vecs = tuple(v[l].reshape(1, QK_DIM) for v in (lambda_q1, lambda_k1, lambda_q2, lambda_k2))
    g_sub = subln_g[l].reshape(1, V_DIM)
    ln = tuple(v[l].reshape(1, D_MODEL) for v in (ln1_g, ln1_b, ln2_g, ln2_b))

    xs2 = x_sample.reshape(bs * ss, D_MODEL)
    st = state_conv[l].astype(_F32)
    pad = lambda a: jnp.pad(a, ((0, 0), (0, ss - a.shape[1]), (0, 0))).reshape(bs * ss, CONV_DIM)
    hist2 = pad(st)
    hist1 = pad(st[:, 1:])
    yconv_s, q_s, k_s, v_s, u_s, w_in_b = _project_sample(xs2, w_in[l], conv_w[l], hist1, hist2, ss)

    xp2 = x_prompt.reshape(sp, D_MODEL)
    (yconv_p, k_p, v_p, qa, qb, ka, kb, vt, conv_p, w_out_b, w_ff1_b, w_ff2_b) = _project_prompt(
        xp2, w_in_b, conv_w[l], (w_out[l], w_ff1[l], w_ff2[l]))

    def finish(x2d, yconv, attn):
        return _finish(x2d, yconv, attn, w_out_b, ln[0], ln[1], w_ff1_b, w_ff2_b, ln[2], ln[3])

    attn_p = _attention_prompt(qa, qb, ka, kb, vt, lam_vecs, g_sub, lambda_init)
    y_prompt = finish(xp2, yconv_p, attn_p).reshape(bp, sp, D_MODEL)

    ck = cache_k.reshape(depth * bs, past_len * N_HEADS, HEAD_W)
    cv = cache_v.reshape(depth * bs, past_len * N_HEADS, V_DIM)
    attn_s = _attention_sample(q_s, k_s, v_s, ck, cv, lam_vecs, g_sub, lambda_init, ss)
    y_sample = finish(xs2, yconv_s, attn_s).reshape(bs, ss, D_MODEL)

    k_prompt = k_p.reshape(depth, bp, sp, N_HEADS, HEAD_W)
    v_prompt = v_p.reshape(depth, bp, sp, N_HEADS, V_DIM)
    conv_prompt = conv_p.reshape(depth, bp, CONV_W - 1, CONV_DIM)
    k_sample = k_s.reshape(depth, bs, ss, N_HEADS, HEAD_W)
    v_sample = v_s.reshape(depth, bs, ss, N_HEADS, V_DIM)
    conv_sample = u_s.reshape(bs, ss, CONV_DIM)[:, ss - (CONV_W - 1):].reshape(
        depth, bs, CONV_W - 1, CONV_DIM)
    return (y_prompt, y_sample, k_prompt, v_prompt, conv_prompt, k_sample, v_sample, conv_sample)
```
